```python
import jax, jax.numpy as jnp
from jax import lax
import numpy as np

D_MODEL = 1024
BATCH = 4
SEQ = 8192
DEPTH = 2
DEC_BATCH = 16
DEC_SEQ = 16
PAST_LEN = 2048

CHUNK = 64
Q_BLOCK = 128
HEAD_DIM = 64
SB_HEADS = 8
FOX_HEADS = 8
SB_WIDTH = SB_HEADS * HEAD_DIM
FOX_WIDTH = FOX_HEADS * HEAD_DIM
D_FF = 2816
N_EXPERTS = 8
TOP_K = 2
D_FF_EXPERT = 3584
N_DENSE = (DEPTH + 1) // 2
N_MOE = DEPTH // 2
RMS_EPS = 1e-6
FORGET_BIAS = 3.0
IN_WIDTHS = (SB_WIDTH, SB_WIDTH, SB_WIDTH, FOX_WIDTH, FOX_WIDTH, FOX_WIDTH, FOX_HEADS, 2 * D_MODEL)
IN_WIDTH = sum(IN_WIDTHS)

kernel_name = 'stickbreak_fox_gated_streaming_encoder_step'


def rms_norm(x, g):
    xf = x.astype(jnp.float32)
    y = xf * lax.rsqrt(jnp.mean(xf * xf, axis=-1, keepdims=True) + RMS_EPS)
    return y.astype(x.dtype) * g


def modulate(h, shift, scale):
    return h * (1 + scale[:, None, :]) + shift[:, None, :]


def swiglu(t, w_gate_up, w_down):
    a, b = jnp.split(t @ w_gate_up, 2, axis=-1)
    return (jax.nn.silu(a) * b) @ w_down


def moe_swiglu(h, w_router, w_gate_up, w_down):
    bsz, t, d = h.shape
    tok = h.reshape(bsz * t, d)
    logits = (tok @ w_router).astype(jnp.float32)
    top_logit, top_idx = lax.top_k(logits, TOP_K)
    top_w = jax.nn.softmax(top_logit, axis=-1)
    gates = jnp.sum(jax.nn.one_hot(top_idx, N_EXPERTS, dtype=jnp.float32) * top_w[..., None], axis=1)
    out = jnp.zeros_like(tok)
    for e in range(N_EXPERTS):
        out = out + gates[:, e:e + 1].astype(tok.dtype) * swiglu(tok, w_gate_up[e], w_down[e])
    return out.reshape(bsz, t, d)


def stick_breaking_attention(q, k, v, q_pos, k_pos):
    z = jnp.einsum('bqhd,bkhd->bhqk', q, k, preferred_element_type=jnp.float32) * HEAD_DIM ** -0.5
    visible = k_pos[None, :] < q_pos[:, None]
    log_keep = jnp.where(visible, -jax.nn.softplus(z), 0.0)
    later = lax.cumsum(log_keep, axis=3, reverse=True) - log_keep
    weights = jnp.where(visible, jnp.exp(jax.nn.log_sigmoid(z) + later), 0.0)
    return jnp.einsum('bhqk,bkhd->bqhd', weights, v).astype(v.dtype)


def forgetting_attention(q, k, v, f_q, f_k, q_pos, k_pos):
    s = jnp.einsum('bqhd,bkhd->bhqk', q, k, preferred_element_type=jnp.float32) * HEAD_DIM ** -0.5
    s = s + jnp.transpose(f_q, (0, 2, 1))[:, :, :, None] - jnp.transpose(f_k, (0, 2, 1))[:, :, None, :]
    s = jnp.where(k_pos[None, :] <= q_pos[:, None], s, -jnp.inf)
    p = jax.nn.softmax(s, axis=-1)
    return jnp.einsum('bhqk,bkhd->bqhd', p, v).astype(v.dtype)


def sweep_query_blocks(attend, q_args, q_pos):
    nb = q_pos.shape[0] // Q_BLOCK

    def to_blocks(a):
        return jnp.moveaxis(a.reshape(a.shape[0], nb, Q_BLOCK, *a.shape[2:]), 1, 0)

    blocks = tuple(to_blocks(a) for a in q_args) + (q_pos.reshape(nb, Q_BLOCK),)
    out = lax.map(lambda blk: attend(*blk), blocks)
    out = jnp.moveaxis(out, 0, 1)
    return out.reshape(out.shape[0], nb * Q_BLOCK, *out.shape[3:])


def token_mix(h, past, w_in, b_forget, w_sb_out, w_fox_out, w_out):
    bsz, t_new, _ = h.shape
    split_points = np.cumsum(IN_WIDTHS)[:-1].tolist()
    q_sb, k_sb, v_sb, q_fx, k_fx, v_fx, f_logit, g = jnp.split(h @ w_in, split_points, axis=-1)
    sb_heads = lambda a: a.reshape(bsz, t_new, SB_HEADS, HEAD_DIM)
    fx_heads = lambda a: a.reshape(bsz, t_new, FOX_HEADS, HEAD_DIM)
    q_sb, k_sb, v_sb = sb_heads(q_sb), sb_heads(k_sb), sb_heads(v_sb)
    q_fx, k_fx, v_fx = fx_heads(q_fx), fx_heads(k_fx), fx_heads(v_fx)
    log_f = jax.nn.log_sigmoid(f_logit.astype(jnp.float32) + b_forget.astype(jnp.float32))
    if past is None:
        n_past = 0
        ks, vs, kf, vf, lf = k_sb, v_sb, k_fx, v_fx, log_f
    else:
        c_ks, c_vs, c_kf, c_vf, c_lf = past
        n_past = c_ks.shape[1]
        ks = jnp.concatenate([c_ks, k_sb], axis=1)
        vs = jnp.concatenate([c_vs, v_sb], axis=1)
        kf = jnp.concatenate([c_kf, k_fx], axis=1)
        vf = jnp.concatenate([c_vf, v_fx], axis=1)
        lf = jnp.concatenate([c_lf.astype(jnp.float32), log_f], axis=1)
    cum_f = jnp.cumsum(lf, axis=1)
    f_q = cum_f[:, n_past:]
    k_pos = jnp.arange(n_past + t_new)
    q_pos = n_past + jnp.arange(t_new)
    sb_fn = lambda qb, pb: stick_breaking_attention(qb, ks, vs, pb, k_pos)
    fx_fn = lambda qb, fb, pb: forgetting_attention(qb, kf, vf, fb, cum_f, pb, k_pos)
    if past is None:
        o_sb = sweep_query_blocks(sb_fn, (q_sb,), q_pos)
        o_fx = sweep_query_blocks(fx_fn, (q_fx, f_q), q_pos)
    else:
        o_sb = sb_fn(q_sb, q_pos)
        o_fx = fx_fn(q_fx, f_q, q_pos)
    gate_sb, gate_fx = jnp.split(jax.nn.sigmoid(g), 2, axis=-1)
    merged = (gate_sb * (o_sb.reshape(bsz, t_new, SB_WIDTH) @ w_sb_out)
              + gate_fx * (o_fx.reshape(bsz, t_new, FOX_WIDTH) @ w_fox_out))
    return merged @ w_out, (k_sb, v_sb, k_fx, v_fx, log_f)


def trunk_layer(x, c, past, channel_mix, w_mod, b_mod, g_pre_mix, g_post_mix, g_pre_ffn, g_post_ffn,
                w_in, b_forget, w_sb_out, w_fox_out, w_out):
    shift1, scale1, gate1, shift2, scale2, gate2 = jnp.split(jax.nn.silu(c) @ w_mod + b_mod, 6, axis=-1)
    h = modulate(rms_norm(x, g_pre_mix), shift1, scale1)
    y, new_rows = token_mix(h, past, w_in, b_forget, w_sb_out, w_fox_out, w_out)
    x = x + gate1[:, None, :] * rms_norm(y, g_post_mix)
    h = modulate(rms_norm(x, g_pre_ffn), shift2, scale2)
    x = x + gate2[:, None, :] * rms_norm(channel_mix(h), g_post_ffn)
    return x, new_rows


def setup_inputs(seed: int = 0) -> dict:
    key = jax.random.key(seed)
    ks = jax.random.split(key, 32)

    def nrm(k, shape, scale):
        return scale * jax.random.normal(k, shape, jnp.float32)

    s = D_MODEL ** -0.5
    return {
        'x_prompt': nrm(ks[0], (BATCH, SEQ, D_MODEL), 1.0),
        'x_sample': nrm(ks[1], (DEC_BATCH, DEC_SEQ, D_MODEL), 1.0),
        'c_prompt': nrm(ks[2], (BATCH, D_MODEL), 1.0),
        'c_sample': nrm(ks[3], (DEC_BATCH, D_MODEL), 1.0),
        'cache_sb_k': nrm(ks[4], (DEPTH, DEC_BATCH, PAST_LEN, SB_HEADS, HEAD_DIM), 1.0),
        'cache_sb_v': nrm(ks[5], (DEPTH, DEC_BATCH, PAST_LEN, SB_HEADS, HEAD_DIM), 1.0),
        'cache_fox_k': nrm(ks[6], (DEPTH, DEC_BATCH, PAST_LEN, FOX_HEADS, HEAD_DIM), 1.0),
        'cache_fox_v': nrm(ks[7], (DEPTH, DEC_BATCH, PAST_LEN, FOX_HEADS, HEAD_DIM), 1.0),
        'cache_fox_logf': jax.nn.log_sigmoid(FORGET_BIAS + nrm(ks[8], (DEPTH, DEC_BATCH, PAST_LEN, FOX_HEADS), 1.0)),
        'w_mod': nrm(ks[9], (DEPTH, D_MODEL, 6 * D_MODEL), 0.5 * s),
        'b_mod': nrm(ks[10], (DEPTH, 6 * D_MODEL), 0.02),
        'g_pre_mix': 1.0 + nrm(ks[11], (DEPTH, D_MODEL), 0.02),
        'g_post_mix': 1.0 + nrm(ks[12], (DEPTH, D_MODEL), 0.02),
        'g_pre_ffn': 1.0 + nrm(ks[13], (DEPTH, D_MODEL), 0.02),
        'g_post_ffn': 1.0 + nrm(ks[14], (DEPTH, D_MODEL), 0.02),
        'w_in': nrm(ks[15], (DEPTH, D_MODEL, IN_WIDTH), s),
        'b_forget': FORGET_BIAS + nrm(ks[16], (DEPTH, FOX_HEADS), 0.5),
        'w_sb_out': nrm(ks[17], (DEPTH, SB_WIDTH, D_MODEL), SB_WIDTH ** -0.5),
        'w_fox_out': nrm(ks[18], (DEPTH, FOX_WIDTH, D_MODEL), FOX_WIDTH ** -0.5),
        'w_out': nrm(ks[19], (DEPTH, D_MODEL, D_MODEL), s),
        'w_ffn_gate_up': nrm(ks[20], (N_DENSE, D_MODEL, 2 * D_FF), s),
        'w_ffn_down': nrm(ks[21], (N_DENSE, D_FF, D_MODEL), D_FF ** -0.5),
        'w_router': nrm(ks[22], (N_MOE, D_MODEL, N_EXPERTS), s),
        'w_moe_gate_up': nrm(ks[23], (N_MOE, N_EXPERTS, D_MODEL, 2 * D_FF_EXPERT), s),
        'w_moe_down': nrm(ks[24], (N_MOE, N_EXPERTS, D_FF_EXPERT, D_MODEL), D_FF_EXPERT ** -0.5),
    }


def reference(x_prompt, x_sample, c_prompt, c_sample, cache_sb_k, cache_sb_v, cache_fox_k, cache_fox_v,
              cache_fox_logf, w_mod, b_mod, g_pre_mix, g_post_mix, g_pre_ffn, g_post_ffn, w_in, b_forget,
              w_sb_out, w_fox_out, w_out, w_ffn_gate_up, w_ffn_down, w_router, w_moe_gate_up, w_moe_down):
    xp, xs = x_prompt, x_sample
    rows_p, rows_s = [], []
    for l in range(DEPTH):
        if l % 2 == 0:
            wgu, wd = w_ffn_gate_up[l // 2], w_ffn_down[l // 2]
            channel_mix = lambda h: swiglu(h, wgu, wd)
        else:
            wr, wgu, wd = w_router[l // 2], w_moe_gate_up[l // 2], w_moe_down[l // 2]
            channel_mix = lambda h: moe_swiglu(h, wr, wgu, wd)
        layer_w = (w_mod[l], b_mod[l], g_pre_mix[l], g_post_mix[l], g_pre_ffn[l], g_post_ffn[l],
                   w_in[l], b_forget[l], w_sb_out[l], w_fox_out[l], w_out[l])
        xp, new_p = trunk_layer(xp, c_prompt, None, channel_mix, *layer_w)
        past = (cache_sb_k[l], cache_sb_v[l], cache_fox_k[l], cache_fox_v[l], cache_fox_logf[l])
        xs, new_s = trunk_layer(xs, c_sample, past, channel_mix, *layer_w)
        rows_p.append(new_p)
        rows_s.append(new_s)
    stack = lambda rows, i: jnp.stack([r[i] for r in rows], axis=0)
    return (xp, xs,
            stack(rows_p, 0), stack(rows_p, 1), stack(rows_p, 2), stack(rows_p, 3), stack(rows_p, 4),
            stack(rows_s, 0), stack(rows_s, 1), stack(rows_s, 2), stack(rows_s, 3), stack(rows_s, 4))
```

```python
import functools

import jax
import jax.numpy as jnp
from jax import lax
from jax.experimental import pallas as pl
from jax.experimental.pallas import tpu as pltpu

F32 = jnp.float32
BF16 = jnp.bfloat16

HEAD_DIM = 64
N_HEADS = 8
LANES = 128
HEAD_PAIRS = N_HEADS * HEAD_DIM // LANES
WIDTH = N_HEADS * HEAD_DIM
TOP_K = 2
RMS_EPS = 1e-6
SB_DEAD_LOG = -104.0
VMEM_LIMIT = 56 * 1024 * 1024


def _params(sem, vmem=VMEM_LIMIT):
    return pltpu.CompilerParams(dimension_semantics=sem, vmem_limit_bytes=vmem)


def _dot(a, b):
    return jnp.dot(a, b, preferred_element_type=F32)


def _dot_nt(a, b):
    return lax.dot_general(a, b, (((1,), (1,)), ((), ())), preferred_element_type=F32)


def _sigmoid(x):
    return 1.0 / (1.0 + jnp.exp(-x))


def _softplus(x):
    return jnp.maximum(x, 0.0) + jnp.log(1.0 + jnp.exp(-jnp.abs(x)))


def _rms(x, g):
    return x * lax.rsqrt(jnp.mean(x * x, axis=-1, keepdims=True) + RMS_EPS) * g


def _mod_kernel(c_ref, w_ref, b_ref, o_ref):
    c = c_ref[...]
    s = c * _sigmoid(c)
    o_ref[0] = jnp.dot(s, w_ref[0], precision=lax.Precision.HIGHEST, preferred_element_type=F32) + b_ref[0]


def _modulation(c_all, w_mod, b_mod):
    depth, d, d6 = w_mod.shape
    rows = c_all.shape[0]
    tn = 1024
    return pl.pallas_call(
        _mod_kernel,
        grid=(depth, d6 // tn),
        in_specs=[pl.BlockSpec((rows, d), lambda l, j: (0, 0)),
                  pl.BlockSpec((1, d, tn), lambda l, j: (l, 0, j)),
                  pl.BlockSpec((1, 1, tn), lambda l, j: (l, 0, j))],
        out_specs=pl.BlockSpec((1, rows, tn), lambda l, j: (l, 0, j)),
        out_shape=jax.ShapeDtypeStruct((depth, rows, d6), F32),
        compiler_params=_params(("arbitrary", "arbitrary")),
        name="modulation",
    )(c_all, w_mod, b_mod.reshape(depth, 1, d6))


def _mod_spec(mod, tm, tiles_per_group):
    _, r, d = mod.shape
    return pl.BlockSpec((1, r, d), lambda i, *_: (i // tiles_per_group, 0, 0))


def _inproj_kernel(x_ref, g_ref, shift_ref, scale_ref, wqkv_ref, wf_ref, wg_ref, bf_ref,
                   qkv_ref, ksb_ref, vsb_ref, kfx_ref, vfx_ref, lf_ref, sg_ref):
    h = _rms(x_ref[...], g_ref[...]) * (1.0 + scale_ref[0]) + shift_ref[0]
    hb = h.astype(BF16)
    f32_outs = {1: ksb_ref, 2: vsb_ref, 4: kfx_ref, 5: vfx_ref}
    for c in range(6):
        cols = slice(c * WIDTH, (c + 1) * WIDTH)
        acc = _dot(hb, wqkv_ref[:, cols])
        if c in f32_outs:
            f32_outs[c][...] = acc
            qkv_ref[:, cols] = acc.astype(BF16)
        else:
            qkv_ref[:, cols] = (acc * HEAD_DIM ** -0.5).astype(BF16)
    f = _dot(hb, wf_ref[...])[:, :N_HEADS] + bf_ref[...]
    lf_ref[...] = -_softplus(-f)
    for c in range(wg_ref.shape[1] // WIDTH):
        cols = slice(c * WIDTH, (c + 1) * WIDTH)
        sg_ref[:, cols] = _sigmoid(_dot(hb, wg_ref[:, cols])).astype(BF16)


def _in_projection(x, g_pre, shift, scale, wqkv, wf, wg, b_forget, tm):
    n, d = x.shape
    tiles_per_group = (n // tm) // shift.shape[0]
    const = lambda i: (0, 0)
    row = lambda i: (i, 0)
    return pl.pallas_call(
        _inproj_kernel,
        grid=(n // tm,),
        in_specs=[pl.BlockSpec((tm, d), row),
                  pl.BlockSpec((1, d), const),
                  _mod_spec(shift, tm, tiles_per_group),
                  _mod_spec(scale, tm, tiles_per_group),
                  pl.BlockSpec(wqkv.shape, const),
                  pl.BlockSpec(wf.shape, const),
                  pl.BlockSpec(wg.shape, const),
                  pl.BlockSpec((1, N_HEADS), const)],
        out_specs=[pl.BlockSpec((tm, 6 * WIDTH), row)] + [pl.BlockSpec((tm, WIDTH), row)] * 4
                  + [pl.BlockSpec((tm, N_HEADS), row), pl.BlockSpec((tm, wg.shape[1]), row)],
        out_shape=[jax.ShapeDtypeStruct((n, 6 * WIDTH), BF16)] + [jax.ShapeDtypeStruct((n, WIDTH), F32)] * 4
                  + [jax.ShapeDtypeStruct((n, N_HEADS), F32), jax.ShapeDtypeStruct((n, wg.shape[1]), BF16)],
        compiler_params=_params(("arbitrary",)),
        name="in_projection",
    )(x, g_pre, shift, scale, wqkv, wf, wg, b_forget)


CUMSUM_CHUNK = 256


def _cumsum_kernel(x_ref, o_ref):
    t = x_ref.shape[2]
    r = lax.broadcasted_iota(jnp.int32, (CUMSUM_CHUNK, CUMSUM_CHUNK), 0)
    c = lax.broadcasted_iota(jnp.int32, (CUMSUM_CHUNK, CUMSUM_CHUNK), 1)
    upper = jnp.where(r <= c, 1.0, 0.0).astype(F32)

    def step(i, carry):
        start = pl.multiple_of(i * CUMSUM_CHUNK, CUMSUM_CHUNK)
        seg = x_ref[0, :, pl.ds(start, CUMSUM_CHUNK)]
        cs = jnp.dot(seg, upper, precision=lax.Precision.HIGHEST, preferred_element_type=F32) + carry
        o_ref[0, :, pl.ds(start, CUMSUM_CHUNK)] = cs
        return cs[:, CUMSUM_CHUNK - 1:CUMSUM_CHUNK]

    lax.fori_loop(0, t // CUMSUM_CHUNK, step, jnp.zeros((x_ref.shape[1], 1), F32))


def _cumsum_lanes(x):
    b, r, t = x.shape
    return pl.pallas_call(
        _cumsum_kernel,
        grid=(b,),
        in_specs=[pl.BlockSpec((1, r, t), lambda i: (i, 0, 0))],
        out_specs=pl.BlockSpec((1, r, t), lambda i: (i, 0, 0)),
        out_shape=jax.ShapeDtypeStruct((b, r, t), F32),
        compiler_params=_params(("arbitrary",)),
        name="forget_cumsum",
    )(x)


def _pair_layouts(cum_t, t_query_start, t_query):
    b, _, tp = cum_t.shape
    f_keys = cum_t.reshape(b, HEAD_PAIRS, 2, tp)
    f_query = jnp.swapaxes(f_keys[:, :, :, t_query_start:t_query_start + t_query], 2, 3)
    return f_keys, f_query


def _head_masks():
    lane = lax.broadcasted_iota(jnp.int32, (1, LANES), 1)
    return lane < HEAD_DIM


def _split_heads(q2, first_head):
    zero = jnp.zeros_like(q2)
    return jnp.where(first_head, q2, zero), jnp.where(first_head, zero, q2)


def _strict_lower_neg(n):
    r = lax.broadcasted_iota(jnp.int32, (n, n), 0)
    c = lax.broadcasted_iota(jnp.int32, (n, n), 1)
    return jnp.where(r > c, -1.0, 0.0).astype(BF16)


def _sb_block(qh, k2, v2, neg_tri, visible, acc_ref, r_ref, hd):
    z = _dot_nt(qh, k2)
    sp = _softplus(z)
    if visible is not None:
        sp = jnp.where(visible, sp, 0.0)
    later = _dot(sp.astype(BF16), neg_tri)
    p = jnp.exp(z - sp + later)
    if visible is not None:
        p = jnp.where(visible, p, 0.0)
    r = r_ref[hd]
    acc_ref[hd] += jnp.exp(r) * _dot(p.astype(BF16), v2)
    r_ref[hd] = r + later[:, 0:1] - sp[:, 0:1]


def _fox_block(qh, k2, v2_ones, fq, fk, allowed, acc_ref, m_ref, hd):
    s = _dot_nt(qh, k2) + fq - fk
    if allowed is not None:
        s = jnp.where(allowed, s, -jnp.inf)
    m_old = m_ref[hd]
    m_new = jnp.maximum(m_old, jnp.max(s, axis=-1, keepdims=True))
    p = jnp.exp(s - m_new)
    acc_ref[hd] = jnp.exp(m_old - m_new) * acc_ref[hd] + _dot(p.astype(BF16), v2_ones)
    m_ref[hd] = m_new


def _fox_finish(acc_ref, first_head):
    a, b = acc_ref[0], acc_ref[1]
    num = jnp.where(first_head, a, b)
    den = jnp.where(first_head, pltpu.roll(a, HEAD_DIM, 1), pltpu.roll(b, HEAD_DIM, 1))
    return num / den


def _with_ones(v2, first_head):
    one = jnp.ones_like(v2)
    return jnp.where(first_head, v2, one), jnp.where(first_head, one, v2)


def _sb_prompt_kernel(q_ref, k_ref, v_ref, o_ref, acc_ref, r_ref, *, tq):
    i = pl.program_id(2)
    first_head = _head_masks()
    qs = _split_heads(q_ref[0], first_head)
    neg_tri = _strict_lower_neg(tq)
    row = lax.broadcasted_iota(jnp.int32, (tq, tq), 0)
    col = lax.broadcasted_iota(jnp.int32, (tq, tq), 1)
    visible = col < row
    acc_ref[...] = jnp.zeros_like(acc_ref)
    r_ref[...] = jnp.zeros_like(r_ref)

    def block(j, vis):
        start = pl.multiple_of(j * tq, tq)
        k2 = k_ref[0, pl.ds(start, tq), :]
        v2 = v_ref[0, pl.ds(start, tq), :]
        for hd in range(2):
            _sb_block(qs[hd], k2, v2, neg_tri, vis, acc_ref, r_ref, hd)

    def live():
        return jnp.max(jnp.maximum(r_ref[0], r_ref[1]))

    block(i, visible)

    def cond(carry):
        j, r_max = carry
        return jnp.logical_and(j >= 0, r_max > SB_DEAD_LOG)

    def body(carry):
        j, _ = carry
        block(j, None)
        return j - 1, live()

    lax.while_loop(cond, body, (i - 1, live()))
    o_ref[0] = jnp.where(first_head, acc_ref[0], acc_ref[1]).astype(o_ref.dtype)


def _sb_prompt(qkv, tq):
    b, t, _ = qkv.shape
    kernel = functools.partial(_sb_prompt_kernel, tq=tq)
    return pl.pallas_call(
        kernel,
        grid=(b, HEAD_PAIRS, t // tq),
        in_specs=[pl.BlockSpec((1, tq, LANES), lambda b_, p, i: (b_, i, p)),
                  pl.BlockSpec((1, t, LANES), lambda b_, p, i: (b_, 0, HEAD_PAIRS + p)),
                  pl.BlockSpec((1, t, LANES), lambda b_, p, i: (b_, 0, 2 * HEAD_PAIRS + p))],
        out_specs=pl.BlockSpec((1, tq, LANES), lambda b_, p, i: (b_, i, p)),
        out_shape=jax.ShapeDtypeStruct((b, t, WIDTH), BF16),
        scratch_shapes=[pltpu.VMEM((2, tq, LANES), F32), pltpu.VMEM((2, tq, 1), F32)],
        compiler_params=_params(("arbitrary", "arbitrary", "arbitrary")),
        name="sb_attention_prompt",
    )(qkv, qkv, qkv)


def _fox_prompt_kernel(q_ref, k_ref, v_ref, fq_ref, fk_ref, o_ref, acc_ref, m_ref, *, tq):
    i = pl.program_id(2)
    first_head = _head_masks()
    qs = _split_heads(q_ref[0], first_head)
    row = lax.broadcasted_iota(jnp.int32, (tq, tq), 0)
    col = lax.broadcasted_iota(jnp.int32, (tq, tq), 1)
    allowed = col <= row
    acc_ref[...] = jnp.zeros_like(acc_ref)
    m_ref[...] = jnp.full_like(m_ref, -jnp.inf)
    fq = fq_ref[0, 0]

    def block(j, mask):
        start = pl.multiple_of(j * tq, tq)
        k2 = k_ref[0, pl.ds(start, tq), :]
        vs = _with_ones(v_ref[0, pl.ds(start, tq), :], first_head)
        for hd in range(2):
            fk = fk_ref[0, 0, hd:hd + 1, pl.ds(start, tq)]
            _fox_block(qs[hd], k2, vs[hd], fq[:, hd:hd + 1], fk, mask, acc_ref, m_ref, hd)

    block(i, allowed)

    def body(n, carry):
        block(i - 1 - n, None)
        return carry

    lax.fori_loop(0, i, body, 0)
    o_ref[0] = _fox_finish(acc_ref, first_head).astype(o_ref.dtype)


def _fox_prompt(qkv, f_query, f_keys, tq):
    b, t, _ = qkv.shape
    kernel = functools.partial(_fox_prompt_kernel, tq=tq)
    return pl.pallas_call(
        kernel,
        grid=(b, HEAD_PAIRS, t // tq),
        in_specs=[pl.BlockSpec((1, tq, LANES), lambda b_, p, i: (b_, i, 3 * HEAD_PAIRS + p)),
                  pl.BlockSpec((1, t, LANES), lambda b_, p, i: (b_, 0, 4 * HEAD_PAIRS + p)),
                  pl.BlockSpec((1, t, LANES), lambda b_, p, i: (b_, 0, 5 * HEAD_PAIRS + p)),
                  pl.BlockSpec((1, 1, tq, 2), lambda b_, p, i: (b_, p, i, 0)),
                  pl.BlockSpec((1, 1, 2, t), lambda b_, p, i: (b_, p, 0, 0))],
        out_specs=pl.BlockSpec((1, tq, LANES), lambda b_, p, i: (b_, i, p)),
        out_shape=jax.ShapeDtypeStruct((b, t, WIDTH), BF16),
        scratch_shapes=[pltpu.VMEM((2, tq, LANES), F32), pltpu.VMEM((2, tq, 1), F32)],
        compiler_params=_params(("arbitrary", "arbitrary", "arbitrary")),
        name="fox_attention_prompt",
    )(qkv, qkv, qkv, f_query, f_keys)


def _sample_attn_kernel(qsb_ref, ksb_ref, vsb_ref, qfx_ref, kfx_ref, vfx_ref,
                        csk_ref, csv_ref, cfk_ref, cfv_ref, fq_ref, fk_ref,
                        osb_ref, ofx_ref, acc_ref, st_ref, *, tk):
    t_new = qsb_ref.shape[1]
    n_past = csk_ref.shape[1]
    n_blocks = n_past // tk
    first_head = _head_masks()
    row = lax.broadcasted_iota(jnp.int32, (t_new, t_new), 0)
    col = lax.broadcasted_iota(jnp.int32, (t_new, t_new), 1)

    qs = _split_heads(qsb_ref[0], first_head)
    acc_ref[...] = jnp.zeros_like(acc_ref)
    st_ref[...] = jnp.zeros_like(st_ref)
    tri_new = _strict_lower_neg(t_new)
    tri_past = _strict_lower_neg(tk)
    for hd in range(2):
        _sb_block(qs[hd], ksb_ref[0], vsb_ref[0], tri_new, col < row, acc_ref, st_ref, hd)

    def live():
        return jnp.max(jnp.maximum(st_ref[0], st_ref[1]))

    def sb_cond(carry):
        j, r_max = carry
        return jnp.logical_and(j >= 0, r_max > SB_DEAD_LOG)

    def sb_body(carry):
        j, _ = carry
        start = pl.multiple_of(j * tk, tk)
        k2 = csk_ref[0, pl.ds(start, tk), :].astype(BF16)
        v2 = csv_ref[0, pl.ds(start, tk), :].astype(BF16)
        for hd in range(2):
            _sb_block(qs[hd], k2, v2, tri_past, None, acc_ref, st_ref, hd)
        return j - 1, live()

    lax.while_loop(sb_cond, sb_body, (n_blocks - 1, live()))
    osb_ref[0] = jnp.where(first_head, acc_ref[0], acc_ref[1]).astype(osb_ref.dtype)

    qs = _split_heads(qfx_ref[0], first_head)
    acc_ref[...] = jnp.zeros_like(acc_ref)
    st_ref[...] = jnp.full_like(st_ref, -jnp.inf)
    fq = fq_ref[0, 0]
    vs = _with_ones(vfx_ref[0], first_head)
    for hd in range(2):
        fk = fk_ref[0, 0, hd:hd + 1, n_past:n_past + t_new]
        _fox_block(qs[hd], kfx_ref[0], vs[hd], fq[:, hd:hd + 1], fk, col <= row, acc_ref, st_ref, hd)

    def fx_body(j, carry):
        start = pl.multiple_of(j * tk, tk)
        k2 = cfk_ref[0, pl.ds(start, tk), :].astype(BF16)
        vs_ = _with_ones(cfv_ref[0, pl.ds(start, tk), :].astype(BF16), first_head)
        for hd in range(2):
            fk = fk_ref[0, 0, hd:hd + 1, pl.ds(start, tk)]
            _fox_block(qs[hd], k2, vs_[hd], fq[:, hd:hd + 1], fk, None, acc_ref, st_ref, hd)
        return carry

    lax.fori_loop(0, n_blocks, fx_body, 0)
    ofx_ref[0] = _fox_finish(acc_ref, first_head).astype(ofx_ref.dtype)


def _sample_attention(qkv, cache_sb_k, cache_sb_v, cache_fox_k, cache_fox_v, f_query, f_keys, tk):
    b, t_new, _ = qkv.shape
    n_past = cache_sb_k.shape[1]
    tp = f_keys.shape[3]
    new = lambda c: pl.BlockSpec((1, t_new, LANES), lambda b_, p: (b_, 0, c * HEAD_PAIRS + p))
    past = pl.BlockSpec((1, n_past, LANES), lambda b_, p: (b_, 0, p))
    out = pl.BlockSpec((1, t_new, LANES), lambda b_, p: (b_, 0, p))
    kernel = functools.partial(_sample_attn_kernel, tk=tk)
    return pl.pallas_call(
        kernel,
        grid=(b, HEAD_PAIRS),
        in_specs=[new(0), new(1), new(2), new(3), new(4), new(5), past, past, past, past,
                  pl.BlockSpec((1, 1, t_new, 2), lambda b_, p: (b_, p, 0, 0)),
                  pl.BlockSpec((1, 1, 2, tp), lambda b_, p: (b_, p, 0, 0))],
        out_specs=[out, out],
        out_shape=[jax.ShapeDtypeStruct((b, t_new, WIDTH), BF16)] * 2,
        scratch_shapes=[pltpu.VMEM((2, t_new, LANES), F32), pltpu.VMEM((2, t_new, 1), F32)],
        compiler_params=_params(("arbitrary", "arbitrary")),
        name="attention_sample",
    )(qkv, qkv, qkv, qkv, qkv, qkv, cache_sb_k, cache_sb_v, cache_fox_k, cache_fox_v, f_query, f_keys)


def _mix_out_kernel(x_ref, osb_ref, ofx_ref, sg_ref, gate_ref, g_ref, wsb_ref, wfx_ref, wo_ref, o_ref):
    d = x_ref.shape[1]
    merged = (sg_ref[:, :d].astype(F32) * _dot(osb_ref[...], wsb_ref[...])
              + sg_ref[:, d:].astype(F32) * _dot(ofx_ref[...], wfx_ref[...]))
    y = _dot(merged.astype(BF16), wo_ref[...])
    o_ref[...] = x_ref[...] + gate_ref[0] * _rms(y, g_ref[...])


def _mix_out(x, o_sb, o_fx, sg, gate, g_post, w_sb_out, w_fox_out, w_out, tm):
    n, d = x.shape
    tiles_per_group = (n // tm) // gate.shape[0]
    const = lambda i: (0, 0)
    row = lambda i: (i, 0)
    return pl.pallas_call(
        _mix_out_kernel,
        grid=(n // tm,),
        in_specs=[pl.BlockSpec((tm, d), row), pl.BlockSpec((tm, WIDTH), row), pl.BlockSpec((tm, WIDTH), row),
                  pl.BlockSpec((tm, 2 * d), row), _mod_spec(gate, tm, tiles_per_group),
                  pl.BlockSpec((1, d), const), pl.BlockSpec(w_sb_out.shape, const),
                  pl.BlockSpec(w_fox_out.shape, const), pl.BlockSpec(w_out.shape, const)],
        out_specs=pl.BlockSpec((tm, d), row),
        out_shape=jax.ShapeDtypeStruct((n, d), F32),
        compiler_params=_params(("arbitrary",)),
        name="mix_out",
    )(x, o_sb, o_fx, sg, gate, g_post, w_sb_out, w_fox_out, w_out)


def _swiglu_chunk(hb, wa, wb, wd):
    a = _dot(hb, wa)
    b = _dot(hb, wb)
    return a * _sigmoid(a) * b, wd


def _ffn_kernel(x_ref, gpre_ref, shift_ref, scale_ref, gate_ref, gpost_ref, wa_ref, wb_ref, wd_ref,
                o_ref, h_ref, acc_ref):
    f = pl.program_id(1)

    @pl.when(f == 0)
    def _():
        h = _rms(x_ref[...], gpre_ref[...]) * (1.0 + scale_ref[0]) + shift_ref[0]
        h_ref[...] = h.astype(BF16)
        acc_ref[...] = jnp.zeros_like(acc_ref)

    hb = h_ref[...]
    a = _dot(hb, wa_ref[...])
    b = _dot(hb, wb_ref[...])
    act = a * _sigmoid(a) * b
    acc_ref[...] += _dot(act.astype(BF16), wd_ref[...])

    @pl.when(f == pl.num_programs(1) - 1)
    def _():
        o_ref[...] = x_ref[...] + gate_ref[0] * _rms(acc_ref[...], gpost_ref[...])


def _ffn(x, g_pre, shift, scale, gate, g_post, w_gate_up, w_down, tm, tf):
    n, d = x.shape
    d_ff = w_down.shape[0]
    nf = d_ff // tf
    tiles_per_group = (n // tm) // shift.shape[0]
    const = lambda i, f: (0, 0)
    row = lambda i, f: (i, 0)
    return pl.pallas_call(
        _ffn_kernel,
        grid=(n // tm, nf),
        in_specs=[pl.BlockSpec((tm, d), row), pl.BlockSpec((1, d), const),
                  _mod_spec(shift, tm, tiles_per_group), _mod_spec(scale, tm, tiles_per_group),
                  _mod_spec(gate, tm, tiles_per_group), pl.BlockSpec((1, d), const),
                  pl.BlockSpec((d, tf), lambda i, f: (0, f)),
                  pl.BlockSpec((d, tf), lambda i, f: (0, nf + f)),
                  pl.BlockSpec((tf, d), lambda i, f: (f, 0))],
        out_specs=pl.BlockSpec((tm, d), row),
        out_shape=jax.ShapeDtypeStruct((n, d), F32),
        scratch_shapes=[pltpu.VMEM((tm, d), BF16), pltpu.VMEM((tm, d), F32)],
        compiler_params=_params(("arbitrary", "arbitrary")),
        name="ffn_dense",
    )(x, g_pre, shift, scale, gate, g_post, w_gate_up, w_gate_up, w_down)


def _top2_gates(logits):
    n_e = logits.shape[1]
    idx = lax.broadcasted_iota(jnp.int32, logits.shape, 1)
    m1 = jnp.max(logits, axis=-1, keepdims=True)
    i1 = jnp.min(jnp.where(logits == m1, idx, n_e), axis=-1, keepdims=True)
    rest = jnp.where(idx == i1, -jnp.inf, logits)
    m2 = jnp.max(rest, axis=-1, keepdims=True)
    i2 = jnp.min(jnp.where(rest == m2, idx, n_e), axis=-1, keepdims=True)
    e2 = jnp.exp(m2 - m1)
    w1 = 1.0 / (1.0 + e2)
    w2 = e2 / (1.0 + e2)
    return jnp.where(idx == i1, w1, 0.0) + jnp.where(idx == i2, w2, 0.0)


def _moe_kernel(x_ref, gpre_ref, shift_ref, scale_ref, gate_ref, gpost_ref, wr_ref, wa_ref, wb_ref, wd_ref,
                o_ref, h_ref, acc_ref, gates_ref):
    e = pl.program_id(1)
    f = pl.program_id(2)

    @pl.when(jnp.logical_and(e == 0, f == 0))
    def _():
        h = _rms(x_ref[...], gpre_ref[...]) * (1.0 + scale_ref[0]) + shift_ref[0]
        h_ref[...] = h.astype(BF16)
        acc_ref[...] = jnp.zeros_like(acc_ref)
        logits = jnp.dot(h, wr_ref[...], precision=lax.Precision.HIGHEST, preferred_element_type=F32)
        gates_ref[...] = _top2_gates(logits[:, :gates_ref.shape[1]])

    hb = h_ref[...]
    a = _dot(hb, wa_ref[0])
    b = _dot(hb, wb_ref[0])
    n_e = gates_ref.shape[1]
    lane = lax.broadcasted_iota(jnp.int32, (1, n_e), 1)
    g = jnp.sum(jnp.where(lane == e, gates_ref[...], 0.0), axis=-1, keepdims=True)
    act = a * _sigmoid(a) * b * g
    acc_ref[...] += _dot(act.astype(BF16), wd_ref[0])

    @pl.when(jnp.logical_and(e == pl.num_programs(1) - 1, f == pl.num_programs(2) - 1))
    def _():
        o_ref[...] = x_ref[...] + gate_ref[0] * _rms(acc_ref[...], gpost_ref[...])


def _moe(x, g_pre, shift, scale, gate, g_post, w_router, w_gate_up, w_down, n_experts, tm, tf):
    n, d = x.shape
    d_ff = w_down.shape[1]
    nf = d_ff // tf
    tiles_per_group = (n // tm) // shift.shape[0]
    const = lambda i, e, f: (0, 0)
    row = lambda i, e, f: (i, 0)
    return pl.pallas_call(
        _moe_kernel,
        grid=(n // tm, n_experts, nf),
        in_specs=[pl.BlockSpec((tm, d), row), pl.BlockSpec((1, d), const),
                  _mod_spec(shift, tm, tiles_per_group), _mod_spec(scale, tm, tiles_per_group),
                  _mod_spec(gate, tm, tiles_per_group), pl.BlockSpec((1, d), const),
                  pl.BlockSpec(w_router.shape, const),
                  pl.BlockSpec((1, d, tf), lambda i, e, f: (e, 0, f)),
                  pl.BlockSpec((1, d, tf), lambda i, e, f: (e, 0, nf + f)),
                  pl.BlockSpec((1, tf, d), lambda i, e, f: (e, f, 0))],
        out_specs=pl.BlockSpec((tm, d), row),
        out_shape=jax.ShapeDtypeStruct((n, d), F32),
        scratch_shapes=[pltpu.VMEM((tm, d), BF16), pltpu.VMEM((tm, d), F32), pltpu.VMEM((tm, n_experts), F32)],
        compiler_params=_params(("arbitrary", "arbitrary", "arbitrary")),
        name="ffn_experts",
    )(x, g_pre, shift, scale, gate, g_post, w_router, w_gate_up, w_gate_up, w_down)


def _row_tile(n, want):
    return want if n % want == 0 else n


def kernel(x_prompt, x_sample, c_prompt, c_sample, cache_sb_k, cache_sb_v, cache_fox_k, cache_fox_v, cache_fox_logf, w_mod, b_mod, g_pre_mix, g_post_mix, g_pre_ffn, g_post_ffn, w_in, b_forget, w_sb_out, w_fox_out, w_out, w_ffn_gate_up, w_ffn_down, w_router, w_moe_gate_up, w_moe_down):
    bsz, seq, d = x_prompt.shape
    dec_b, dec_t, _ = x_sample.shape
    depth = w_mod.shape[0]
    n_past = cache_sb_k.shape[2]
    n_experts = w_router.shape[2]
    n_p, n_s = bsz * seq, dec_b * dec_t
    tq = _row_tile(seq, 256)
    past_chunk = _row_tile(n_past, 256)

    c_all = jnp.concatenate([c_prompt, c_sample], axis=0)
    c_rows = -(-c_all.shape[0] // 8) * 8
    c_all = jnp.pad(c_all, ((0, c_rows - c_all.shape[0]), (0, 0)))
    mod = _modulation(c_all, w_mod, b_mod)

    xp = x_prompt.reshape(n_p, d)
    xs = x_sample.reshape(n_s, d)
    rows_p, rows_s = [], []
    for l in range(depth):
        mod_p = mod[l, :bsz].reshape(bsz, 1, 6, d)
        mod_s = jnp.repeat(mod[l, bsz:bsz + dec_b].reshape(dec_b, 6, d), dec_t, axis=0)[None]
        mp = [mod_p[:, :, i] for i in range(6)]
        ms = [mod_s[:, :, i] for i in range(6)]
        vec = lambda a: a[l].reshape(1, -1)

        wqkv = w_in[l, :, :6 * WIDTH].astype(BF16)
        wf = jnp.pad(w_in[l, :, 6 * WIDTH:6 * WIDTH + N_HEADS], ((0, 0), (0, LANES - N_HEADS))).astype(BF16)
        wg = w_in[l, :, 6 * WIDTH + N_HEADS:].astype(BF16)
        bf = b_forget[l].reshape(1, N_HEADS)
        wsb, wfx, wo = w_sb_out[l].astype(BF16), w_fox_out[l].astype(BF16), w_out[l].astype(BF16)

        qkv_p, ksb_p, vsb_p, kfx_p, vfx_p, lf_p, sg_p = _in_projection(
            xp, vec(g_pre_mix), mp[0], mp[1], wqkv, wf, wg, bf, _row_tile(seq, 256))
        cum_p = _cumsum_lanes(jnp.swapaxes(lf_p.reshape(bsz, seq, N_HEADS), 1, 2))
        fk_p, fq_p = _pair_layouts(cum_p, 0, seq)
        qkv_p3 = qkv_p.reshape(bsz, seq, 6 * WIDTH)
        osb_p = _sb_prompt(qkv_p3, tq)
        ofx_p = _fox_prompt(qkv_p3, fq_p, fk_p, tq)
        xp = _mix_out(xp, osb_p.reshape(n_p, WIDTH), ofx_p.reshape(n_p, WIDTH), sg_p, mp[2], vec(g_post_mix),
                      wsb, wfx, wo, _row_tile(seq, 512))

        qkv_s, ksb_s, vsb_s, kfx_s, vfx_s, lf_s, sg_s = _in_projection(
            xs, vec(g_pre_mix), ms[0], ms[1], wqkv, wf, wg, bf, n_s)
        t_all = n_past + dec_t
        t_pad = -(-t_all // CUMSUM_CHUNK) * CUMSUM_CHUNK
        lf_all = jnp.concatenate([cache_fox_logf[l].astype(F32), lf_s.reshape(dec_b, dec_t, N_HEADS),
                                  jnp.zeros((dec_b, t_pad - t_all, N_HEADS), F32)], axis=1)
        cum_s = _cumsum_lanes(jnp.swapaxes(lf_all, 1, 2))
        fk_s, fq_s = _pair_layouts(cum_s, n_past, dec_t)
        cache = lambda a: a[l].reshape(dec_b, n_past, WIDTH)
        osb_s, ofx_s = _sample_attention(qkv_s.reshape(dec_b, dec_t, 6 * WIDTH), cache(cache_sb_k), cache(cache_sb_v),
                                         cache(cache_fox_k), cache(cache_fox_v), fq_s, fk_s, past_chunk)
        xs = _mix_out(xs, osb_s.reshape(n_s, WIDTH), ofx_s.reshape(n_s, WIDTH), sg_s, ms[2], vec(g_post_mix),
                      wsb, wfx, wo, n_s)

        if l % 2 == 0:
            wgu, wd = w_ffn_gate_up[l // 2].astype(BF16), w_ffn_down[l // 2].astype(BF16)
            tf = 256
            xp = _ffn(xp, vec(g_pre_ffn), mp[3], mp[4], mp[5], vec(g_post_ffn), wgu, wd, _row_tile(seq, 1024), tf)
            xs = _ffn(xs, vec(g_pre_ffn), ms[3], ms[4], ms[5], vec(g_post_ffn), wgu, wd, n_s, tf)
        else:
            wr = jnp.pad(w_router[l // 2], ((0, 0), (0, LANES - n_experts)))
            wgu, wd = w_moe_gate_up[l // 2].astype(BF16), w_moe_down[l // 2].astype(BF16)
            tf = 512
            xp = _moe(xp, vec(g_pre_ffn), mp[3], mp[4], mp[5], vec(g_post_ffn), wr, wgu, wd, n_experts,
                      _row_tile(seq, 1024), tf)
            xs = _moe(xs, vec(g_pre_ffn), ms[3], ms[4], ms[5], vec(g_post_ffn), wr, wgu, wd, n_experts, n_s, tf)

        heads = lambda a, b_, t_: a.reshape(b_, t_, N_HEADS, HEAD_DIM)
        rows_p.append((heads(ksb_p, bsz, seq), heads(vsb_p, bsz, seq), heads(kfx_p, bsz, seq),
                       heads(vfx_p, bsz, seq), lf_p.reshape(bsz, seq, N_HEADS)))
        rows_s.append((heads(ksb_s, dec_b, dec_t), heads(vsb_s, dec_b, dec_t), heads(kfx_s, dec_b, dec_t),
                       heads(vfx_s, dec_b, dec_t), lf_s.reshape(dec_b, dec_t, N_HEADS)))

    stack = lambda rows, i: jnp.stack([r[i] for r in rows], axis=0)
    return (xp.reshape(bsz, seq, d), xs.reshape(dec_b, dec_t, d),
            stack(rows_p, 0), stack(rows_p, 1), stack(rows_p, 2), stack(rows_p, 3), stack(rows_p, 4),
            stack(rows_s, 0), stack(rows_s, 1), stack(rows_s, 2), stack(rows_s, 3), stack(rows_s, 4))
```

```python
import functools

import jax
import jax.numpy as jnp
from jax import lax
from jax.experimental import pallas as pl
from jax.experimental.pallas import tpu as pltpu

F32 = jnp.float32
BF16 = jnp.bfloat16

HEAD_DIM = 64
N_HEADS = 8
LANES = 128
HEAD_PAIRS = N_HEADS * HEAD_DIM // LANES
WIDTH = N_HEADS * HEAD_DIM
TOP_K = 2
RMS_EPS = 1e-6
LOG2_E = 1.4426950408889634
SB_DEAD_LOG = -104.0
VMEM_LIMIT = 56 * 1024 * 1024


def _params(sem, vmem=VMEM_LIMIT):
    return pltpu.CompilerParams(dimension_semantics=sem, vmem_limit_bytes=vmem)


def _dot(a, b):
    return jnp.dot(a, b, preferred_element_type=F32)


def _dot_nt(a, b):
    return lax.dot_general(a, b, (((1,), (1,)), ((), ())), preferred_element_type=F32)


def _sigmoid(x):
    return 1.0 / (1.0 + jnp.exp(-x))


def _softplus(x):
    return jnp.maximum(x, 0.0) + jnp.log(1.0 + jnp.exp(-jnp.abs(x)))


def _rms(x, g):
    return x * lax.rsqrt(jnp.mean(x * x, axis=-1, keepdims=True) + RMS_EPS) * g


def _mod_kernel(c_ref, w_ref, b_ref, o_ref):
    c = c_ref[...]
    s = c * _sigmoid(c)
    o_ref[0] = jnp.dot(s, w_ref[0], precision=lax.Precision.HIGHEST, preferred_element_type=F32) + b_ref[0]


def _modulation(c_all, w_mod, b_mod):
    depth, d, d6 = w_mod.shape
    rows = c_all.shape[0]
    tn = 1024
    return pl.pallas_call(
        _mod_kernel,
        grid=(depth, d6 // tn),
        in_specs=[pl.BlockSpec((rows, d), lambda l, j: (0, 0)),
                  pl.BlockSpec((1, d, tn), lambda l, j: (l, 0, j)),
                  pl.BlockSpec((1, 1, tn), lambda l, j: (l, 0, j))],
        out_specs=pl.BlockSpec((1, rows, tn), lambda l, j: (l, 0, j)),
        out_shape=jax.ShapeDtypeStruct((depth, rows, d6), F32),
        compiler_params=_params(("arbitrary", "arbitrary")),
        name="modulation",
    )(c_all, w_mod, b_mod.reshape(depth, 1, d6))


def _mod_spec(mod, tm, tiles_per_group):
    _, r, d = mod.shape
    return pl.BlockSpec((1, r, d), lambda i, *_: (i // tiles_per_group, 0, 0))


def _inproj_kernel(x_ref, g_ref, shift_ref, scale_ref, wqkv_ref, wf_ref, wg_ref, bf_ref,
                   qkv_ref, ksb_ref, vsb_ref, kfx_ref, vfx_ref, lf_ref, sg_ref):
    h = _rms(x_ref[...], g_ref[...]) * (1.0 + scale_ref[0]) + shift_ref[0]
    hb = h.astype(BF16)
    f32_outs = {1: ksb_ref, 2: vsb_ref, 4: kfx_ref, 5: vfx_ref}
    for c in range(6):
        cols = slice(c * WIDTH, (c + 1) * WIDTH)
        acc = _dot(hb, wqkv_ref[:, cols])
        if c in f32_outs:
            f32_outs[c][...] = acc
            qkv_ref[:, cols] = acc.astype(BF16)
        else:
            q_scale = HEAD_DIM ** -0.5 * (LOG2_E if c == 3 else 1.0)
            qkv_ref[:, cols] = (acc * q_scale).astype(BF16)
    f = _dot(hb, wf_ref[...])[:, :N_HEADS] + bf_ref[...]
    lf_ref[...] = -_softplus(-f)
    for c in range(wg_ref.shape[1] // WIDTH):
        cols = slice(c * WIDTH, (c + 1) * WIDTH)
        sg_ref[:, cols] = _sigmoid(_dot(hb, wg_ref[:, cols])).astype(BF16)


def _in_projection(x, g_pre, shift, scale, wqkv, wf, wg, b_forget, tm):
    n, d = x.shape
    tiles_per_group = (n // tm) // shift.shape[0]
    const = lambda i: (0, 0)
    row = lambda i: (i, 0)
    return pl.pallas_call(
        _inproj_kernel,
        grid=(n // tm,),
        in_specs=[pl.BlockSpec((tm, d), row),
                  pl.BlockSpec((1, d), const),
                  _mod_spec(shift, tm, tiles_per_group),
                  _mod_spec(scale, tm, tiles_per_group),
                  pl.BlockSpec(wqkv.shape, const),
                  pl.BlockSpec(wf.shape, const),
                  pl.BlockSpec(wg.shape, const),
                  pl.BlockSpec((1, N_HEADS), const)],
        out_specs=[pl.BlockSpec((tm, 6 * WIDTH), row)] + [pl.BlockSpec((tm, WIDTH), row)] * 4
                  + [pl.BlockSpec((tm, N_HEADS), row), pl.BlockSpec((tm, wg.shape[1]), row)],
        out_shape=[jax.ShapeDtypeStruct((n, 6 * WIDTH), BF16)] + [jax.ShapeDtypeStruct((n, WIDTH), F32)] * 4
                  + [jax.ShapeDtypeStruct((n, N_HEADS), F32), jax.ShapeDtypeStruct((n, wg.shape[1]), BF16)],
        compiler_params=_params(("arbitrary",)),
        name="in_projection",
    )(x, g_pre, shift, scale, wqkv, wf, wg, b_forget)


CUMSUM_CHUNK = 256


def _cumsum_kernel(x_ref, o_ref):
    t = x_ref.shape[2]
    r = lax.broadcasted_iota(jnp.int32, (CUMSUM_CHUNK, CUMSUM_CHUNK), 0)
    c = lax.broadcasted_iota(jnp.int32, (CUMSUM_CHUNK, CUMSUM_CHUNK), 1)
    upper = jnp.where(r <= c, 1.0, 0.0).astype(F32)

    def step(i, carry):
        start = pl.multiple_of(i * CUMSUM_CHUNK, CUMSUM_CHUNK)
        seg = x_ref[0, :, pl.ds(start, CUMSUM_CHUNK)]
        cs = jnp.dot(seg, upper, precision=lax.Precision.HIGHEST, preferred_element_type=F32) + carry
        o_ref[0, :, pl.ds(start, CUMSUM_CHUNK)] = cs
        return cs[:, CUMSUM_CHUNK - 1:CUMSUM_CHUNK]

    lax.fori_loop(0, t // CUMSUM_CHUNK, step, jnp.zeros((x_ref.shape[1], 1), F32))


def _cumsum_lanes(x):
    b, r, t = x.shape
    return pl.pallas_call(
        _cumsum_kernel,
        grid=(b,),
        in_specs=[pl.BlockSpec((1, r, t), lambda i: (i, 0, 0))],
        out_specs=pl.BlockSpec((1, r, t), lambda i: (i, 0, 0)),
        out_shape=jax.ShapeDtypeStruct((b, r, t), F32),
        compiler_params=_params(("arbitrary",)),
        name="forget_cumsum",
    )(x)


def _pair_layouts(cum_t, t_query_start, t_query):
    b, _, tp = cum_t.shape
    f_keys = cum_t.reshape(b, HEAD_PAIRS, 2, tp)
    f_query = jnp.swapaxes(f_keys[:, :, :, t_query_start:t_query_start + t_query], 2, 3)
    return f_keys, f_query


def _head_masks():
    lane = lax.broadcasted_iota(jnp.int32, (1, LANES), 1)
    return lane < HEAD_DIM


def _split_heads(q2, first_head):
    zero = jnp.zeros_like(q2)
    return jnp.where(first_head, q2, zero), jnp.where(first_head, zero, q2)


def _strict_lower_neg(n):
    r = lax.broadcasted_iota(jnp.int32, (n, n), 0)
    c = lax.broadcasted_iota(jnp.int32, (n, n), 1)
    return jnp.where(r > c, -1.0, 0.0).astype(BF16)


def _sb_block(qh, k2, v2, neg_tri, visible, acc_ref, r_ref, hd):
    z = _dot_nt(qh, k2)
    sp = _softplus(z)
    if visible is not None:
        sp = jnp.where(visible, sp, 0.0)
    later = _dot(sp.astype(BF16), neg_tri)
    p = jnp.exp(z - sp + later)
    if visible is not None:
        p = jnp.where(visible, p, 0.0)
    r = r_ref[hd]
    acc_ref[hd] += jnp.exp(r) * _dot(p.astype(BF16), v2)
    r_ref[hd] = r + later[:, 0:1] - sp[:, 0:1]


def _fox_block(qh, k2, v2_ones, fq, fk, allowed, acc_ref, m_ref, hd):
    s = _dot_nt(qh, k2) + fq - fk
    if allowed is not None:
        s = jnp.where(allowed, s, -jnp.inf)
    m_old = m_ref[hd]
    m_new = jnp.maximum(m_old, jnp.max(s, axis=-1, keepdims=True))
    p = jnp.exp2(s - m_new)
    acc_ref[hd] = jnp.exp2(m_old - m_new) * acc_ref[hd] + _dot(p.astype(BF16), v2_ones)
    m_ref[hd] = m_new


def _fox_finish(acc_ref, first_head):
    a, b = acc_ref[0], acc_ref[1]
    num = jnp.where(first_head, a, b)
    den = jnp.where(first_head, pltpu.roll(a, HEAD_DIM, 1), pltpu.roll(b, HEAD_DIM, 1))
    return num / den


def _with_ones(v2, first_head):
    one = jnp.ones_like(v2)
    return jnp.where(first_head, v2, one), jnp.where(first_head, one, v2)


def _sb_prompt_kernel(q_ref, k_ref, v_ref, o_ref, acc_ref, r_ref, *, tq):
    i = pl.program_id(2)
    first_head = _head_masks()
    qs = _split_heads(q_ref[0], first_head)
    neg_tri = _strict_lower_neg(tq)
    row = lax.broadcasted_iota(jnp.int32, (tq, tq), 0)
    col = lax.broadcasted_iota(jnp.int32, (tq, tq), 1)
    visible = col < row
    acc_ref[...] = jnp.zeros_like(acc_ref)
    r_ref[...] = jnp.zeros_like(r_ref)

    def block(j, vis):
        start = pl.multiple_of(j * tq, tq)
        k2 = k_ref[0, pl.ds(start, tq), :]
        v2 = v_ref[0, pl.ds(start, tq), :]
        for hd in range(2):
            _sb_block(qs[hd], k2, v2, neg_tri, vis, acc_ref, r_ref, hd)

    def live():
        return jnp.max(jnp.maximum(r_ref[0], r_ref[1]))

    block(i, visible)

    def cond(carry):
        j, r_max = carry
        return jnp.logical_and(j >= 0, r_max > SB_DEAD_LOG)

    def body(carry):
        j, _ = carry
        block(j, None)
        return j - 1, live()

    lax.while_loop(cond, body, (i - 1, live()))
    o_ref[0] = jnp.where(first_head, acc_ref[0], acc_ref[1]).astype(o_ref.dtype)


def _sb_prompt(qkv, tq):
    b, t, _ = qkv.shape
    kernel = functools.partial(_sb_prompt_kernel, tq=tq)
    return pl.pallas_call(
        kernel,
        grid=(b, HEAD_PAIRS, t // tq),
        in_specs=[pl.BlockSpec((1, tq, LANES), lambda b_, p, i: (b_, i, p)),
                  pl.BlockSpec((1, t, LANES), lambda b_, p, i: (b_, 0, HEAD_PAIRS + p)),
                  pl.BlockSpec((1, t, LANES), lambda b_, p, i: (b_, 0, 2 * HEAD_PAIRS + p))],
        out_specs=pl.BlockSpec((1, tq, LANES), lambda b_, p, i: (b_, i, p)),
        out_shape=jax.ShapeDtypeStruct((b, t, WIDTH), BF16),
        scratch_shapes=[pltpu.VMEM((2, tq, LANES), F32), pltpu.VMEM((2, tq, 1), F32)],
        compiler_params=_params(("arbitrary", "arbitrary", "arbitrary")),
        name="sb_attention_prompt",
    )(qkv, qkv, qkv)


FORGET_SPLIT = 3


def _forget_key_columns(cum_t):
    b, _, t = cum_t.shape
    rest = -cum_t
    terms = []
    for _ in range(FORGET_SPLIT):
        part = lax.bitcast_convert_type(lax.bitcast_convert_type(rest, jnp.uint32) & jnp.uint32(0xFFFF0000), F32)
        terms.append(part.astype(BF16))
        rest = rest - part
    cols = jnp.stack(terms, axis=2).reshape(b, HEAD_PAIRS, 2 * FORGET_SPLIT, t)
    cols = jnp.swapaxes(cols, 2, 3)
    return jnp.pad(cols, ((0, 0), (0, 0), (0, 0), (0, LANES - 2 * FORGET_SPLIT)))


def _fox_prompt_kernel(qt_ref, k_ref, kf_ref, vt_ref, fq_ref, o_ref, w_ref, acc_ref, m_ref, *, tq):
    i = pl.program_id(2)
    sub = lax.broadcasted_iota(jnp.int32, (LANES, tq), 0)
    first_rows = sub < HEAD_DIM
    qt = qt_ref[0]
    zero = jnp.zeros_like(qt)
    for hd in range(2):
        lo = hd * FORGET_SPLIT
        select = jnp.where(jnp.logical_and(sub >= lo, sub < lo + FORGET_SPLIT), 1.0, 0.0).astype(BF16)
        w_ref[hd, :LANES, :] = jnp.where(first_rows, qt, zero) if hd == 0 else jnp.where(first_rows, zero, qt)
        w_ref[hd, LANES:, :] = select
    acc_ref[...] = jnp.zeros_like(acc_ref)
    m_ref[...] = jnp.full_like(m_ref, -jnp.inf)
    key_idx = lax.broadcasted_iota(jnp.int32, (tq, tq), 0)
    query_idx = lax.broadcasted_iota(jnp.int32, (tq, tq), 1)
    allowed = key_idx <= query_idx

    def block(j, mask):
        start = pl.multiple_of(j * tq, tq)
        keys = jnp.concatenate([k_ref[0, pl.ds(start, tq), :], kf_ref[0, 0, pl.ds(start, tq), :]], axis=1)
        vt = vt_ref[0, :, pl.ds(start, tq)]
        one = jnp.ones_like(vt)
        first_v = lax.broadcasted_iota(jnp.int32, vt.shape, 0) < HEAD_DIM
        vts = (jnp.where(first_v, vt, one), jnp.where(first_v, one, vt))
        for hd in range(2):
            u = _dot(keys, w_ref[hd])
            if mask is not None:
                u = jnp.where(mask, u, -jnp.inf)
            fq = fq_ref[0, 0, hd:hd + 1, :]
            m_old = m_ref[hd]
            m_new = jnp.maximum(m_old, jnp.max(u, axis=0, keepdims=True) + fq)
            p = jnp.exp2(u + (fq - m_new))
            acc_ref[hd] = jnp.exp2(m_old - m_new) * acc_ref[hd] + _dot(vts[hd], p.astype(BF16))
            m_ref[hd] = m_new

    block(i, allowed)

    def body(n, carry):
        block(i - 1 - n, None)
        return carry

    lax.fori_loop(0, i, body, 0)
    a, b = acc_ref[0], acc_ref[1]
    out_t = jnp.where(first_rows, a / a[HEAD_DIM:HEAD_DIM + 1, :], b / b[0:1, :])
    o_ref[0] = out_t.T.astype(o_ref.dtype)


def _fox_prompt(qkv, q_t, v_t, f_rows, f_key_cols, tq):
    b, t, _ = qkv.shape
    kernel = functools.partial(_fox_prompt_kernel, tq=tq)
    return pl.pallas_call(
        kernel,
        grid=(b, HEAD_PAIRS, t // tq),
        in_specs=[pl.BlockSpec((1, LANES, tq), lambda b_, p, i: (b_, p, i)),
                  pl.BlockSpec((1, t, LANES), lambda b_, p, i: (b_, 0, 4 * HEAD_PAIRS + p)),
                  pl.BlockSpec((1, 1, t, LANES), lambda b_, p, i: (b_, p, 0, 0)),
                  pl.BlockSpec((1, LANES, t), lambda b_, p, i: (b_, p, 0)),
                  pl.BlockSpec((1, 1, 2, tq), lambda b_, p, i: (b_, p, 0, i))],
        out_specs=pl.BlockSpec((1, tq, LANES), lambda b_, p, i: (b_, i, p)),
        out_shape=jax.ShapeDtypeStruct((b, t, WIDTH), BF16),
        scratch_shapes=[pltpu.VMEM((2, 2 * LANES, tq), BF16), pltpu.VMEM((2, LANES, tq), F32),
                        pltpu.VMEM((2, 1, tq), F32)],
        compiler_params=_params(("arbitrary", "arbitrary", "arbitrary")),
        name="fox_attention_prompt",
    )(q_t, qkv, f_key_cols, v_t, f_rows)


def _sample_attn_kernel(qsb_ref, ksb_ref, vsb_ref, qfx_ref, kfx_ref, vfx_ref,
                        csk_ref, csv_ref, cfk_ref, cfv_ref, fq_ref, fk_ref,
                        osb_ref, ofx_ref, acc_ref, st_ref, *, tk):
    t_new = qsb_ref.shape[1]
    n_past = csk_ref.shape[1]
    n_blocks = n_past // tk
    first_head = _head_masks()
    row = lax.broadcasted_iota(jnp.int32, (t_new, t_new), 0)
    col = lax.broadcasted_iota(jnp.int32, (t_new, t_new), 1)

    qs = _split_heads(qsb_ref[0], first_head)
    acc_ref[...] = jnp.zeros_like(acc_ref)
    st_ref[...] = jnp.zeros_like(st_ref)
    tri_new = _strict_lower_neg(t_new)
    tri_past = _strict_lower_neg(tk)
    for hd in range(2):
        _sb_block(qs[hd], ksb_ref[0], vsb_ref[0], tri_new, col < row, acc_ref, st_ref, hd)

    def live():
        return jnp.max(jnp.maximum(st_ref[0], st_ref[1]))

    def sb_cond(carry):
        j, r_max = carry
        return jnp.logical_and(j >= 0, r_max > SB_DEAD_LOG)

    def sb_body(carry):
        j, _ = carry
        start = pl.multiple_of(j * tk, tk)
        k2 = csk_ref[0, pl.ds(start, tk), :].astype(BF16)
        v2 = csv_ref[0, pl.ds(start, tk), :].astype(BF16)
        for hd in range(2):
            _sb_block(qs[hd], k2, v2, tri_past, None, acc_ref, st_ref, hd)
        return j - 1, live()

    lax.while_loop(sb_cond, sb_body, (n_blocks - 1, live()))
    osb_ref[0] = jnp.where(first_head, acc_ref[0], acc_ref[1]).astype(osb_ref.dtype)

    qs = _split_heads(qfx_ref[0], first_head)
    acc_ref[...] = jnp.zeros_like(acc_ref)
    st_ref[...] = jnp.full_like(st_ref, -jnp.inf)
    fq = fq_ref[0, 0]
    vs = _with_ones(vfx_ref[0], first_head)
    for hd in range(2):
        fk = fk_ref[0, 0, hd:hd + 1, n_past:n_past + t_new]
        _fox_block(qs[hd], kfx_ref[0], vs[hd], fq[:, hd:hd + 1], fk, col <= row, acc_ref, st_ref, hd)

    k2 = cfk_ref[0].astype(BF16)
    vs_ = _with_ones(cfv_ref[0].astype(BF16), first_head)
    for hd in range(2):
        fk = fk_ref[0, 0, hd:hd + 1, 0:n_past]
        _fox_block(qs[hd], k2, vs_[hd], fq[:, hd:hd + 1], fk, None, acc_ref, st_ref, hd)
    ofx_ref[0] = _fox_finish(acc_ref, first_head).astype(ofx_ref.dtype)


def _sample_attention(qkv, cache_sb_k, cache_sb_v, cache_fox_k, cache_fox_v, f_query, f_keys, tk):
    b, t_new, _ = qkv.shape
    n_past = cache_sb_k.shape[1]
    tp = f_keys.shape[3]
    new = lambda c: pl.BlockSpec((1, t_new, LANES), lambda b_, p: (b_, 0, c * HEAD_PAIRS + p))
    past = pl.BlockSpec((1, n_past, LANES), lambda b_, p: (b_, 0, p))
    out = pl.BlockSpec((1, t_new, LANES), lambda b_, p: (b_, 0, p))
    kernel = functools.partial(_sample_attn_kernel, tk=tk)
    return pl.pallas_call(
        kernel,
        grid=(b, HEAD_PAIRS),
        in_specs=[new(0), new(1), new(2), new(3), new(4), new(5), past, past, past, past,
                  pl.BlockSpec((1, 1, t_new, 2), lambda b_, p: (b_, p, 0, 0)),
                  pl.BlockSpec((1, 1, 2, tp), lambda b_, p: (b_, p, 0, 0))],
        out_specs=[out, out],
        out_shape=[jax.ShapeDtypeStruct((b, t_new, WIDTH), BF16)] * 2,
        scratch_shapes=[pltpu.VMEM((2, t_new, LANES), F32), pltpu.VMEM((2, t_new, 1), F32)],
        compiler_params=_params(("arbitrary", "arbitrary")),
        name="attention_sample",
    )(qkv, qkv, qkv, qkv, qkv, qkv, cache_sb_k, cache_sb_v, cache_fox_k, cache_fox_v, f_query, f_keys)


def _mix_out_kernel(x_ref, osb_ref, ofx_ref, sg_ref, gate_ref, g_ref, wsb_ref, wfx_ref, wo_ref, o_ref):
    d = x_ref.shape[1]
    merged = (sg_ref[:, :d].astype(F32) * _dot(osb_ref[...], wsb_ref[...])
              + sg_ref[:, d:].astype(F32) * _dot(ofx_ref[...], wfx_ref[...]))
    y = _dot(merged.astype(BF16), wo_ref[...])
    o_ref[...] = x_ref[...] + gate_ref[0] * _rms(y, g_ref[...])


def _mix_out(x, o_sb, o_fx, sg, gate, g_post, w_sb_out, w_fox_out, w_out, tm):
    n, d = x.shape
    tiles_per_group = (n // tm) // gate.shape[0]
    const = lambda i: (0, 0)
    row = lambda i: (i, 0)
    return pl.pallas_call(
        _mix_out_kernel,
        grid=(n // tm,),
        in_specs=[pl.BlockSpec((tm, d), row), pl.BlockSpec((tm, WIDTH), row), pl.BlockSpec((tm, WIDTH), row),
                  pl.BlockSpec((tm, 2 * d), row), _mod_spec(gate, tm, tiles_per_group),
                  pl.BlockSpec((1, d), const), pl.BlockSpec(w_sb_out.shape, const),
                  pl.BlockSpec(w_fox_out.shape, const), pl.BlockSpec(w_out.shape, const)],
        out_specs=pl.BlockSpec((tm, d), row),
        out_shape=jax.ShapeDtypeStruct((n, d), F32),
        compiler_params=_params(("arbitrary",)),
        name="mix_out",
    )(x, o_sb, o_fx, sg, gate, g_post, w_sb_out, w_fox_out, w_out)


def _swiglu_chunk(hb, wa, wb, wd):
    a = _dot(hb, wa)
    b = _dot(hb, wb)
    return a * _sigmoid(a) * b, wd


def _ffn_kernel(x_ref, gpre_ref, shift_ref, scale_ref, gate_ref, gpost_ref, wa_ref, wb_ref, wd_ref,
                o_ref, h_ref, acc_ref):
    f = pl.program_id(1)

    @pl.when(f == 0)
    def _():
        h = _rms(x_ref[...], gpre_ref[...]) * (1.0 + scale_ref[0]) + shift_ref[0]
        h_ref[...] = h.astype(BF16)
        acc_ref[...] = jnp.zeros_like(acc_ref)

    hb = h_ref[...]
    a = _dot(hb, wa_ref[...])
    b = _dot(hb, wb_ref[...])
    act = a * _sigmoid(a) * b
    acc_ref[...] += _dot(act.astype(BF16), wd_ref[...])

    @pl.when(f == pl.num_programs(1) - 1)
    def _():
        o_ref[...] = x_ref[...] + gate_ref[0] * _rms(acc_ref[...], gpost_ref[...])


def _ffn(x, g_pre, shift, scale, gate, g_post, w_gate_up, w_down, tm, tf):
    n, d = x.shape
    d_ff = w_down.shape[0]
    nf = d_ff // tf
    tiles_per_group = (n // tm) // shift.shape[0]
    const = lambda i, f: (0, 0)
    row = lambda i, f: (i, 0)
    return pl.pallas_call(
        _ffn_kernel,
        grid=(n // tm, nf),
        in_specs=[pl.BlockSpec((tm, d), row), pl.BlockSpec((1, d), const),
                  _mod_spec(shift, tm, tiles_per_group), _mod_spec(scale, tm, tiles_per_group),
                  _mod_spec(gate, tm, tiles_per_group), pl.BlockSpec((1, d), const),
                  pl.BlockSpec((d, tf), lambda i, f: (0, f)),
                  pl.BlockSpec((d, tf), lambda i, f: (0, nf + f)),
                  pl.BlockSpec((tf, d), lambda i, f: (f, 0))],
        out_specs=pl.BlockSpec((tm, d), row),
        out_shape=jax.ShapeDtypeStruct((n, d), F32),
        scratch_shapes=[pltpu.VMEM((tm, d), BF16), pltpu.VMEM((tm, d), F32)],
        compiler_params=_params(("arbitrary", "arbitrary")),
        name="ffn_dense",
    )(x, g_pre, shift, scale, gate, g_post, w_gate_up, w_gate_up, w_down)


def _route_kernel(x_ref, gpre_ref, shift_ref, scale_ref, wr_ref, h_ref, idx_ref, w_ref, *, n_experts):
    h = _rms(x_ref[...], gpre_ref[...]) * (1.0 + scale_ref[0]) + shift_ref[0]
    h_ref[...] = h
    logits = jnp.dot(h, wr_ref[...], precision=lax.Precision.HIGHEST, preferred_element_type=F32)[:, :n_experts]
    idx = lax.broadcasted_iota(jnp.int32, logits.shape, 1)
    m1 = jnp.max(logits, axis=-1, keepdims=True)
    i1 = jnp.min(jnp.where(logits == m1, idx, n_experts), axis=-1, keepdims=True)
    rest = jnp.where(idx == i1, -jnp.inf, logits)
    m2 = jnp.max(rest, axis=-1, keepdims=True)
    i2 = jnp.min(jnp.where(rest == m2, idx, n_experts), axis=-1, keepdims=True)
    e2 = jnp.exp(m2 - m1)
    slot = lax.broadcasted_iota(jnp.int32, idx_ref.shape, 1)
    idx_ref[...] = jnp.where(slot == 0, i1, i2)
    w_ref[...] = jnp.where(slot == 0, 1.0 / (1.0 + e2), e2 / (1.0 + e2))


def _route(x, g_pre, shift, scale, w_router, n_experts, tm):
    n, d = x.shape
    tiles_per_group = (n // tm) // shift.shape[0]
    const = lambda i: (0, 0)
    row = lambda i: (i, 0)
    return pl.pallas_call(
        functools.partial(_route_kernel, n_experts=n_experts),
        grid=(n // tm,),
        in_specs=[pl.BlockSpec((tm, d), row), pl.BlockSpec((1, d), const),
                  _mod_spec(shift, tm, tiles_per_group), _mod_spec(scale, tm, tiles_per_group),
                  pl.BlockSpec(w_router.shape, const)],
        out_specs=[pl.BlockSpec((tm, d), row), pl.BlockSpec((tm, TOP_K), row), pl.BlockSpec((tm, TOP_K), row)],
        out_shape=[jax.ShapeDtypeStruct((n, d), F32), jax.ShapeDtypeStruct((n, TOP_K), jnp.int32),
                   jax.ShapeDtypeStruct((n, TOP_K), F32)],
        compiler_params=_params(("arbitrary",)),
        name="moe_route",
    )(x, g_pre, shift, scale, w_router)


def _dispatch_tables(top_idx, top_w, n_experts, tm):
    n = top_idx.shape[0]
    n_assign = n * TOP_K
    expert = top_idx.reshape(n_assign)
    onehot = (expert[:, None] == jnp.arange(n_experts, dtype=jnp.int32)[None, :]).astype(jnp.int32)
    rank = jnp.sum((jnp.cumsum(onehot, axis=0) - onehot) * onehot, axis=1)
    counts = jnp.sum(onehot, axis=0)
    padded = (counts + tm - 1) // tm * tm
    group_end = jnp.cumsum(padded)
    pos = (group_end - padded)[expert] + rank
    p_rows = n_assign + n_experts * tm
    src_token = jnp.zeros((p_rows,), jnp.int32).at[pos].set(jnp.arange(n_assign, dtype=jnp.int32) // TOP_K)
    row_weight = jnp.zeros((p_rows,), F32).at[pos].set(top_w.reshape(n_assign)).reshape(p_rows, 1)
    tile_start = jnp.arange(p_rows // tm, dtype=jnp.int32) * tm
    tile_expert = jnp.minimum(jnp.sum(tile_start[:, None] >= group_end[None, :], axis=1), n_experts - 1)
    n_used = (group_end[-1] // tm).reshape(1)
    return src_token, row_weight, tile_expert.astype(jnp.int32), n_used.astype(jnp.int32), pos.astype(jnp.int32)


def _gather_rows(src_hbm, dst, sem, index_of_row, n_rows):
    def issue(r, carry):
        pltpu.make_async_copy(src_hbm.at[pl.ds(index_of_row(r), 1), :], dst.at[pl.ds(r, 1), :], sem).start()
        return carry

    lax.fori_loop(0, n_rows, issue, 0, unroll=8)


def _wait_rows(src_hbm, dst, sem):
    pltpu.make_async_copy(src_hbm.at[pl.ds(0, dst.shape[0]), :], dst, sem).wait()


def _experts_kernel(src_ref, expert_ref, used_ref, h_hbm, roww_ref, wa_ref, wb_ref, wd_ref, o_ref,
                    rows_ref, xb_ref, acc_ref, sem, *, tm):
    i = pl.program_id(0)
    f = pl.program_id(1)
    n_used = used_ref[0]
    slot = lax.rem(i, 2)
    active = i < n_used

    def gather(tile, into):
        _gather_rows(h_hbm, rows_ref.at[into], sem.at[into], lambda r: src_ref[tile * tm + r], tm)

    @pl.when(jnp.logical_and(f == 0, active))
    def _():
        @pl.when(i == 0)
        def _():
            gather(0, 0)

        _wait_rows(h_hbm, rows_ref.at[slot], sem.at[slot])

        @pl.when(i + 1 < n_used)
        def _():
            gather(i + 1, 1 - slot)

        xb_ref[...] = rows_ref[slot].astype(BF16)
        acc_ref[...] = jnp.zeros_like(acc_ref)

    @pl.when(active)
    def _():
        xb = xb_ref[...]
        a = _dot(xb, wa_ref[0])
        b = _dot(xb, wb_ref[0])
        act = a * _sigmoid(a) * b
        acc_ref[...] += _dot(act.astype(BF16), wd_ref[0])

    last = f == pl.num_programs(1) - 1

    @pl.when(jnp.logical_and(last, active))
    def _():
        o_ref[...] = acc_ref[...] * roww_ref[...]

    @pl.when(jnp.logical_and(last, jnp.logical_not(active)))
    def _():
        o_ref[...] = jnp.zeros_like(o_ref)


def _experts(h, src_token, row_weight, tile_expert, n_used, w_gate_up, w_down, tm, tf):
    p_rows = src_token.shape[0]
    d = h.shape[1]
    nf = w_down.shape[1] // tf
    ff = lambda i, f, used: jnp.where(i < used[0], f, nf - 1)
    grid_spec = pltpu.PrefetchScalarGridSpec(
        num_scalar_prefetch=3,
        grid=(p_rows // tm, nf),
        in_specs=[pl.BlockSpec(memory_space=pl.ANY),
                  pl.BlockSpec((tm, 1), lambda i, f, src, ex, used: (i, 0)),
                  pl.BlockSpec((1, d, tf), lambda i, f, src, ex, used: (ex[i], 0, ff(i, f, used))),
                  pl.BlockSpec((1, d, tf), lambda i, f, src, ex, used: (ex[i], 0, nf + ff(i, f, used))),
                  pl.BlockSpec((1, tf, d), lambda i, f, src, ex, used: (ex[i], ff(i, f, used), 0))],
        out_specs=pl.BlockSpec((tm, d), lambda i, f, src, ex, used: (i, 0)),
        scratch_shapes=[pltpu.VMEM((2, tm, d), F32), pltpu.VMEM((tm, d), BF16), pltpu.VMEM((tm, d), F32),
                        pltpu.SemaphoreType.DMA((2,))])
    return pl.pallas_call(
        functools.partial(_experts_kernel, tm=tm),
        grid_spec=grid_spec,
        out_shape=jax.ShapeDtypeStruct((p_rows, d), F32),
        compiler_params=_params(("arbitrary", "arbitrary")),
        name="moe_experts",
    )(src_token, tile_expert, n_used, h, row_weight, w_gate_up, w_gate_up, w_down)


def _combine_kernel(pos_ref, y_hbm, x_ref, gate_ref, gpost_ref, o_ref, rows_ref, sem, *, tm):
    i = pl.program_id(0)
    slot = lax.rem(i, 2)

    def gather(tile, into):
        for k in range(TOP_K):
            _gather_rows(y_hbm, rows_ref.at[into, k], sem.at[into],
                         lambda r, k=k: pos_ref[(tile * tm + r) * TOP_K + k], tm)

    @pl.when(i == 0)
    def _():
        gather(0, 0)

    for k in range(TOP_K):
        _wait_rows(y_hbm, rows_ref.at[slot, k], sem.at[slot])

    @pl.when(i + 1 < pl.num_programs(0))
    def _():
        gather(i + 1, 1 - slot)

    mixed = rows_ref[slot, 0] + rows_ref[slot, 1]
    o_ref[...] = x_ref[...] + gate_ref[0] * _rms(mixed, gpost_ref[...])


def _combine(y, pos, x, gate, g_post, tm):
    n, d = x.shape
    tiles_per_group = (n // tm) // gate.shape[0]
    grid_spec = pltpu.PrefetchScalarGridSpec(
        num_scalar_prefetch=1,
        grid=(n // tm,),
        in_specs=[pl.BlockSpec(memory_space=pl.ANY),
                  pl.BlockSpec((tm, d), lambda i, pos_: (i, 0)),
                  _mod_spec(gate, tm, tiles_per_group),
                  pl.BlockSpec((1, d), lambda i, pos_: (0, 0))],
        out_specs=pl.BlockSpec((tm, d), lambda i, pos_: (i, 0)),
        scratch_shapes=[pltpu.VMEM((2, TOP_K, tm, d), F32), pltpu.SemaphoreType.DMA((2,))])
    return pl.pallas_call(
        functools.partial(_combine_kernel, tm=tm),
        grid_spec=grid_spec,
        out_shape=jax.ShapeDtypeStruct((n, d), F32),
        compiler_params=_params(("arbitrary",)),
        name="moe_combine",
    )(pos, y, x, gate, g_post)


def _moe(x, g_pre, shift, scale, gate, g_post, w_router, w_gate_up, w_down, n_experts, tm_route, tm, tf, tm_out):
    h, top_idx, top_w = _route(x, g_pre, shift, scale, w_router, n_experts, tm_route)
    src_token, row_weight, tile_expert, n_used, pos = _dispatch_tables(top_idx, top_w, n_experts, tm)
    y = _experts(h, src_token, row_weight, tile_expert, n_used, w_gate_up, w_down, tm, tf)
    return _combine(y, pos, x, gate, g_post, tm_out)


def _row_tile(n, want):
    return want if n % want == 0 else n


def kernel(x_prompt, x_sample, c_prompt, c_sample, cache_sb_k, cache_sb_v, cache_fox_k, cache_fox_v, cache_fox_logf, w_mod, b_mod, g_pre_mix, g_post_mix, g_pre_ffn, g_post_ffn, w_in, b_forget, w_sb_out, w_fox_out, w_out, w_ffn_gate_up, w_ffn_down, w_router, w_moe_gate_up, w_moe_down):
    bsz, seq, d = x_prompt.shape
    dec_b, dec_t, _ = x_sample.shape
    depth = w_mod.shape[0]
    n_past = cache_sb_k.shape[2]
    n_experts = w_router.shape[2]
    n_p, n_s = bsz * seq, dec_b * dec_t
    tq = _row_tile(seq, 256)
    past_chunk = _row_tile(n_past, 256)

    c_all = jnp.concatenate([c_prompt, c_sample], axis=0)
    c_rows = -(-c_all.shape[0] // 8) * 8
    c_all = jnp.pad(c_all, ((0, c_rows - c_all.shape[0]), (0, 0)))
    mod = _modulation(c_all, w_mod, b_mod)

    xp = x_prompt.reshape(n_p, d)
    xs = x_sample.reshape(n_s, d)
    rows_p, rows_s = [], []
    for l in range(depth):
        mod_p = mod[l, :bsz].reshape(bsz, 1, 6, d)
        mod_s = jnp.repeat(mod[l, bsz:bsz + dec_b].reshape(dec_b, 6, d), dec_t, axis=0)[None]
        mp = [mod_p[:, :, i] for i in range(6)]
        ms = [mod_s[:, :, i] for i in range(6)]
        vec = lambda a: a[l].reshape(1, -1)

        wqkv = w_in[l, :, :6 * WIDTH].astype(BF16)
        wf = jnp.pad(w_in[l, :, 6 * WIDTH:6 * WIDTH + N_HEADS], ((0, 0), (0, LANES - N_HEADS))).astype(BF16)
        wg = w_in[l, :, 6 * WIDTH + N_HEADS:].astype(BF16)
        bf = b_forget[l].reshape(1, N_HEADS)
        wsb, wfx, wo = w_sb_out[l].astype(BF16), w_fox_out[l].astype(BF16), w_out[l].astype(BF16)

        qkv_p, ksb_p, vsb_p, kfx_p, vfx_p, lf_p, sg_p = _in_projection(
            xp, vec(g_pre_mix), mp[0], mp[1], wqkv, wf, wg, bf, _row_tile(seq, 256))
        cum_p = _cumsum_lanes(jnp.swapaxes(lf_p.reshape(bsz, seq, N_HEADS), 1, 2)) * LOG2_E
        qkv_p3 = qkv_p.reshape(bsz, seq, 6 * WIDTH)
        osb_p = _sb_prompt(qkv_p3, tq)
        ofx_p = _fox_prompt(qkv_p3, jnp.swapaxes(qkv_p3[:, :, 3 * WIDTH:4 * WIDTH], 1, 2),
                            jnp.swapaxes(qkv_p3[:, :, 5 * WIDTH:], 1, 2),
                            cum_p.reshape(bsz, HEAD_PAIRS, 2, seq), _forget_key_columns(cum_p), _row_tile(seq, 512))
        xp = _mix_out(xp, osb_p.reshape(n_p, WIDTH), ofx_p.reshape(n_p, WIDTH), sg_p, mp[2], vec(g_post_mix),
                      wsb, wfx, wo, _row_tile(seq, 512))

        qkv_s, ksb_s, vsb_s, kfx_s, vfx_s, lf_s, sg_s = _in_projection(
            xs, vec(g_pre_mix), ms[0], ms[1], wqkv, wf, wg, bf, n_s)
        t_all = n_past + dec_t
        t_pad = -(-t_all // CUMSUM_CHUNK) * CUMSUM_CHUNK
        lf_all = jnp.concatenate([cache_fox_logf[l].astype(F32), lf_s.reshape(dec_b, dec_t, N_HEADS),
                                  jnp.zeros((dec_b, t_pad - t_all, N_HEADS), F32)], axis=1)
        cum_s = _cumsum_lanes(jnp.swapaxes(lf_all, 1, 2)) * LOG2_E
        fk_s, fq_s = _pair_layouts(cum_s, n_past, dec_t)
        cache = lambda a: a[l].reshape(dec_b, n_past, WIDTH)
        osb_s, ofx_s = _sample_attention(qkv_s.reshape(dec_b, dec_t, 6 * WIDTH), cache(cache_sb_k), cache(cache_sb_v),
                                         cache(cache_fox_k), cache(cache_fox_v), fq_s, fk_s, past_chunk)
        xs = _mix_out(xs, osb_s.reshape(n_s, WIDTH), ofx_s.reshape(n_s, WIDTH), sg_s, ms[2], vec(g_post_mix),
                      wsb, wfx, wo, n_s)

        if l % 2 == 0:
            wgu, wd = w_ffn_gate_up[l // 2].astype(BF16), w_ffn_down[l // 2].astype(BF16)
            tf = 256
            xp = _ffn(xp, vec(g_pre_ffn), mp[3], mp[4], mp[5], vec(g_post_ffn), wgu, wd, _row_tile(seq, 1024), tf)
            xs = _ffn(xs, vec(g_pre_ffn), ms[3], ms[4], ms[5], vec(g_post_ffn), wgu, wd, n_s, tf)
        else:
            wr = jnp.pad(w_router[l // 2], ((0, 0), (0, LANES - n_experts)))
            wgu, wd = w_moe_gate_up[l // 2].astype(BF16), w_moe_down[l // 2].astype(BF16)
            tf = _row_tile(wd.shape[1], 896)
            xp = _moe(xp, vec(g_pre_ffn), mp[3], mp[4], mp[5], vec(g_post_ffn), wr, wgu, wd, n_experts,
                      _row_tile(seq, 512), 1024, tf, _row_tile(seq, 256))
            xs = _moe(xs, vec(g_pre_ffn), ms[3], ms[4], ms[5], vec(g_post_ffn), wr, wgu, wd, n_experts,
                      n_s, 128, tf, n_s)

        heads = lambda a, b_, t_: a.reshape(b_, t_, N_HEADS, HEAD_DIM)
        rows_p.append((heads(ksb_p, bsz, seq), heads(vsb_p, bsz, seq), heads(kfx_p, bsz, seq),
                       heads(vfx_p, bsz, seq), lf_p.reshape(bsz, seq, N_HEADS)))
        rows_s.append((heads(ksb_s, dec_b, dec_t), heads(vsb_s, dec_b, dec_t), heads(kfx_s, dec_b, dec_t),
                       heads(vfx_s, dec_b, dec_t), lf_s.reshape(dec_b, dec_t, N_HEADS)))

    stack = lambda rows, i: jnp.stack([r[i] for r in rows], axis=0)
    return (xp.reshape(bsz, seq, d), xs.reshape(dec_b, dec_t, d),
            stack(rows_p, 0), stack(rows_p, 1), stack(rows_p, 2), stack(rows_p, 3), stack(rows_p, 4),
            stack(rows_s, 0), stack(rows_s, 1), stack(rows_s, 2), stack(rows_s, 3), stack(rows_s, 4))
```

```python
import functools

import jax
import jax.numpy as jnp
from jax import lax
from jax.experimental import pallas as pl
from jax.experimental.pallas import tpu as pltpu

F32 = jnp.float32
BF16 = jnp.bfloat16

HEAD_DIM = 64
N_HEADS = 8
LANES = 128
HEAD_PAIRS = N_HEADS * HEAD_DIM // LANES
WIDTH = N_HEADS * HEAD_DIM
TOP_K = 2
RMS_EPS = 1e-6
LOG2_E = 1.4426950408889634
SB_DEAD_LOG = -104.0
VMEM_LIMIT = 56 * 1024 * 1024


def _params(sem, vmem=VMEM_LIMIT):
    return pltpu.CompilerParams(dimension_semantics=sem, vmem_limit_bytes=vmem)


def _dot(a, b):
    return jnp.dot(a, b, preferred_element_type=F32)


def _dot_nt(a, b):
    return lax.dot_general(a, b, (((1,), (1,)), ((), ())), preferred_element_type=F32)


def _sigmoid(x):
    return 1.0 / (1.0 + jnp.exp(-x))


def _softplus(x):
    return jnp.maximum(x, 0.0) + jnp.log(1.0 + jnp.exp(-jnp.abs(x)))


def _rms(x, g):
    return x * lax.rsqrt(jnp.mean(x * x, axis=-1, keepdims=True) + RMS_EPS) * g


def _mod_kernel(c_ref, w_ref, b_ref, o_ref):
    c = c_ref[...]
    s = c * _sigmoid(c)
    o_ref[0] = jnp.dot(s, w_ref[0], precision=lax.Precision.HIGHEST, preferred_element_type=F32) + b_ref[0]


def _modulation(c_all, w_mod, b_mod):
    depth, d, d6 = w_mod.shape
    rows = c_all.shape[0]
    tn = 1024
    return pl.pallas_call(
        _mod_kernel,
        grid=(depth, d6 // tn),
        in_specs=[pl.BlockSpec((rows, d), lambda l, j: (0, 0)),
                  pl.BlockSpec((1, d, tn), lambda l, j: (l, 0, j)),
                  pl.BlockSpec((1, 1, tn), lambda l, j: (l, 0, j))],
        out_specs=pl.BlockSpec((1, rows, tn), lambda l, j: (l, 0, j)),
        out_shape=jax.ShapeDtypeStruct((depth, rows, d6), F32),
        compiler_params=_params(("arbitrary", "arbitrary")),
        name="modulation",
    )(c_all, w_mod, b_mod.reshape(depth, 1, d6))


def _mod_spec(mod, tm, tiles_per_group):
    _, r, d = mod.shape
    return pl.BlockSpec((1, r, d), lambda i, *_: (i // tiles_per_group, 0, 0))


TRANSPOSED_SECTIONS = (0, 2, 3, 5)


def _inproj_kernel(x_ref, g_ref, shift_ref, scale_ref, wqkv_ref, wf_ref, wg_ref, bf_ref, *refs):
    qkv_ref, qvt_ref, ksb_ref, vsb_ref, kfx_ref, vfx_ref, lf_ref, sg_ref = refs[-8:]
    tm = x_ref.shape[0]
    h = _rms(x_ref[...], g_ref[...]) * (1.0 + scale_ref[0]) + shift_ref[0]
    hb = h.astype(BF16)
    f32_outs = {1: ksb_ref, 2: vsb_ref, 4: kfx_ref, 5: vfx_ref}
    for c in range(6):
        cols = slice(c * WIDTH, (c + 1) * WIDTH)
        acc = _dot(hb, wqkv_ref[:, cols])
        if c in f32_outs:
            f32_outs[c][0] = acc.reshape(tm, N_HEADS, HEAD_DIM)
        if c in (0, 3):
            acc = acc * (HEAD_DIM ** -0.5 * (LOG2_E if c == 3 else 1.0))
        qkv_ref[:, cols] = acc.astype(BF16)
        if c in TRANSPOSED_SECTIONS:
            t = TRANSPOSED_SECTIONS.index(c)
            qvt_ref[t * WIDTH:(t + 1) * WIDTH, :] = acc.T.astype(BF16)
    f = _dot(hb, wf_ref[...])[:, :N_HEADS] + bf_ref[...]
    lf_ref[0] = -_softplus(-f)
    for c in range(wg_ref.shape[1] // WIDTH):
        cols = slice(c * WIDTH, (c + 1) * WIDTH)
        sg_ref[:, cols] = _sigmoid(_dot(hb, wg_ref[:, cols])).astype(BF16)


def _in_projection(x, g_pre, shift, scale, wqkv, wf, wg, b_forget, tm, layer, depth, stacked):
    n, d = x.shape
    tiles_per_group = (n // tm) // shift.shape[0]
    const = lambda i: (0, 0)
    row = lambda i: (i, 0)
    heads_spec = pl.BlockSpec((1, tm, N_HEADS, HEAD_DIM), lambda i: (layer, i, 0, 0))
    heads_shape = jax.ShapeDtypeStruct((depth, n, N_HEADS, HEAD_DIM), F32)
    n_in = 8
    stacked = () if stacked is None else tuple(stacked)
    return pl.pallas_call(
        _inproj_kernel,
        grid=(n // tm,),
        in_specs=[pl.BlockSpec((tm, d), row),
                  pl.BlockSpec((1, d), const),
                  _mod_spec(shift, tm, tiles_per_group),
                  _mod_spec(scale, tm, tiles_per_group),
                  pl.BlockSpec(wqkv.shape, const),
                  pl.BlockSpec(wf.shape, const),
                  pl.BlockSpec(wg.shape, const),
                  pl.BlockSpec((1, N_HEADS), const)] + [pl.BlockSpec(memory_space=pl.ANY)] * len(stacked),
        out_specs=[pl.BlockSpec((tm, 6 * WIDTH), row), pl.BlockSpec((len(TRANSPOSED_SECTIONS) * WIDTH, tm), lambda i: (0, i))]
                  + [heads_spec] * 4
                  + [pl.BlockSpec((1, tm, N_HEADS), lambda i: (layer, i, 0)), pl.BlockSpec((tm, wg.shape[1]), row)],
        out_shape=[jax.ShapeDtypeStruct((n, 6 * WIDTH), BF16),
                   jax.ShapeDtypeStruct((len(TRANSPOSED_SECTIONS) * WIDTH, n), BF16)] + [heads_shape] * 4
                  + [jax.ShapeDtypeStruct((depth, n, N_HEADS), F32), jax.ShapeDtypeStruct((n, wg.shape[1]), BF16)],
        input_output_aliases={n_in + k: 2 + k for k in range(len(stacked))},
        compiler_params=_params(("arbitrary",)),
        name="in_projection",
    )(x, g_pre, shift, scale, wqkv, wf, wg, b_forget, *stacked)


CUMSUM_CHUNK = 256


def _cumsum_kernel(x_ref, o_ref):
    t = x_ref.shape[2]
    r = lax.broadcasted_iota(jnp.int32, (CUMSUM_CHUNK, CUMSUM_CHUNK), 0)
    c = lax.broadcasted_iota(jnp.int32, (CUMSUM_CHUNK, CUMSUM_CHUNK), 1)
    upper = jnp.where(r <= c, 1.0, 0.0).astype(F32)

    def step(i, carry):
        start = pl.multiple_of(i * CUMSUM_CHUNK, CUMSUM_CHUNK)
        seg = x_ref[0, :, pl.ds(start, CUMSUM_CHUNK)]
        cs = jnp.dot(seg, upper, precision=lax.Precision.HIGHEST, preferred_element_type=F32) + carry
        o_ref[0, :, pl.ds(start, CUMSUM_CHUNK)] = cs
        return cs[:, CUMSUM_CHUNK - 1:CUMSUM_CHUNK]

    lax.fori_loop(0, t // CUMSUM_CHUNK, step, jnp.zeros((x_ref.shape[1], 1), F32))


def _cumsum_lanes(x):
    b, r, t = x.shape
    return pl.pallas_call(
        _cumsum_kernel,
        grid=(b,),
        in_specs=[pl.BlockSpec((1, r, t), lambda i: (i, 0, 0))],
        out_specs=pl.BlockSpec((1, r, t), lambda i: (i, 0, 0)),
        out_shape=jax.ShapeDtypeStruct((b, r, t), F32),
        compiler_params=_params(("arbitrary",)),
        name="forget_cumsum",
    )(x)


def _pair_layouts(cum_t, t_query_start, t_query):
    b, _, tp = cum_t.shape
    f_keys = cum_t.reshape(b, HEAD_PAIRS, 2, tp)
    f_query = jnp.swapaxes(f_keys[:, :, :, t_query_start:t_query_start + t_query], 2, 3)
    return f_keys, f_query


def _head_masks():
    lane = lax.broadcasted_iota(jnp.int32, (1, LANES), 1)
    return lane < HEAD_DIM


def _split_heads(q2, first_head):
    zero = jnp.zeros_like(q2)
    return jnp.where(first_head, q2, zero), jnp.where(first_head, zero, q2)


def _strict_lower_neg(n):
    r = lax.broadcasted_iota(jnp.int32, (n, n), 0)
    c = lax.broadcasted_iota(jnp.int32, (n, n), 1)
    return jnp.where(r > c, -1.0, 0.0).astype(BF16)


def _sb_block(qh, k2, v2, neg_tri, visible, acc_ref, r_ref, hd):
    z = _dot_nt(qh, k2)
    sp = _softplus(z)
    if visible is not None:
        sp = jnp.where(visible, sp, 0.0)
    later = _dot(sp.astype(BF16), neg_tri)
    p = jnp.exp(z - sp + later)
    if visible is not None:
        p = jnp.where(visible, p, 0.0)
    r = r_ref[hd]
    acc_ref[hd] += jnp.exp(r) * _dot(p.astype(BF16), v2)
    r_ref[hd] = r + later[:, 0:1] - sp[:, 0:1]


def _fox_block(qh, k2, v2_ones, fq, fk, allowed, acc_ref, m_ref, hd):
    s = _dot_nt(qh, k2) + fq - fk
    if allowed is not None:
        s = jnp.where(allowed, s, -jnp.inf)
    m_old = m_ref[hd]
    m_new = jnp.maximum(m_old, jnp.max(s, axis=-1, keepdims=True))
    p = jnp.exp2(s - m_new)
    acc_ref[hd] = jnp.exp2(m_old - m_new) * acc_ref[hd] + _dot(p.astype(BF16), v2_ones)
    m_ref[hd] = m_new


def _fox_finish(acc_ref, first_head):
    a, b = acc_ref[0], acc_ref[1]
    num = jnp.where(first_head, a, b)
    den = jnp.where(first_head, pltpu.roll(a, HEAD_DIM, 1), pltpu.roll(b, HEAD_DIM, 1))
    return num / den


def _with_ones(v2, first_head):
    one = jnp.ones_like(v2)
    return jnp.where(first_head, v2, one), jnp.where(first_head, one, v2)


def _sb_prompt_kernel(qt_ref, k_ref, vt_ref, o_ref, w_ref, acc_ref, r_ref, *, tq):
    i = pl.program_id(2)
    sub = lax.broadcasted_iota(jnp.int32, (LANES, tq), 0)
    first_rows = sub < HEAD_DIM
    qt = qt_ref[...]
    zero = jnp.zeros_like(qt)
    w_ref[0] = jnp.where(first_rows, qt, zero)
    w_ref[1] = jnp.where(first_rows, zero, qt)
    acc_ref[...] = jnp.zeros_like(acc_ref)
    r_ref[...] = jnp.zeros_like(r_ref)
    key_idx = lax.broadcasted_iota(jnp.int32, (tq, tq), 0)
    query_idx = lax.broadcasted_iota(jnp.int32, (tq, tq), 1)
    visible = key_idx < query_idx
    neg_tri = jnp.where(query_idx > key_idx, -1.0, 0.0).astype(BF16)

    def block(j, vis):
        start = pl.multiple_of(j * tq, tq)
        keys = k_ref[0, pl.ds(start, tq), :]
        vt = vt_ref[:, pl.ds(start, tq)]
        for hd in range(2):
            z = _dot(keys, w_ref[hd])
            sp = _softplus(z)
            if vis is not None:
                sp = jnp.where(vis, sp, 0.0)
            later = _dot(neg_tri, sp.astype(BF16))
            p = jnp.exp(z - sp + later)
            if vis is not None:
                p = jnp.where(vis, p, 0.0)
            r = r_ref[hd]
            acc_ref[hd] += jnp.exp(r) * _dot(vt, p.astype(BF16))
            r_ref[hd] = r + later[0:1, :] - sp[0:1, :]

    def live():
        return jnp.max(jnp.maximum(r_ref[0], r_ref[1]))

    block(i, visible)

    def cond(carry):
        j, r_max = carry
        return jnp.logical_and(j >= 0, r_max > SB_DEAD_LOG)

    def body(carry):
        j, _ = carry
        block(j, None)
        return j - 1, live()

    lax.while_loop(cond, body, (i - 1, live()))
    o_ref[0] = jnp.where(first_rows, acc_ref[0], acc_ref[1]).T.astype(o_ref.dtype)


def _sb_prompt(qkv, qv_t, tq):
    b, t, _ = qkv.shape
    nq = t // tq
    kernel = functools.partial(_sb_prompt_kernel, tq=tq)
    return pl.pallas_call(
        kernel,
        grid=(b, HEAD_PAIRS, nq),
        in_specs=[pl.BlockSpec((LANES, tq), lambda b_, p, i: (p, b_ * nq + i)),
                  pl.BlockSpec((1, t, LANES), lambda b_, p, i: (b_, 0, HEAD_PAIRS + p)),
                  pl.BlockSpec((LANES, t), lambda b_, p, i: (HEAD_PAIRS + p, b_))],
        out_specs=pl.BlockSpec((1, tq, LANES), lambda b_, p, i: (b_, i, p)),
        out_shape=jax.ShapeDtypeStruct((b, t, WIDTH), BF16),
        scratch_shapes=[pltpu.VMEM((2, LANES, tq), BF16), pltpu.VMEM((2, LANES, tq), F32),
                        pltpu.VMEM((2, 1, tq), F32)],
        compiler_params=_params(("arbitrary", "arbitrary", "arbitrary")),
        name="sb_attention_prompt",
    )(qv_t, qkv, qv_t)


FORGET_SPLIT = 3


def _forget_key_columns(cum_t):
    b, _, t = cum_t.shape
    rest = -cum_t
    terms = []
    for _ in range(FORGET_SPLIT):
        part = lax.bitcast_convert_type(lax.bitcast_convert_type(rest, jnp.uint32) & jnp.uint32(0xFFFF0000), F32)
        terms.append(part.astype(BF16))
        rest = rest - part
    cols = jnp.stack(terms, axis=2).reshape(b, HEAD_PAIRS, 2 * FORGET_SPLIT, t)
    cols = jnp.swapaxes(cols, 2, 3)
    return jnp.pad(cols, ((0, 0), (0, 0), (0, 0), (0, LANES - 2 * FORGET_SPLIT)))


FOX_DEAD_LOG2 = -150.0


def _fox_prompt_kernel(kmax_ref, fend_ref, qt_ref, k_ref, kf_ref, vt_ref, fq_ref, o_ref,
                       w_ref, acc_ref, m_ref, zb_ref, *, tq):
    bi = pl.program_id(0)
    pi = pl.program_id(1)
    i = pl.program_id(2)
    sub = lax.broadcasted_iota(jnp.int32, (LANES, tq), 0)
    first_rows = sub < HEAD_DIM
    qt = qt_ref[...]
    zero = jnp.zeros_like(qt)
    for hd in range(2):
        lo = hd * FORGET_SPLIT
        select = jnp.where(jnp.logical_and(sub >= lo, sub < lo + FORGET_SPLIT), 1.0, 0.0).astype(BF16)
        q_head = jnp.where(first_rows, qt, zero) if hd == 0 else jnp.where(first_rows, zero, qt)
        w_ref[hd, :LANES, :] = q_head
        w_ref[hd, LANES:, :] = select
        q_f32 = q_head.astype(F32)
        q_norm = jnp.sqrt(jnp.sum(q_f32 * q_f32, axis=0, keepdims=True))
        zb_ref[hd] = q_norm * (kmax_ref[bi, 2 * pi + hd] * 1.01) + fq_ref[0, 0, hd:hd + 1, :]
    acc_ref[...] = jnp.zeros_like(acc_ref)
    m_ref[...] = jnp.full_like(m_ref, -jnp.inf)
    key_idx = lax.broadcasted_iota(jnp.int32, (tq, tq), 0)
    query_idx = lax.broadcasted_iota(jnp.int32, (tq, tq), 1)
    allowed = key_idx <= query_idx

    def block(j, mask):
        start = pl.multiple_of(j * tq, tq)
        keys = jnp.concatenate([k_ref[0, pl.ds(start, tq), :], kf_ref[0, 0, pl.ds(start, tq), :]], axis=1)
        vt = vt_ref[:, pl.ds(start, tq)]
        one = jnp.ones_like(vt)
        first_v = lax.broadcasted_iota(jnp.int32, vt.shape, 0) < HEAD_DIM
        vts = (jnp.where(first_v, vt, one), jnp.where(first_v, one, vt))
        for hd in range(2):
            u = _dot(keys, w_ref[hd])
            if mask is not None:
                u = jnp.where(mask, u, -jnp.inf)
            fq = fq_ref[0, 0, hd:hd + 1, :]
            m_old = m_ref[hd]
            m_new = jnp.maximum(m_old, jnp.max(u, axis=0, keepdims=True) + fq)
            p = jnp.exp2(u + (fq - m_new))
            acc_ref[hd] = jnp.exp2(m_old - m_new) * acc_ref[hd] + _dot(vts[hd], p.astype(BF16))
            m_ref[hd] = m_new

    def alive(j):
        jc = jnp.maximum(j, 0)
        reach = [jnp.max(zb_ref[hd] - m_ref[hd]) - fend_ref[bi, pi, hd, jc] for hd in range(2)]
        return jnp.maximum(reach[0], reach[1]) > FOX_DEAD_LOG2

    block(i, allowed)

    def cond(carry):
        j, go = carry
        return jnp.logical_and(j >= 0, go)

    def body(carry):
        j, _ = carry
        block(j, None)
        return j - 1, alive(j - 1)

    lax.while_loop(cond, body, (i - 1, alive(i - 1)))
    a, b = acc_ref[0], acc_ref[1]
    out_t = jnp.where(first_rows, a / a[HEAD_DIM:HEAD_DIM + 1, :], b / b[0:1, :])
    o_ref[0] = out_t.T.astype(o_ref.dtype)


def _fox_prompt(qkv, qv_t, cum_t, tq):
    b, t, _ = qkv.shape
    nq = t // tq
    f_rows = cum_t.reshape(b, HEAD_PAIRS, 2, t)
    f_block_end = f_rows[:, :, :, tq - 1::tq]
    k_fx = qkv[:, :, 4 * WIDTH:5 * WIDTH].astype(F32).reshape(b, t, N_HEADS, HEAD_DIM)
    k_max = jnp.sqrt(jnp.max(jnp.sum(k_fx * k_fx, axis=-1), axis=1))
    smem = pl.BlockSpec(memory_space=pltpu.SMEM)
    kernel = functools.partial(_fox_prompt_kernel, tq=tq)
    return pl.pallas_call(
        kernel,
        grid=(b, HEAD_PAIRS, nq),
        in_specs=[smem, smem,
                  pl.BlockSpec((LANES, tq), lambda b_, p, i: (2 * HEAD_PAIRS + p, b_ * nq + i)),
                  pl.BlockSpec((1, t, LANES), lambda b_, p, i: (b_, 0, 4 * HEAD_PAIRS + p)),
                  pl.BlockSpec((1, 1, t, LANES), lambda b_, p, i: (b_, p, 0, 0)),
                  pl.BlockSpec((LANES, t), lambda b_, p, i: (3 * HEAD_PAIRS + p, b_)),
                  pl.BlockSpec((1, 1, 2, tq), lambda b_, p, i: (b_, p, 0, i))],
        out_specs=pl.BlockSpec((1, tq, LANES), lambda b_, p, i: (b_, i, p)),
        out_shape=jax.ShapeDtypeStruct((b, t, WIDTH), BF16),
        scratch_shapes=[pltpu.VMEM((2, 2 * LANES, tq), BF16), pltpu.VMEM((2, LANES, tq), F32),
                        pltpu.VMEM((2, 1, tq), F32), pltpu.VMEM((2, 1, tq), F32)],
        compiler_params=_params(("arbitrary", "arbitrary", "arbitrary")),
        name="fox_attention_prompt",
    )(k_max, f_block_end, qv_t, qkv, _forget_key_columns(cum_t), qv_t, f_rows)


def _sample_attn_kernel(qsb_ref, ksb_ref, vsb_ref, qfx_ref, kfx_ref, vfx_ref,
                        csk_ref, csv_ref, cfk_ref, cfv_ref, fq_ref, fk_ref,
                        osb_ref, ofx_ref, acc_ref, st_ref, *, tk):
    t_new = qsb_ref.shape[1]
    n_past = csk_ref.shape[1]
    n_blocks = n_past // tk
    first_head = _head_masks()
    row = lax.broadcasted_iota(jnp.int32, (t_new, t_new), 0)
    col = lax.broadcasted_iota(jnp.int32, (t_new, t_new), 1)

    qs = _split_heads(qsb_ref[0], first_head)
    acc_ref[...] = jnp.zeros_like(acc_ref)
    st_ref[...] = jnp.zeros_like(st_ref)
    tri_new = _strict_lower_neg(t_new)
    tri_past = _strict_lower_neg(tk)
    for hd in range(2):
        _sb_block(qs[hd], ksb_ref[0], vsb_ref[0], tri_new, col < row, acc_ref, st_ref, hd)

    def live():
        return jnp.max(jnp.maximum(st_ref[0], st_ref[1]))

    def sb_cond(carry):
        j, r_max = carry
        return jnp.logical_and(j >= 0, r_max > SB_DEAD_LOG)

    def sb_body(carry):
        j, _ = carry
        start = pl.multiple_of(j * tk, tk)
        k2 = csk_ref[0, pl.ds(start, tk), :].astype(BF16)
        v2 = csv_ref[0, pl.ds(start, tk), :].astype(BF16)
        for hd in range(2):
            _sb_block(qs[hd], k2, v2, tri_past, None, acc_ref, st_ref, hd)
        return j - 1, live()

    lax.while_loop(sb_cond, sb_body, (n_blocks - 1, live()))
    osb_ref[0] = jnp.where(first_head, acc_ref[0], acc_ref[1]).astype(osb_ref.dtype)

    qs = _split_heads(qfx_ref[0], first_head)
    acc_ref[...] = jnp.zeros_like(acc_ref)
    st_ref[...] = jnp.full_like(st_ref, -jnp.inf)
    fq = fq_ref[0, 0]
    vs = _with_ones(vfx_ref[0], first_head)
    for hd in range(2):
        fk = fk_ref[0, 0, hd:hd + 1, n_past:n_past + t_new]
        _fox_block(qs[hd], kfx_ref[0], vs[hd], fq[:, hd:hd + 1], fk, col <= row, acc_ref, st_ref, hd)

    k2 = cfk_ref[0].astype(BF16)
    vs_ = _with_ones(cfv_ref[0].astype(BF16), first_head)
    for hd in range(2):
        fk = fk_ref[0, 0, hd:hd + 1, 0:n_past]
        _fox_block(qs[hd], k2, vs_[hd], fq[:, hd:hd + 1], fk, None, acc_ref, st_ref, hd)
    ofx_ref[0] = _fox_finish(acc_ref, first_head).astype(ofx_ref.dtype)


def _sample_attention(qkv, cache_sb_k, cache_sb_v, cache_fox_k, cache_fox_v, f_query, f_keys, tk):
    b, t_new, _ = qkv.shape
    n_past = cache_sb_k.shape[1]
    tp = f_keys.shape[3]
    new = lambda c: pl.BlockSpec((1, t_new, LANES), lambda b_, p: (b_, 0, c * HEAD_PAIRS + p))
    past = pl.BlockSpec((1, n_past, LANES), lambda b_, p: (b_, 0, p))
    out = pl.BlockSpec((1, t_new, LANES), lambda b_, p: (b_, 0, p))
    kernel = functools.partial(_sample_attn_kernel, tk=tk)
    return pl.pallas_call(
        kernel,
        grid=(b, HEAD_PAIRS),
        in_specs=[new(0), new(1), new(2), new(3), new(4), new(5), past, past, past, past,
                  pl.BlockSpec((1, 1, t_new, 2), lambda b_, p: (b_, p, 0, 0)),
                  pl.BlockSpec((1, 1, 2, tp), lambda b_, p: (b_, p, 0, 0))],
        out_specs=[out, out],
        out_shape=[jax.ShapeDtypeStruct((b, t_new, WIDTH), BF16)] * 2,
        scratch_shapes=[pltpu.VMEM((2, t_new, LANES), F32), pltpu.VMEM((2, t_new, 1), F32)],
        compiler_params=_params(("arbitrary", "arbitrary")),
        name="attention_sample",
    )(qkv, qkv, qkv, qkv, qkv, qkv, cache_sb_k, cache_sb_v, cache_fox_k, cache_fox_v, f_query, f_keys)


def _mix_out_kernel(x_ref, osb_ref, ofx_ref, sg_ref, gate_ref, g_ref, wsb_ref, wfx_ref, wo_ref, o_ref):
    d = x_ref.shape[1]
    merged = (sg_ref[:, :d].astype(F32) * _dot(osb_ref[...], wsb_ref[...])
              + sg_ref[:, d:].astype(F32) * _dot(ofx_ref[...], wfx_ref[...]))
    y = _dot(merged.astype(BF16), wo_ref[...])
    o_ref[...] = x_ref[...] + gate_ref[0] * _rms(y, g_ref[...])


def _mix_out(x, o_sb, o_fx, sg, gate, g_post, w_sb_out, w_fox_out, w_out, tm):
    n, d = x.shape
    tiles_per_group = (n // tm) // gate.shape[0]
    const = lambda i: (0, 0)
    row = lambda i: (i, 0)
    return pl.pallas_call(
        _mix_out_kernel,
        grid=(n // tm,),
        in_specs=[pl.BlockSpec((tm, d), row), pl.BlockSpec((tm, WIDTH), row), pl.BlockSpec((tm, WIDTH), row),
                  pl.BlockSpec((tm, 2 * d), row), _mod_spec(gate, tm, tiles_per_group),
                  pl.BlockSpec((1, d), const), pl.BlockSpec(w_sb_out.shape, const),
                  pl.BlockSpec(w_fox_out.shape, const), pl.BlockSpec(w_out.shape, const)],
        out_specs=pl.BlockSpec((tm, d), row),
        out_shape=jax.ShapeDtypeStruct((n, d), F32),
        compiler_params=_params(("arbitrary",)),
        name="mix_out",
    )(x, o_sb, o_fx, sg, gate, g_post, w_sb_out, w_fox_out, w_out)


def _swiglu_chunk(hb, wa, wb, wd):
    a = _dot(hb, wa)
    b = _dot(hb, wb)
    return a * _sigmoid(a) * b, wd


def _ffn_kernel(x_ref, gpre_ref, shift_ref, scale_ref, gate_ref, gpost_ref, wa_ref, wb_ref, wd_ref,
                o_ref, h_ref, acc_ref):
    f = pl.program_id(1)

    @pl.when(f == 0)
    def _():
        h = _rms(x_ref[...], gpre_ref[...]) * (1.0 + scale_ref[0]) + shift_ref[0]
        h_ref[...] = h.astype(BF16)
        acc_ref[...] = jnp.zeros_like(acc_ref)

    hb = h_ref[...]
    a = _dot(hb, wa_ref[...])
    b = _dot(hb, wb_ref[...])
    act = a * _sigmoid(a) * b
    acc_ref[...] += _dot(act.astype(BF16), wd_ref[...])

    @pl.when(f == pl.num_programs(1) - 1)
    def _():
        o_ref[...] = x_ref[...] + gate_ref[0] * _rms(acc_ref[...], gpost_ref[...])


def _ffn(x, g_pre, shift, scale, gate, g_post, w_gate_up, w_down, tm, tf):
    n, d = x.shape
    d_ff = w_down.shape[0]
    nf = d_ff // tf
    tiles_per_group = (n // tm) // shift.shape[0]
    const = lambda i, f: (0, 0)
    row = lambda i, f: (i, 0)
    return pl.pallas_call(
        _ffn_kernel,
        grid=(n // tm, nf),
        in_specs=[pl.BlockSpec((tm, d), row), pl.BlockSpec((1, d), const),
                  _mod_spec(shift, tm, tiles_per_group), _mod_spec(scale, tm, tiles_per_group),
                  _mod_spec(gate, tm, tiles_per_group), pl.BlockSpec((1, d), const),
                  pl.BlockSpec((d, tf), lambda i, f: (0, f)),
                  pl.BlockSpec((d, tf), lambda i, f: (0, nf + f)),
                  pl.BlockSpec((tf, d), lambda i, f: (f, 0))],
        out_specs=pl.BlockSpec((tm, d), row),
        out_shape=jax.ShapeDtypeStruct((n, d), F32),
        scratch_shapes=[pltpu.VMEM((tm, d), BF16), pltpu.VMEM((tm, d), F32)],
        compiler_params=_params(("arbitrary", "arbitrary")),
        name="ffn_dense",
    )(x, g_pre, shift, scale, gate, g_post, w_gate_up, w_gate_up, w_down)


def _route_kernel(x_ref, gpre_ref, shift_ref, scale_ref, wr_ref, h_ref, idx_ref, w_ref, *, n_experts):
    h = _rms(x_ref[...], gpre_ref[...]) * (1.0 + scale_ref[0]) + shift_ref[0]
    h_ref[...] = h
    logits = jnp.dot(h, wr_ref[...], precision=lax.Precision.HIGHEST, preferred_element_type=F32)[:, :n_experts]
    idx = lax.broadcasted_iota(jnp.int32, logits.shape, 1)
    m1 = jnp.max(logits, axis=-1, keepdims=True)
    i1 = jnp.min(jnp.where(logits == m1, idx, n_experts), axis=-1, keepdims=True)
    rest = jnp.where(idx == i1, -jnp.inf, logits)
    m2 = jnp.max(rest, axis=-1, keepdims=True)
    i2 = jnp.min(jnp.where(rest == m2, idx, n_experts), axis=-1, keepdims=True)
    e2 = jnp.exp(m2 - m1)
    slot = lax.broadcasted_iota(jnp.int32, idx_ref.shape, 1)
    idx_ref[...] = jnp.where(slot == 0, i1, i2)
    w_ref[...] = jnp.where(slot == 0, 1.0 / (1.0 + e2), e2 / (1.0 + e2))


def _route(x, g_pre, shift, scale, w_router, n_experts, tm):
    n, d = x.shape
    tiles_per_group = (n // tm) // shift.shape[0]
    const = lambda i: (0, 0)
    row = lambda i: (i, 0)
    return pl.pallas_call(
        functools.partial(_route_kernel, n_experts=n_experts),
        grid=(n // tm,),
        in_specs=[pl.BlockSpec((tm, d), row), pl.BlockSpec((1, d), const),
                  _mod_spec(shift, tm, tiles_per_group), _mod_spec(scale, tm, tiles_per_group),
                  pl.BlockSpec(w_router.shape, const)],
        out_specs=[pl.BlockSpec((tm, d), row), pl.BlockSpec((tm, TOP_K), row), pl.BlockSpec((tm, TOP_K), row)],
        out_shape=[jax.ShapeDtypeStruct((n, d), F32), jax.ShapeDtypeStruct((n, TOP_K), jnp.int32),
                   jax.ShapeDtypeStruct((n, TOP_K), F32)],
        compiler_params=_params(("arbitrary",)),
        name="moe_route",
    )(x, g_pre, shift, scale, w_router)


def _dispatch_tables(top_idx, n_experts, tm):
    n = top_idx.shape[0]
    n_assign = n * TOP_K
    expert = top_idx.reshape(n_assign)
    onehot = (expert[:, None] == jnp.arange(n_experts, dtype=jnp.int32)[None, :]).astype(jnp.int32)
    rank = jnp.sum((jnp.cumsum(onehot, axis=0) - onehot) * onehot, axis=1)
    counts = jnp.sum(onehot, axis=0)
    padded = (counts + tm - 1) // tm * tm
    group_end = jnp.cumsum(padded)
    pos = (group_end - padded)[expert] + rank
    p_rows = n_assign + n_experts * tm
    src_token = jnp.zeros((p_rows,), jnp.int32).at[pos].set(jnp.arange(n_assign, dtype=jnp.int32) // TOP_K)
    tile_start = jnp.arange(p_rows // tm, dtype=jnp.int32) * tm
    tile_expert = jnp.minimum(jnp.sum(tile_start[:, None] >= group_end[None, :], axis=1), n_experts - 1)
    n_used = (group_end[-1] // tm).reshape(1)
    return src_token, tile_expert.astype(jnp.int32), n_used.astype(jnp.int32), pos.astype(jnp.int32)


def _gather_rows(src_hbm, dst, sem, index_of_row, n_rows):
    def issue(r, carry):
        pltpu.make_async_copy(src_hbm.at[pl.ds(index_of_row(r), 1), :], dst.at[pl.ds(r, 1), :], sem).start()
        return carry

    lax.fori_loop(0, n_rows, issue, 0, unroll=8)


def _wait_rows(src_hbm, dst, sem):
    pltpu.make_async_copy(src_hbm.at[pl.ds(0, dst.shape[0]), :], dst, sem).wait()


def _experts_kernel(src_ref, expert_ref, used_ref, h_hbm, wa_ref, wb_ref, wd_ref, o_ref,
                    rows_ref, xb_ref, acc_ref, sem, *, tm):
    i = pl.program_id(0)
    f = pl.program_id(1)
    n_used = used_ref[0]
    slot = lax.rem(i, 2)
    active = i < n_used

    def gather(tile, into):
        _gather_rows(h_hbm, rows_ref.at[into], sem.at[into], lambda r: src_ref[tile * tm + r], tm)

    @pl.when(jnp.logical_and(f == 0, active))
    def _():
        @pl.when(i == 0)
        def _():
            gather(0, 0)

        _wait_rows(h_hbm, rows_ref.at[slot], sem.at[slot])

        @pl.when(i + 1 < n_used)
        def _():
            gather(i + 1, 1 - slot)

        xb_ref[...] = rows_ref[slot].astype(BF16)
        acc_ref[...] = jnp.zeros_like(acc_ref)

    @pl.when(active)
    def _():
        xb = xb_ref[...]
        a = _dot(xb, wa_ref[0])
        b = _dot(xb, wb_ref[0])
        act = a * _sigmoid(a) * b
        acc_ref[...] += _dot(act.astype(BF16), wd_ref[0])

    last = f == pl.num_programs(1) - 1

    @pl.when(jnp.logical_and(last, active))
    def _():
        o_ref[...] = acc_ref[...]

    @pl.when(jnp.logical_and(last, jnp.logical_not(active)))
    def _():
        o_ref[...] = jnp.zeros_like(o_ref)


def _experts(h, src_token, tile_expert, n_used, w_gate_up, w_down, tm, tf):
    p_rows = src_token.shape[0]
    d = h.shape[1]
    nf = w_down.shape[1] // tf
    ff = lambda i, f, used: jnp.where(i < used[0], f, nf - 1)
    grid_spec = pltpu.PrefetchScalarGridSpec(
        num_scalar_prefetch=3,
        grid=(p_rows // tm, nf),
        in_specs=[pl.BlockSpec(memory_space=pl.ANY),
                  pl.BlockSpec((1, d, tf), lambda i, f, src, ex, used: (ex[i], 0, ff(i, f, used))),
                  pl.BlockSpec((1, d, tf), lambda i, f, src, ex, used: (ex[i], 0, nf + ff(i, f, used))),
                  pl.BlockSpec((1, tf, d), lambda i, f, src, ex, used: (ex[i], ff(i, f, used), 0))],
        out_specs=pl.BlockSpec((tm, d), lambda i, f, src, ex, used: (i, 0)),
        scratch_shapes=[pltpu.VMEM((2, tm, d), F32), pltpu.VMEM((tm, d), BF16), pltpu.VMEM((tm, d), F32),
                        pltpu.SemaphoreType.DMA((2,))])
    return pl.pallas_call(
        functools.partial(_experts_kernel, tm=tm),
        grid_spec=grid_spec,
        out_shape=jax.ShapeDtypeStruct((p_rows, d), F32),
        compiler_params=_params(("arbitrary", "arbitrary")),
        name="moe_experts",
    )(src_token, tile_expert, n_used, h, w_gate_up, w_gate_up, w_down)


def _combine_kernel(pos_ref, y_hbm, x_ref, w_ref, gate_ref, gpost_ref, o_ref, rows_ref, sem, *, tm):
    i = pl.program_id(0)
    slot = lax.rem(i, 2)

    def gather(tile, into):
        for k in range(TOP_K):
            _gather_rows(y_hbm, rows_ref.at[into, k], sem.at[into],
                         lambda r, k=k: pos_ref[(tile * tm + r) * TOP_K + k], tm)

    @pl.when(i == 0)
    def _():
        gather(0, 0)

    for k in range(TOP_K):
        _wait_rows(y_hbm, rows_ref.at[slot, k], sem.at[slot])

    @pl.when(i + 1 < pl.num_programs(0))
    def _():
        gather(i + 1, 1 - slot)

    w = w_ref[...]
    mixed = w[:, 0:1] * rows_ref[slot, 0] + w[:, 1:2] * rows_ref[slot, 1]
    o_ref[...] = x_ref[...] + gate_ref[0] * _rms(mixed, gpost_ref[...])


def _combine(y, pos, top_w, x, gate, g_post, tm):
    n, d = x.shape
    tiles_per_group = (n // tm) // gate.shape[0]
    grid_spec = pltpu.PrefetchScalarGridSpec(
        num_scalar_prefetch=1,
        grid=(n // tm,),
        in_specs=[pl.BlockSpec(memory_space=pl.ANY),
                  pl.BlockSpec((tm, d), lambda i, pos_: (i, 0)),
                  pl.BlockSpec((tm, TOP_K), lambda i, pos_: (i, 0)),
                  _mod_spec(gate, tm, tiles_per_group),
                  pl.BlockSpec((1, d), lambda i, pos_: (0, 0))],
        out_specs=pl.BlockSpec((tm, d), lambda i, pos_: (i, 0)),
        scratch_shapes=[pltpu.VMEM((2, TOP_K, tm, d), F32), pltpu.SemaphoreType.DMA((2,))])
    return pl.pallas_call(
        functools.partial(_combine_kernel, tm=tm),
        grid_spec=grid_spec,
        out_shape=jax.ShapeDtypeStruct((n, d), F32),
        compiler_params=_params(("arbitrary",)),
        name="moe_combine",
    )(pos, y, x, top_w, gate, g_post)


def _moe(x, g_pre, shift, scale, gate, g_post, w_router, w_gate_up, w_down, n_experts, tm_route, tm, tf, tm_out):
    h, top_idx, top_w = _route(x, g_pre, shift, scale, w_router, n_experts, tm_route)
    src_token, tile_expert, n_used, pos = _dispatch_tables(top_idx, n_experts, tm)
    y = _experts(h, src_token, tile_expert, n_used, w_gate_up, w_down, tm, tf)
    return _combine(y, pos, top_w, x, gate, g_post, tm_out)


def _row_tile(n, want):
    return want if n % want == 0 else n


def kernel(x_prompt, x_sample, c_prompt, c_sample, cache_sb_k, cache_sb_v, cache_fox_k, cache_fox_v, cache_fox_logf, w_mod, b_mod, g_pre_mix, g_post_mix, g_pre_ffn, g_post_ffn, w_in, b_forget, w_sb_out, w_fox_out, w_out, w_ffn_gate_up, w_ffn_down, w_router, w_moe_gate_up, w_moe_down):
    bsz, seq, d = x_prompt.shape
    dec_b, dec_t, _ = x_sample.shape
    depth = w_mod.shape[0]
    n_past = cache_sb_k.shape[2]
    n_experts = w_router.shape[2]
    n_p, n_s = bsz * seq, dec_b * dec_t
    tq = _row_tile(seq, 256)
    past_chunk = _row_tile(n_past, 256)

    c_all = jnp.concatenate([c_prompt, c_sample], axis=0)
    c_rows = -(-c_all.shape[0] // 8) * 8
    c_all = jnp.pad(c_all, ((0, c_rows - c_all.shape[0]), (0, 0)))
    mod = _modulation(c_all, w_mod, b_mod)

    xp = x_prompt.reshape(n_p, d)
    xs = x_sample.reshape(n_s, d)
    stacked_p = stacked_s = None
    for l in range(depth):
        mod_p = mod[l, :bsz].reshape(bsz, 1, 6, d)
        mod_s = jnp.repeat(mod[l, bsz:bsz + dec_b].reshape(dec_b, 6, d), dec_t, axis=0)[None]
        mp = [mod_p[:, :, i] for i in range(6)]
        ms = [mod_s[:, :, i] for i in range(6)]
        vec = lambda a: a[l].reshape(1, -1)

        wqkv = w_in[l, :, :6 * WIDTH].astype(BF16)
        wf = jnp.pad(w_in[l, :, 6 * WIDTH:6 * WIDTH + N_HEADS], ((0, 0), (0, LANES - N_HEADS))).astype(BF16)
        wg = w_in[l, :, 6 * WIDTH + N_HEADS:].astype(BF16)
        bf = b_forget[l].reshape(1, N_HEADS)
        wsb, wfx, wo = w_sb_out[l].astype(BF16), w_fox_out[l].astype(BF16), w_out[l].astype(BF16)

        qkv_p, qvt_p, *stacked_p, sg_p = _in_projection(
            xp, vec(g_pre_mix), mp[0], mp[1], wqkv, wf, wg, bf, _row_tile(seq, 256), l, depth, stacked_p)
        lf_p = stacked_p[4][l]
        cum_p = _cumsum_lanes(jnp.swapaxes(lf_p.reshape(bsz, seq, N_HEADS), 1, 2)) * LOG2_E
        qkv_p3 = qkv_p.reshape(bsz, seq, 6 * WIDTH)
        osb_p = _sb_prompt(qkv_p3, qvt_p, tq)
        ofx_p = _fox_prompt(qkv_p3, qvt_p, cum_p, _row_tile(seq, 512))
        xp = _mix_out(xp, osb_p.reshape(n_p, WIDTH), ofx_p.reshape(n_p, WIDTH), sg_p, mp[2], vec(g_post_mix),
                      wsb, wfx, wo, _row_tile(seq, 512))

        qkv_s, _, *stacked_s, sg_s = _in_projection(
            xs, vec(g_pre_mix), ms[0], ms[1], wqkv, wf, wg, bf, n_s, l, depth, stacked_s)
        lf_s = stacked_s[4][l]
        t_all = n_past + dec_t
        t_pad = -(-t_all // CUMSUM_CHUNK) * CUMSUM_CHUNK
        lf_all = jnp.concatenate([cache_fox_logf[l].astype(F32), lf_s.reshape(dec_b, dec_t, N_HEADS),
                                  jnp.zeros((dec_b, t_pad - t_all, N_HEADS), F32)], axis=1)
        cum_s = _cumsum_lanes(jnp.swapaxes(lf_all, 1, 2)) * LOG2_E
        fk_s, fq_s = _pair_layouts(cum_s, n_past, dec_t)
        cache = lambda a: a[l].reshape(dec_b, n_past, WIDTH)
        osb_s, ofx_s = _sample_attention(qkv_s.reshape(dec_b, dec_t, 6 * WIDTH), cache(cache_sb_k), cache(cache_sb_v),
                                         cache(cache_fox_k), cache(cache_fox_v), fq_s, fk_s, past_chunk)
        xs = _mix_out(xs, osb_s.reshape(n_s, WIDTH), ofx_s.reshape(n_s, WIDTH), sg_s, ms[2], vec(g_post_mix),
                      wsb, wfx, wo, n_s)

        if l % 2 == 0:
            wgu, wd = w_ffn_gate_up[l // 2].astype(BF16), w_ffn_down[l // 2].astype(BF16)
            tf = 256
            xp = _ffn(xp, vec(g_pre_ffn), mp[3], mp[4], mp[5], vec(g_post_ffn), wgu, wd, _row_tile(seq, 1024), tf)
            xs = _ffn(xs, vec(g_pre_ffn), ms[3], ms[4], ms[5], vec(g_post_ffn), wgu, wd, n_s, tf)
        else:
            wr = jnp.pad(w_router[l // 2], ((0, 0), (0, LANES - n_experts)))
            wgu, wd = w_moe_gate_up[l // 2].astype(BF16), w_moe_down[l // 2].astype(BF16)
            tf = _row_tile(wd.shape[1], 896)
            xp = _moe(xp, vec(g_pre_ffn), mp[3], mp[4], mp[5], vec(g_post_ffn), wr, wgu, wd, n_experts,
                      _row_tile(seq, 512), 1024, tf, _row_tile(seq, 256))
            xs = _moe(xs, vec(g_pre_ffn), ms[3], ms[4], ms[5], vec(g_post_ffn), wr, wgu, wd, n_experts,
                      n_s, 128, tf, n_s)

    split_p = [a.reshape(depth, bsz, seq, *a.shape[2:]) for a in stacked_p]
    split_s = [a.reshape(depth, dec_b, dec_t, *a.shape[2:]) for a in stacked_s]
    return (xp.reshape(bsz, seq, d), xs.reshape(dec_b, dec_t, d), *split_p, *split_s)
```

```python
import functools

import jax
import jax.numpy as jnp
from jax import lax
from jax.experimental import pallas as pl
from jax.experimental.pallas import tpu as pltpu

F32 = jnp.float32
BF16 = jnp.bfloat16

HEAD_DIM = 64
N_HEADS = 8
LANES = 128
HEAD_PAIRS = N_HEADS * HEAD_DIM // LANES
WIDTH = N_HEADS * HEAD_DIM
TOP_K = 2
RMS_EPS = 1e-6
LOG2_E = 1.4426950408889634
SB_DEAD_LOG = -104.0
VMEM_LIMIT = 56 * 1024 * 1024


def _params(sem, vmem=VMEM_LIMIT):
    return pltpu.CompilerParams(dimension_semantics=sem, vmem_limit_bytes=vmem)


def _dot(a, b):
    return jnp.dot(a, b, preferred_element_type=F32)


def _dot_nt(a, b):
    return lax.dot_general(a, b, (((1,), (1,)), ((), ())), preferred_element_type=F32)


def _sigmoid(x):
    return 1.0 / (1.0 + jnp.exp(-x))


def _softplus(x):
    return jnp.maximum(x, 0.0) + jnp.log(1.0 + jnp.exp(-jnp.abs(x)))


def _rms(x, g):
    return x * lax.rsqrt(jnp.mean(x * x, axis=-1, keepdims=True) + RMS_EPS) * g


def _mod_kernel(c_ref, w_ref, b_ref, o_ref):
    c = c_ref[...]
    s = c * _sigmoid(c)
    o_ref[0] = jnp.dot(s, w_ref[0], precision=lax.Precision.HIGHEST, preferred_element_type=F32) + b_ref[0]


def _modulation(c_all, w_mod, b_mod):
    depth, d, d6 = w_mod.shape
    rows = c_all.shape[0]
    tn = 1024
    return pl.pallas_call(
        _mod_kernel,
        grid=(depth, d6 // tn),
        in_specs=[pl.BlockSpec((rows, d), lambda l, j: (0, 0)),
                  pl.BlockSpec((1, d, tn), lambda l, j: (l, 0, j)),
                  pl.BlockSpec((1, 1, tn), lambda l, j: (l, 0, j))],
        out_specs=pl.BlockSpec((1, rows, tn), lambda l, j: (l, 0, j)),
        out_shape=jax.ShapeDtypeStruct((depth, rows, d6), F32),
        compiler_params=_params(("arbitrary", "arbitrary")),
        name="modulation",
    )(c_all, w_mod, b_mod.reshape(depth, 1, d6))


def _mod_spec(mod, tm, tiles_per_group):
    _, r, d = mod.shape
    return pl.BlockSpec((1, r, d), lambda i, *_: (i // tiles_per_group, 0, 0))


TRANSPOSED_SECTIONS = (0, 2, 3, 5)


def _inproj_kernel(x_ref, g_ref, shift_ref, scale_ref, wqkv_ref, wf_ref, wg_ref, bf_ref, *refs):
    qkv_ref, qvt_ref, ksb_ref, vsb_ref, kfx_ref, vfx_ref, lf_ref, sg_ref = refs[-8:]
    tm = x_ref.shape[0]
    h = _rms(x_ref[...], g_ref[...]) * (1.0 + scale_ref[0]) + shift_ref[0]
    hb = h.astype(BF16)
    f32_outs = {1: ksb_ref, 2: vsb_ref, 4: kfx_ref, 5: vfx_ref}
    for c in range(6):
        cols = slice(c * WIDTH, (c + 1) * WIDTH)
        acc = _dot(hb, wqkv_ref[:, cols])
        if c in f32_outs:
            f32_outs[c][0] = acc.reshape(tm, N_HEADS, HEAD_DIM)
        if c in (0, 3):
            acc = acc * (HEAD_DIM ** -0.5 * (LOG2_E if c == 3 else 1.0))
        qkv_ref[:, cols] = acc.astype(BF16)
        if c in TRANSPOSED_SECTIONS:
            t = TRANSPOSED_SECTIONS.index(c)
            qvt_ref[t * WIDTH:(t + 1) * WIDTH, :] = acc.T.astype(BF16)
    f = _dot(hb, wf_ref[...])[:, :N_HEADS] + bf_ref[...]
    lf_ref[0] = -_softplus(-f)
    for c in range(wg_ref.shape[1] // WIDTH):
        cols = slice(c * WIDTH, (c + 1) * WIDTH)
        sg_ref[:, cols] = _sigmoid(_dot(hb, wg_ref[:, cols])).astype(BF16)


def _in_projection(x, g_pre, shift, scale, wqkv, wf, wg, b_forget, tm, layer, depth, stacked):
    n, d = x.shape
    tiles_per_group = (n // tm) // shift.shape[0]
    const = lambda i: (0, 0)
    row = lambda i: (i, 0)
    heads_spec = pl.BlockSpec((1, tm, N_HEADS, HEAD_DIM), lambda i: (layer, i, 0, 0))
    heads_shape = jax.ShapeDtypeStruct((depth, n, N_HEADS, HEAD_DIM), F32)
    n_in = 8
    stacked = () if stacked is None else tuple(stacked)
    return pl.pallas_call(
        _inproj_kernel,
        grid=(n // tm,),
        in_specs=[pl.BlockSpec((tm, d), row),
                  pl.BlockSpec((1, d), const),
                  _mod_spec(shift, tm, tiles_per_group),
                  _mod_spec(scale, tm, tiles_per_group),
                  pl.BlockSpec(wqkv.shape, const),
                  pl.BlockSpec(wf.shape, const),
                  pl.BlockSpec(wg.shape, const),
                  pl.BlockSpec((1, N_HEADS), const)] + [pl.BlockSpec(memory_space=pl.ANY)] * len(stacked),
        out_specs=[pl.BlockSpec((tm, 6 * WIDTH), row), pl.BlockSpec((len(TRANSPOSED_SECTIONS) * WIDTH, tm), lambda i: (0, i))]
                  + [heads_spec] * 4
                  + [pl.BlockSpec((1, tm, N_HEADS), lambda i: (layer, i, 0)), pl.BlockSpec((tm, wg.shape[1]), row)],
        out_shape=[jax.ShapeDtypeStruct((n, 6 * WIDTH), BF16),
                   jax.ShapeDtypeStruct((len(TRANSPOSED_SECTIONS) * WIDTH, n), BF16)] + [heads_shape] * 4
                  + [jax.ShapeDtypeStruct((depth, n, N_HEADS), F32), jax.ShapeDtypeStruct((n, wg.shape[1]), BF16)],
        input_output_aliases={n_in + k: 2 + k for k in range(len(stacked))},
        compiler_params=_params(("arbitrary",)),
        name="in_projection",
    )(x, g_pre, shift, scale, wqkv, wf, wg, b_forget, *stacked)


CUMSUM_CHUNK = 256


FORGET_SPLIT = 3


def _cumsum_kernel(x_ref, o_ref, kcol_ref):
    t = x_ref.shape[2]
    r = lax.broadcasted_iota(jnp.int32, (CUMSUM_CHUNK, CUMSUM_CHUNK), 0)
    c = lax.broadcasted_iota(jnp.int32, (CUMSUM_CHUNK, CUMSUM_CHUNK), 1)
    upper = jnp.where(r <= c, 1.0, 0.0).astype(F32)
    row = lax.broadcasted_iota(jnp.int32, (LANES, LANES), 0)
    lane = lax.broadcasted_iota(jnp.int32, (LANES, LANES), 1)
    head, term = row % N_HEADS, row // N_HEADS
    placed = jnp.logical_and(row < FORGET_SPLIT * N_HEADS, lane == (head % 2) * FORGET_SPLIT + term)
    pad_rows = jnp.zeros((LANES - FORGET_SPLIT * N_HEADS, CUMSUM_CHUNK), F32)

    def step(i, carry):
        start = pl.multiple_of(i * CUMSUM_CHUNK, CUMSUM_CHUNK)
        seg = x_ref[0, :, pl.ds(start, CUMSUM_CHUNK)]
        cs = jnp.dot(seg, upper, precision=lax.Precision.HIGHEST, preferred_element_type=F32) + carry
        f_log2 = cs * LOG2_E
        o_ref[0, :, pl.ds(start, CUMSUM_CHUNK)] = f_log2
        rest = -f_log2
        terms = []
        for _ in range(FORGET_SPLIT):
            part = pltpu.bitcast(pltpu.bitcast(rest, jnp.uint32) & jnp.uint32(0xFFFF0000), F32)
            terms.append(part)
            rest = rest - part
        terms_t = jnp.concatenate(terms + [pad_rows], axis=0).T.astype(BF16)
        for p in range(HEAD_PAIRS):
            select = jnp.where(jnp.logical_and(placed, head // 2 == p), 1.0, 0.0).astype(BF16)
            kcol_ref[0, p, pl.ds(start, CUMSUM_CHUNK), :] = _dot(terms_t, select).astype(BF16)
        return cs[:, CUMSUM_CHUNK - 1:CUMSUM_CHUNK]

    lax.fori_loop(0, t // CUMSUM_CHUNK, step, jnp.zeros((x_ref.shape[1], 1), F32))


def _forget_cumsum(x):
    b, r, t = x.shape
    return pl.pallas_call(
        _cumsum_kernel,
        grid=(b,),
        in_specs=[pl.BlockSpec((1, r, t), lambda i: (i, 0, 0))],
        out_specs=[pl.BlockSpec((1, r, t), lambda i: (i, 0, 0)),
                   pl.BlockSpec((1, HEAD_PAIRS, t, LANES), lambda i: (i, 0, 0, 0))],
        out_shape=[jax.ShapeDtypeStruct((b, r, t), F32), jax.ShapeDtypeStruct((b, HEAD_PAIRS, t, LANES), BF16)],
        compiler_params=_params(("arbitrary",)),
        name="forget_cumsum",
    )(x)


def _pair_layouts(cum_t, t_query_start, t_query):
    b, _, tp = cum_t.shape
    f_keys = cum_t.reshape(b, HEAD_PAIRS, 2, tp)
    f_query = jnp.swapaxes(f_keys[:, :, :, t_query_start:t_query_start + t_query], 2, 3)
    return f_keys, f_query


def _head_masks():
    lane = lax.broadcasted_iota(jnp.int32, (1, LANES), 1)
    return lane < HEAD_DIM


def _split_heads(q2, first_head):
    zero = jnp.zeros_like(q2)
    return jnp.where(first_head, q2, zero), jnp.where(first_head, zero, q2)


def _strict_lower_neg(n):
    r = lax.broadcasted_iota(jnp.int32, (n, n), 0)
    c = lax.broadcasted_iota(jnp.int32, (n, n), 1)
    return jnp.where(r > c, -1.0, 0.0).astype(BF16)


def _sb_block(qh, k2, v2, neg_tri, visible, acc_ref, r_ref, hd):
    z = _dot_nt(qh, k2)
    sp = _softplus(z)
    if visible is not None:
        sp = jnp.where(visible, sp, 0.0)
    later = _dot(sp.astype(BF16), neg_tri)
    p = jnp.exp(z - sp + later)
    if visible is not None:
        p = jnp.where(visible, p, 0.0)
    r = r_ref[hd]
    acc_ref[hd] += jnp.exp(r) * _dot(p.astype(BF16), v2)
    r_ref[hd] = r + later[:, 0:1] - sp[:, 0:1]


def _fox_block(qh, k2, v2_ones, fq, fk, allowed, acc_ref, m_ref, hd):
    s = _dot_nt(qh, k2) + fq - fk
    if allowed is not None:
        s = jnp.where(allowed, s, -jnp.inf)
    m_old = m_ref[hd]
    m_new = jnp.maximum(m_old, jnp.max(s, axis=-1, keepdims=True))
    p = jnp.exp2(s - m_new)
    acc_ref[hd] = jnp.exp2(m_old - m_new) * acc_ref[hd] + _dot(p.astype(BF16), v2_ones)
    m_ref[hd] = m_new


def _fox_finish(acc_ref, first_head):
    a, b = acc_ref[0], acc_ref[1]
    num = jnp.where(first_head, a, b)
    den = jnp.where(first_head, pltpu.roll(a, HEAD_DIM, 1), pltpu.roll(b, HEAD_DIM, 1))
    return num / den


def _with_ones(v2, first_head):
    one = jnp.ones_like(v2)
    return jnp.where(first_head, v2, one), jnp.where(first_head, one, v2)


def _sb_prompt_kernel(qt_ref, k_ref, vt_ref, o_ref, w_ref, acc_ref, r_ref, tri_ref, *, tq):
    i = pl.program_id(2)
    sub = lax.broadcasted_iota(jnp.int32, (LANES, tq), 0)
    first_rows = sub < HEAD_DIM
    qt = qt_ref[...]
    zero = jnp.zeros_like(qt)
    w_ref[0] = jnp.where(first_rows, qt, zero)
    w_ref[1] = jnp.where(first_rows, zero, qt)
    acc_ref[...] = jnp.zeros_like(acc_ref)
    r_ref[...] = jnp.zeros_like(r_ref)

    @pl.when(jnp.logical_and(jnp.logical_and(pl.program_id(0) == 0, pl.program_id(1) == 0), i == 0))
    def _():
        s_idx = lax.broadcasted_iota(jnp.int32, tri_ref.shape, 0)
        j_idx = lax.broadcasted_iota(jnp.int32, tri_ref.shape, 1)
        tri_ref[...] = jnp.where(j_idx > s_idx, -1.0, 0.0).astype(BF16)

    def block(start, size, vis):
        keys = k_ref[0, pl.ds(start, size), :]
        vt = vt_ref[:, pl.ds(start, size)]
        neg_tri = tri_ref[:size, :size]
        for hd in range(2):
            z = _dot(keys, w_ref[hd])
            sp = _softplus(z)
            if vis is not None:
                sp = jnp.where(vis, sp, 0.0)
            later = _dot(neg_tri, sp.astype(BF16))
            p = jnp.exp(z - sp + later)
            if vis is not None:
                p = jnp.where(vis, p, 0.0)
            r = r_ref[hd]
            acc_ref[hd] += jnp.exp(r) * _dot(vt, p.astype(BF16))
            r_ref[hd] = r + later[0:1, :] - sp[0:1, :]

    def live():
        return jnp.max(jnp.maximum(r_ref[0], r_ref[1]))

    first = jnp.maximum(i - 1, 0)
    key_idx = lax.broadcasted_iota(jnp.int32, (2 * tq, tq), 0) + first * tq
    query_idx = lax.broadcasted_iota(jnp.int32, (2 * tq, tq), 1) + i * tq
    block(pl.multiple_of(first * tq, tq), 2 * tq, key_idx < query_idx)

    def cond(carry):
        j, r_max = carry
        return jnp.logical_and(j >= 0, r_max > SB_DEAD_LOG)

    def body(carry):
        j, _ = carry
        block(pl.multiple_of(j * tq, tq), tq, None)
        return j - 1, live()

    lax.while_loop(cond, body, (i - 2, live()))
    o_ref[0] = jnp.where(first_rows, acc_ref[0], acc_ref[1]).T.astype(o_ref.dtype)


def _sb_prompt(qkv, qv_t, tq):
    b, t, _ = qkv.shape
    nq = t // tq
    kernel = functools.partial(_sb_prompt_kernel, tq=tq)
    return pl.pallas_call(
        kernel,
        grid=(b, HEAD_PAIRS, nq),
        in_specs=[pl.BlockSpec((LANES, tq), lambda b_, p, i: (p, b_ * nq + i)),
                  pl.BlockSpec((1, t, LANES), lambda b_, p, i: (b_, 0, HEAD_PAIRS + p)),
                  pl.BlockSpec((LANES, t), lambda b_, p, i: (HEAD_PAIRS + p, b_))],
        out_specs=pl.BlockSpec((1, tq, LANES), lambda b_, p, i: (b_, i, p)),
        out_shape=jax.ShapeDtypeStruct((b, t, WIDTH), BF16),
        scratch_shapes=[pltpu.VMEM((2, LANES, tq), BF16), pltpu.VMEM((2, LANES, tq), F32),
                        pltpu.VMEM((2, 1, tq), F32), pltpu.VMEM((2 * tq, 2 * tq), BF16)],
        compiler_params=_params(("arbitrary", "arbitrary", "arbitrary")),
        name="sb_attention_prompt",
    )(qv_t, qkv, qv_t)


FOX_DEAD_LOG2 = -150.0


def _fox_prompt_kernel(kmax_ref, fend_ref, qt_ref, k_ref, kf_ref, vt_ref, fq_ref, o_ref,
                       w_ref, acc_ref, m_ref, zb_ref, *, tq):
    bi = pl.program_id(0)
    pi = pl.program_id(1)
    i = pl.program_id(2)
    sub = lax.broadcasted_iota(jnp.int32, (LANES, tq), 0)
    first_rows = sub < HEAD_DIM
    qt = qt_ref[...]
    zero = jnp.zeros_like(qt)
    for hd in range(2):
        lo = hd * FORGET_SPLIT
        select = jnp.where(jnp.logical_and(sub >= lo, sub < lo + FORGET_SPLIT), 1.0, 0.0).astype(BF16)
        q_head = jnp.where(first_rows, qt, zero) if hd == 0 else jnp.where(first_rows, zero, qt)
        w_ref[hd, :LANES, :] = q_head
        w_ref[hd, LANES:, :] = select
        q_f32 = q_head.astype(F32)
        q_norm = jnp.sqrt(jnp.sum(q_f32 * q_f32, axis=0, keepdims=True))
        zb_ref[hd] = q_norm * (kmax_ref[bi, 2 * pi + hd] * 1.01) + fq_ref[0, 0, hd:hd + 1, :]
    acc_ref[...] = jnp.zeros_like(acc_ref)
    m_ref[...] = jnp.full_like(m_ref, -jnp.inf)
    key_idx = lax.broadcasted_iota(jnp.int32, (tq, tq), 0)
    query_idx = lax.broadcasted_iota(jnp.int32, (tq, tq), 1)
    allowed = key_idx <= query_idx

    def block(j, mask):
        start = pl.multiple_of(j * tq, tq)
        keys = jnp.concatenate([k_ref[0, pl.ds(start, tq), :], kf_ref[0, 0, pl.ds(start, tq), :]], axis=1)
        vt = vt_ref[:, pl.ds(start, tq)]
        one = jnp.ones_like(vt)
        first_v = lax.broadcasted_iota(jnp.int32, vt.shape, 0) < HEAD_DIM
        vts = (jnp.where(first_v, vt, one), jnp.where(first_v, one, vt))
        for hd in range(2):
            u = _dot(keys, w_ref[hd])
            if mask is not None:
                u = jnp.where(mask, u, -jnp.inf)
            fq = fq_ref[0, 0, hd:hd + 1, :]
            m_old = m_ref[hd]
            m_new = jnp.maximum(m_old, jnp.max(u, axis=0, keepdims=True) + fq)
            p = jnp.exp2(u + (fq - m_new))
            acc_ref[hd] = jnp.exp2(m_old - m_new) * acc_ref[hd] + _dot(vts[hd], p.astype(BF16))
            m_ref[hd] = m_new

    def alive(j):
        jc = jnp.maximum(j, 0)
        reach = [jnp.max(zb_ref[hd] - m_ref[hd]) - fend_ref[bi, pi, hd, jc] for hd in range(2)]
        return jnp.maximum(reach[0], reach[1]) > FOX_DEAD_LOG2

    block(i, allowed)

    def cond(carry):
        j, go = carry
        return jnp.logical_and(j >= 0, go)

    def body(carry):
        j, _ = carry
        block(j, None)
        return j - 1, alive(j - 1)

    lax.while_loop(cond, body, (i - 1, alive(i - 1)))
    a, b = acc_ref[0], acc_ref[1]
    out_t = jnp.where(first_rows, a / a[HEAD_DIM:HEAD_DIM + 1, :], b / b[0:1, :])
    o_ref[0] = out_t.T.astype(o_ref.dtype)


def _fox_prompt(qkv, qv_t, cum_t, key_cols, tq):
    b, t, _ = qkv.shape
    nq = t // tq
    f_rows = cum_t.reshape(b, HEAD_PAIRS, 2, t)
    f_block_end = f_rows[:, :, :, tq - 1::tq]
    k_fx = qkv[:, :, 4 * WIDTH:5 * WIDTH].astype(F32).reshape(b, t, N_HEADS, HEAD_DIM)
    k_max = jnp.sqrt(jnp.max(jnp.sum(k_fx * k_fx, axis=-1), axis=1))
    smem = pl.BlockSpec(memory_space=pltpu.SMEM)
    kernel = functools.partial(_fox_prompt_kernel, tq=tq)
    return pl.pallas_call(
        kernel,
        grid=(b, HEAD_PAIRS, nq),
        in_specs=[smem, smem,
                  pl.BlockSpec((LANES, tq), lambda b_, p, i: (2 * HEAD_PAIRS + p, b_ * nq + i)),
                  pl.BlockSpec((1, t, LANES), lambda b_, p, i: (b_, 0, 4 * HEAD_PAIRS + p)),
                  pl.BlockSpec((1, 1, t, LANES), lambda b_, p, i: (b_, p, 0, 0)),
                  pl.BlockSpec((LANES, t), lambda b_, p, i: (3 * HEAD_PAIRS + p, b_)),
                  pl.BlockSpec((1, 1, 2, tq), lambda b_, p, i: (b_, p, 0, i))],
        out_specs=pl.BlockSpec((1, tq, LANES), lambda b_, p, i: (b_, i, p)),
        out_shape=jax.ShapeDtypeStruct((b, t, WIDTH), BF16),
        scratch_shapes=[pltpu.VMEM((2, 2 * LANES, tq), BF16), pltpu.VMEM((2, LANES, tq), F32),
                        pltpu.VMEM((2, 1, tq), F32), pltpu.VMEM((2, 1, tq), F32)],
        compiler_params=_params(("arbitrary", "arbitrary", "arbitrary")),
        name="fox_attention_prompt",
    )(k_max, f_block_end, qv_t, qkv, key_cols, qv_t, f_rows)


def _sample_attn_kernel(qsb_ref, ksb_ref, vsb_ref, qfx_ref, kfx_ref, vfx_ref,
                        csk_ref, csv_ref, cfk_ref, cfv_ref, fq_ref, fk_ref,
                        osb_ref, ofx_ref, acc_ref, st_ref, *, tk):
    t_new = qsb_ref.shape[1]
    n_past = csk_ref.shape[1]
    n_blocks = n_past // tk
    first_head = _head_masks()
    row = lax.broadcasted_iota(jnp.int32, (t_new, t_new), 0)
    col = lax.broadcasted_iota(jnp.int32, (t_new, t_new), 1)

    qs = _split_heads(qsb_ref[0], first_head)
    acc_ref[...] = jnp.zeros_like(acc_ref)
    st_ref[...] = jnp.zeros_like(st_ref)
    tri_new = _strict_lower_neg(t_new)
    tri_past = _strict_lower_neg(tk)
    for hd in range(2):
        _sb_block(qs[hd], ksb_ref[0], vsb_ref[0], tri_new, col < row, acc_ref, st_ref, hd)

    def live():
        return jnp.max(jnp.maximum(st_ref[0], st_ref[1]))

    def sb_cond(carry):
        j, r_max = carry
        return jnp.logical_and(j >= 0, r_max > SB_DEAD_LOG)

    def sb_body(carry):
        j, _ = carry
        start = pl.multiple_of(j * tk, tk)
        k2 = csk_ref[0, pl.ds(start, tk), :].astype(BF16)
        v2 = csv_ref[0, pl.ds(start, tk), :].astype(BF16)
        for hd in range(2):
            _sb_block(qs[hd], k2, v2, tri_past, None, acc_ref, st_ref, hd)
        return j - 1, live()

    lax.while_loop(sb_cond, sb_body, (n_blocks - 1, live()))
    osb_ref[0] = jnp.where(first_head, acc_ref[0], acc_ref[1]).astype(osb_ref.dtype)

    qs = _split_heads(qfx_ref[0], first_head)
    acc_ref[...] = jnp.zeros_like(acc_ref)
    st_ref[...] = jnp.full_like(st_ref, -jnp.inf)
    fq = fq_ref[0, 0]
    vs = _with_ones(vfx_ref[0], first_head)
    for hd in range(2):
        fk = fk_ref[0, 0, hd:hd + 1, n_past:n_past + t_new]
        _fox_block(qs[hd], kfx_ref[0], vs[hd], fq[:, hd:hd + 1], fk, col <= row, acc_ref, st_ref, hd)

    k2 = cfk_ref[0].astype(BF16)
    vs_ = _with_ones(cfv_ref[0].astype(BF16), first_head)
    for hd in range(2):
        fk = fk_ref[0, 0, hd:hd + 1, 0:n_past]
        _fox_block(qs[hd], k2, vs_[hd], fq[:, hd:hd + 1], fk, None, acc_ref, st_ref, hd)
    ofx_ref[0] = _fox_finish(acc_ref, first_head).astype(ofx_ref.dtype)


def _sample_attention(qkv, cache_sb_k, cache_sb_v, cache_fox_k, cache_fox_v, f_query, f_keys, tk):
    b, t_new, _ = qkv.shape
    n_past = cache_sb_k.shape[1]
    tp = f_keys.shape[3]
    new = lambda c: pl.BlockSpec((1, t_new, LANES), lambda b_, p: (b_, 0, c * HEAD_PAIRS + p))
    past = pl.BlockSpec((1, n_past, LANES), lambda b_, p: (b_, 0, p))
    out = pl.BlockSpec((1, t_new, LANES), lambda b_, p: (b_, 0, p))
    kernel = functools.partial(_sample_attn_kernel, tk=tk)
    return pl.pallas_call(
        kernel,
        grid=(b, HEAD_PAIRS),
        in_specs=[new(0), new(1), new(2), new(3), new(4), new(5), past, past, past, past,
                  pl.BlockSpec((1, 1, t_new, 2), lambda b_, p: (b_, p, 0, 0)),
                  pl.BlockSpec((1, 1, 2, tp), lambda b_, p: (b_, p, 0, 0))],
        out_specs=[out, out],
        out_shape=[jax.ShapeDtypeStruct((b, t_new, WIDTH), BF16)] * 2,
        scratch_shapes=[pltpu.VMEM((2, t_new, LANES), F32), pltpu.VMEM((2, t_new, 1), F32)],
        compiler_params=_params(("arbitrary", "arbitrary")),
        name="attention_sample",
    )(qkv, qkv, qkv, qkv, qkv, qkv, cache_sb_k, cache_sb_v, cache_fox_k, cache_fox_v, f_query, f_keys)


def _mix_out_kernel(x_ref, osb_ref, ofx_ref, sg_ref, gate_ref, g_ref, wsb_ref, wfx_ref, wo_ref, o_ref):
    d = x_ref.shape[1]
    merged = (sg_ref[:, :d].astype(F32) * _dot(osb_ref[...], wsb_ref[...])
              + sg_ref[:, d:].astype(F32) * _dot(ofx_ref[...], wfx_ref[...]))
    y = _dot(merged.astype(BF16), wo_ref[...])
    o_ref[...] = x_ref[...] + gate_ref[0] * _rms(y, g_ref[...])


def _mix_out(x, o_sb, o_fx, sg, gate, g_post, w_sb_out, w_fox_out, w_out, tm):
    n, d = x.shape
    tiles_per_group = (n // tm) // gate.shape[0]
    const = lambda i: (0, 0)
    row = lambda i: (i, 0)
    return pl.pallas_call(
        _mix_out_kernel,
        grid=(n // tm,),
        in_specs=[pl.BlockSpec((tm, d), row), pl.BlockSpec((tm, WIDTH), row), pl.BlockSpec((tm, WIDTH), row),
                  pl.BlockSpec((tm, 2 * d), row), _mod_spec(gate, tm, tiles_per_group),
                  pl.BlockSpec((1, d), const), pl.BlockSpec(w_sb_out.shape, const),
                  pl.BlockSpec(w_fox_out.shape, const), pl.BlockSpec(w_out.shape, const)],
        out_specs=pl.BlockSpec((tm, d), row),
        out_shape=jax.ShapeDtypeStruct((n, d), F32),
        compiler_params=_params(("arbitrary",)),
        name="mix_out",
    )(x, o_sb, o_fx, sg, gate, g_post, w_sb_out, w_fox_out, w_out)


def _swiglu_chunk(hb, wa, wb, wd):
    a = _dot(hb, wa)
    b = _dot(hb, wb)
    return a * _sigmoid(a) * b, wd


def _ffn_kernel(x_ref, gpre_ref, shift_ref, scale_ref, gate_ref, gpost_ref, wa_ref, wb_ref, wd_ref,
                o_ref, h_ref, acc_ref):
    f = pl.program_id(1)

    @pl.when(f == 0)
    def _():
        h = _rms(x_ref[...], gpre_ref[...]) * (1.0 + scale_ref[0]) + shift_ref[0]
        h_ref[...] = h.astype(BF16)
        acc_ref[...] = jnp.zeros_like(acc_ref)

    hb = h_ref[...]
    a = _dot(hb, wa_ref[...])
    b = _dot(hb, wb_ref[...])
    act = a * _sigmoid(a) * b
    acc_ref[...] += _dot(act.astype(BF16), wd_ref[...])

    @pl.when(f == pl.num_programs(1) - 1)
    def _():
        o_ref[...] = x_ref[...] + gate_ref[0] * _rms(acc_ref[...], gpost_ref[...])


def _ffn(x, g_pre, shift, scale, gate, g_post, w_gate_up, w_down, tm, tf):
    n, d = x.shape
    d_ff = w_down.shape[0]
    nf = d_ff // tf
    tiles_per_group = (n // tm) // shift.shape[0]
    const = lambda i, f: (0, 0)
    row = lambda i, f: (i, 0)
    return pl.pallas_call(
        _ffn_kernel,
        grid=(n // tm, nf),
        in_specs=[pl.BlockSpec((tm, d), row), pl.BlockSpec((1, d), const),
                  _mod_spec(shift, tm, tiles_per_group), _mod_spec(scale, tm, tiles_per_group),
                  _mod_spec(gate, tm, tiles_per_group), pl.BlockSpec((1, d), const),
                  pl.BlockSpec((d, tf), lambda i, f: (0, f)),
                  pl.BlockSpec((d, tf), lambda i, f: (0, nf + f)),
                  pl.BlockSpec((tf, d), lambda i, f: (f, 0))],
        out_specs=pl.BlockSpec((tm, d), row),
        out_shape=jax.ShapeDtypeStruct((n, d), F32),
        scratch_shapes=[pltpu.VMEM((tm, d), BF16), pltpu.VMEM((tm, d), F32)],
        compiler_params=_params(("arbitrary", "arbitrary")),
        name="ffn_dense",
    )(x, g_pre, shift, scale, gate, g_post, w_gate_up, w_gate_up, w_down)


def _route_kernel(x_ref, gpre_ref, shift_ref, scale_ref, wr_ref, h_ref, idx_ref, w_ref, *, n_experts):
    h = _rms(x_ref[...], gpre_ref[...]) * (1.0 + scale_ref[0]) + shift_ref[0]
    h_ref[...] = h
    logits = jnp.dot(h, wr_ref[...], precision=lax.Precision.HIGHEST, preferred_element_type=F32)[:, :n_experts]
    idx = lax.broadcasted_iota(jnp.int32, logits.shape, 1)
    m1 = jnp.max(logits, axis=-1, keepdims=True)
    i1 = jnp.min(jnp.where(logits == m1, idx, n_experts), axis=-1, keepdims=True)
    rest = jnp.where(idx == i1, -jnp.inf, logits)
    m2 = jnp.max(rest, axis=-1, keepdims=True)
    i2 = jnp.min(jnp.where(rest == m2, idx, n_experts), axis=-1, keepdims=True)
    e2 = jnp.exp(m2 - m1)
    slot = lax.broadcasted_iota(jnp.int32, idx_ref.shape, 1)
    idx_ref[...] = jnp.where(slot == 0, i1, i2)
    w_ref[...] = jnp.where(slot == 0, 1.0 / (1.0 + e2), e2 / (1.0 + e2))


def _route(x, g_pre, shift, scale, w_router, n_experts, tm):
    n, d = x.shape
    tiles_per_group = (n // tm) // shift.shape[0]
    const = lambda i: (0, 0)
    row = lambda i: (i, 0)
    return pl.pallas_call(
        functools.partial(_route_kernel, n_experts=n_experts),
        grid=(n // tm,),
        in_specs=[pl.BlockSpec((tm, d), row), pl.BlockSpec((1, d), const),
                  _mod_spec(shift, tm, tiles_per_group), _mod_spec(scale, tm, tiles_per_group),
                  pl.BlockSpec(w_router.shape, const)],
        out_specs=[pl.BlockSpec((tm, d), row), pl.BlockSpec((tm, TOP_K), row), pl.BlockSpec((tm, TOP_K), row)],
        out_shape=[jax.ShapeDtypeStruct((n, d), F32), jax.ShapeDtypeStruct((n, TOP_K), jnp.int32),
                   jax.ShapeDtypeStruct((n, TOP_K), F32)],
        compiler_params=_params(("arbitrary",)),
        name="moe_route",
    )(x, g_pre, shift, scale, w_router)


def _dispatch_tables(top_idx, n_experts, tm):
    n = top_idx.shape[0]
    n_assign = n * TOP_K
    expert = top_idx.reshape(n_assign)
    onehot = (expert[:, None] == jnp.arange(n_experts, dtype=jnp.int32)[None, :]).astype(jnp.int32)
    rank = jnp.sum((jnp.cumsum(onehot, axis=0) - onehot) * onehot, axis=1)
    counts = jnp.sum(onehot, axis=0)
    padded = (counts + tm - 1) // tm * tm
    group_end = jnp.cumsum(padded)
    pos = (group_end - padded)[expert] + rank
    p_rows = n_assign + n_experts * tm
    src_token = jnp.zeros((p_rows,), jnp.int32).at[pos].set(jnp.arange(n_assign, dtype=jnp.int32) // TOP_K)
    tile_start = jnp.arange(p_rows // tm, dtype=jnp.int32) * tm
    tile_expert = jnp.minimum(jnp.sum(tile_start[:, None] >= group_end[None, :], axis=1), n_experts - 1)
    n_used = (group_end[-1] // tm).reshape(1)
    return src_token, tile_expert.astype(jnp.int32), n_used.astype(jnp.int32), pos.astype(jnp.int32)


def _gather_rows(src_hbm, dst, sem, index_of_row, n_rows):
    def issue(r, carry):
        pltpu.make_async_copy(src_hbm.at[pl.ds(index_of_row(r), 1), :], dst.at[pl.ds(r, 1), :], sem).start()
        return carry

    lax.fori_loop(0, n_rows, issue, 0, unroll=8)


def _wait_rows(src_hbm, dst, sem):
    pltpu.make_async_copy(src_hbm.at[pl.ds(0, dst.shape[0]), :], dst, sem).wait()


def _experts_kernel(src_ref, expert_ref, used_ref, h_hbm, wa_ref, wb_ref, wd_ref, o_ref,
                    rows_ref, xb_ref, acc_ref, sem, *, tm):
    i = pl.program_id(0)
    f = pl.program_id(1)
    n_used = used_ref[0]
    slot = lax.rem(i, 2)
    active = i < n_used

    def gather(tile, into):
        _gather_rows(h_hbm, rows_ref.at[into], sem.at[into], lambda r: src_ref[tile * tm + r], tm)

    @pl.when(jnp.logical_and(f == 0, active))
    def _():
        @pl.when(i == 0)
        def _():
            gather(0, 0)

        _wait_rows(h_hbm, rows_ref.at[slot], sem.at[slot])

        @pl.when(i + 1 < n_used)
        def _():
            gather(i + 1, 1 - slot)

        xb_ref[...] = rows_ref[slot].astype(BF16)
        acc_ref[...] = jnp.zeros_like(acc_ref)

    @pl.when(active)
    def _():
        xb = xb_ref[...]
        a = _dot(xb, wa_ref[0])
        b = _dot(xb, wb_ref[0])
        act = a * _sigmoid(a) * b
        acc_ref[...] += _dot(act.astype(BF16), wd_ref[0])

    last = f == pl.num_programs(1) - 1

    @pl.when(jnp.logical_and(last, active))
    def _():
        o_ref[...] = acc_ref[...]

    @pl.when(jnp.logical_and(last, jnp.logical_not(active)))
    def _():
        o_ref[...] = jnp.zeros_like(o_ref)


def _experts(h, src_token, tile_expert, n_used, w_gate_up, w_down, tm, tf):
    p_rows = src_token.shape[0]
    d = h.shape[1]
    nf = w_down.shape[1] // tf
    ff = lambda i, f, used: jnp.where(i < used[0], f, nf - 1)
    grid_spec = pltpu.PrefetchScalarGridSpec(
        num_scalar_prefetch=3,
        grid=(p_rows // tm, nf),
        in_specs=[pl.BlockSpec(memory_space=pl.ANY),
                  pl.BlockSpec((1, d, tf), lambda i, f, src, ex, used: (ex[i], 0, ff(i, f, used))),
                  pl.BlockSpec((1, d, tf), lambda i, f, src, ex, used: (ex[i], 0, nf + ff(i, f, used))),
                  pl.BlockSpec((1, tf, d), lambda i, f, src, ex, used: (ex[i], ff(i, f, used), 0))],
        out_specs=pl.BlockSpec((tm, d), lambda i, f, src, ex, used: (i, 0)),
        scratch_shapes=[pltpu.VMEM((2, tm, d), F32), pltpu.VMEM((tm, d), BF16), pltpu.VMEM((tm, d), F32),
                        pltpu.SemaphoreType.DMA((2,))])
    return pl.pallas_call(
        functools.partial(_experts_kernel, tm=tm),
        grid_spec=grid_spec,
        out_shape=jax.ShapeDtypeStruct((p_rows, d), F32),
        compiler_params=_params(("arbitrary", "arbitrary")),
        name="moe_experts",
    )(src_token, tile_expert, n_used, h, w_gate_up, w_gate_up, w_down)


def _combine_kernel(pos_ref, y_hbm, x_ref, w_ref, gate_ref, gpost_ref, o_ref, rows_ref, sem, *, tm):
    i = pl.program_id(0)
    slot = lax.rem(i, 2)

    def gather(tile, into):
        for k in range(TOP_K):
            _gather_rows(y_hbm, rows_ref.at[into, k], sem.at[into],
                         lambda r, k=k: pos_ref[(tile * tm + r) * TOP_K + k], tm)

    @pl.when(i == 0)
    def _():
        gather(0, 0)

    for k in range(TOP_K):
        _wait_rows(y_hbm, rows_ref.at[slot, k], sem.at[slot])

    @pl.when(i + 1 < pl.num_programs(0))
    def _():
        gather(i + 1, 1 - slot)

    w = w_ref[...]
    mixed = w[:, 0:1] * rows_ref[slot, 0] + w[:, 1:2] * rows_ref[slot, 1]
    o_ref[...] = x_ref[...] + gate_ref[0] * _rms(mixed, gpost_ref[...])


def _combine(y, pos, top_w, x, gate, g_post, tm):
    n, d = x.shape
    tiles_per_group = (n // tm) // gate.shape[0]
    grid_spec = pltpu.PrefetchScalarGridSpec(
        num_scalar_prefetch=1,
        grid=(n // tm,),
        in_specs=[pl.BlockSpec(memory_space=pl.ANY),
                  pl.BlockSpec((tm, d), lambda i, pos_: (i, 0)),
                  pl.BlockSpec((tm, TOP_K), lambda i, pos_: (i, 0)),
                  _mod_spec(gate, tm, tiles_per_group),
                  pl.BlockSpec((1, d), lambda i, pos_: (0, 0))],
        out_specs=pl.BlockSpec((tm, d), lambda i, pos_: (i, 0)),
        scratch_shapes=[pltpu.VMEM((2, TOP_K, tm, d), F32), pltpu.SemaphoreType.DMA((2,))])
    return pl.pallas_call(
        functools.partial(_combine_kernel, tm=tm),
        grid_spec=grid_spec,
        out_shape=jax.ShapeDtypeStruct((n, d), F32),
        compiler_params=_params(("arbitrary",)),
        name="moe_combine",
    )(pos, y, x, top_w, gate, g_post)


def _moe(x, g_pre, shift, scale, gate, g_post, w_router, w_gate_up, w_down, n_experts, tm_route, tm, tf, tm_out):
    h, top_idx, top_w = _route(x, g_pre, shift, scale, w_router, n_experts, tm_route)
    src_token, tile_expert, n_used, pos = _dispatch_tables(top_idx, n_experts, tm)
    y = _experts(h, src_token, tile_expert, n_used, w_gate_up, w_down, tm, tf)
    return _combine(y, pos, top_w, x, gate, g_post, tm_out)


def _row_tile(n, want):
    return want if n % want == 0 else n


def kernel(x_prompt, x_sample, c_prompt, c_sample, cache_sb_k, cache_sb_v, cache_fox_k, cache_fox_v, cache_fox_logf, w_mod, b_mod, g_pre_mix, g_post_mix, g_pre_ffn, g_post_ffn, w_in, b_forget, w_sb_out, w_fox_out, w_out, w_ffn_gate_up, w_ffn_down, w_router, w_moe_gate_up, w_moe_down):
    bsz, seq, d = x_prompt.shape
    dec_b, dec_t, _ = x_sample.shape
    depth = w_mod.shape[0]
    n_past = cache_sb_k.shape[2]
    n_experts = w_router.shape[2]
    n_p, n_s = bsz * seq, dec_b * dec_t
    tq = _row_tile(seq, 256)
    past_chunk = _row_tile(n_past, 256)

    c_all = jnp.concatenate([c_prompt, c_sample], axis=0)
    c_rows = -(-c_all.shape[0] // 8) * 8
    c_all = jnp.pad(c_all, ((0, c_rows - c_all.shape[0]), (0, 0)))
    mod = _modulation(c_all, w_mod, b_mod)

    xp = x_prompt.reshape(n_p, d)
    xs = x_sample.reshape(n_s, d)
    stacked_p = stacked_s = None
    for l in range(depth):
        mod_p = mod[l, :bsz].reshape(bsz, 1, 6, d)
        mod_s = jnp.repeat(mod[l, bsz:bsz + dec_b].reshape(dec_b, 6, d), dec_t, axis=0)[None]
        mp = [mod_p[:, :, i] for i in range(6)]
        ms = [mod_s[:, :, i] for i in range(6)]
        vec = lambda a: a[l].reshape(1, -1)

        wqkv = w_in[l, :, :6 * WIDTH].astype(BF16)
        wf = jnp.pad(w_in[l, :, 6 * WIDTH:6 * WIDTH + N_HEADS], ((0, 0), (0, LANES - N_HEADS))).astype(BF16)
        wg = w_in[l, :, 6 * WIDTH + N_HEADS:].astype(BF16)
        bf = b_forget[l].reshape(1, N_HEADS)
        wsb, wfx, wo = w_sb_out[l].astype(BF16), w_fox_out[l].astype(BF16), w_out[l].astype(BF16)

        qkv_p, qvt_p, *stacked_p, sg_p = _in_projection(
            xp, vec(g_pre_mix), mp[0], mp[1], wqkv, wf, wg, bf, _row_tile(seq, 256), l, depth, stacked_p)
        lf_p = stacked_p[4][l]
        cum_p, key_cols_p = _forget_cumsum(jnp.swapaxes(lf_p.reshape(bsz, seq, N_HEADS), 1, 2))
        qkv_p3 = qkv_p.reshape(bsz, seq, 6 * WIDTH)
        osb_p = _sb_prompt(qkv_p3, qvt_p, tq)
        ofx_p = _fox_prompt(qkv_p3, qvt_p, cum_p, key_cols_p, _row_tile(seq, 512))
        xp = _mix_out(xp, osb_p.reshape(n_p, WIDTH), ofx_p.reshape(n_p, WIDTH), sg_p, mp[2], vec(g_post_mix),
                      wsb, wfx, wo, _row_tile(seq, 512))

        qkv_s, _, *stacked_s, sg_s = _in_projection(
            xs, vec(g_pre_mix), ms[0], ms[1], wqkv, wf, wg, bf, n_s, l, depth, stacked_s)
        lf_s = stacked_s[4][l]
        t_all = n_past + dec_t
        t_pad = -(-t_all // CUMSUM_CHUNK) * CUMSUM_CHUNK
        lf_all = jnp.concatenate([cache_fox_logf[l].astype(F32), lf_s.reshape(dec_b, dec_t, N_HEADS),
                                  jnp.zeros((dec_b, t_pad - t_all, N_HEADS), F32)], axis=1)
        cum_s, _ = _forget_cumsum(jnp.swapaxes(lf_all, 1, 2))
        fk_s, fq_s = _pair_layouts(cum_s, n_past, dec_t)
        cache = lambda a: a[l].reshape(dec_b, n_past, WIDTH)
        osb_s, ofx_s = _sample_attention(qkv_s.reshape(dec_b, dec_t, 6 * WIDTH), cache(cache_sb_k), cache(cache_sb_v),
                                         cache(cache_fox_k), cache(cache_fox_v), fq_s, fk_s, past_chunk)
        xs = _mix_out(xs, osb_s.reshape(n_s, WIDTH), ofx_s.reshape(n_s, WIDTH), sg_s, ms[2], vec(g_post_mix),
                      wsb, wfx, wo, n_s)

        if l % 2 == 0:
            wgu, wd = w_ffn_gate_up[l // 2].astype(BF16), w_ffn_down[l // 2].astype(BF16)
            tf = _row_tile(wd.shape[0], 1408)
            xp = _ffn(xp, vec(g_pre_ffn), mp[3], mp[4], mp[5], vec(g_post_ffn), wgu, wd, _row_tile(seq, 512), tf)
            xs = _ffn(xs, vec(g_pre_ffn), ms[3], ms[4], ms[5], vec(g_post_ffn), wgu, wd, n_s, tf)
        else:
            wr = jnp.pad(w_router[l // 2], ((0, 0), (0, LANES - n_experts)))
            wgu, wd = w_moe_gate_up[l // 2].astype(BF16), w_moe_down[l // 2].astype(BF16)
            tf = _row_tile(wd.shape[1], 896)
            xp = _moe(xp, vec(g_pre_ffn), mp[3], mp[4], mp[5], vec(g_post_ffn), wr, wgu, wd, n_experts,
                      _row_tile(seq, 512), 1024, tf, _row_tile(seq, 256))
            xs = _moe(xs, vec(g_pre_ffn), ms[3], ms[4], ms[5], vec(g_post_ffn), wr, wgu, wd, n_experts,
                      n_s, 128, tf, n_s)

    split_p = [a.reshape(depth, bsz, seq, *a.shape[2:]) for a in stacked_p]
    split_s = [a.reshape(depth, dec_b, dec_t, *a.shape[2:]) for a in stacked_s]
    return (xp.reshape(bsz, seq, d), xs.reshape(dec_b, dec_t, d), *split_p, *split_s)
```

```python
import functools

import jax
import jax.numpy as jnp
from jax import lax
from jax.experimental import pallas as pl
from jax.experimental.pallas import tpu as pltpu

F32 = jnp.float32
BF16 = jnp.bfloat16

HEAD_DIM = 64
N_HEADS = 8
LANES = 128
HEAD_PAIRS = N_HEADS * HEAD_DIM // LANES
WIDTH = N_HEADS * HEAD_DIM
TOP_K = 2
RMS_EPS = 1e-6
LOG2_E = 1.4426950408889634
SB_DEAD_LOG = -104.0
VMEM_LIMIT = 56 * 1024 * 1024


def _params(sem, vmem=VMEM_LIMIT):
    return pltpu.CompilerParams(dimension_semantics=sem, vmem_limit_bytes=vmem)


def _dot(a, b):
    return jnp.dot(a, b, preferred_element_type=F32)


def _dot_nt(a, b):
    return lax.dot_general(a, b, (((1,), (1,)), ((), ())), preferred_element_type=F32)


def _sigmoid(x):
    return 1.0 / (1.0 + jnp.exp(-x))


def _softplus(x):
    return jnp.maximum(x, 0.0) + jnp.log(1.0 + jnp.exp(-jnp.abs(x)))


def _rms(x, g):
    return x * lax.rsqrt(jnp.mean(x * x, axis=-1, keepdims=True) + RMS_EPS) * g


def _mod_kernel(c_ref, w_ref, b_ref, o_ref):
    c = c_ref[...]
    s = c * _sigmoid(c)
    o_ref[0] = jnp.dot(s, w_ref[0], precision=lax.Precision.HIGHEST, preferred_element_type=F32) + b_ref[0]


def _modulation(c_all, w_mod, b_mod):
    depth, d, d6 = w_mod.shape
    rows = c_all.shape[0]
    tn = 1024
    return pl.pallas_call(
        _mod_kernel,
        grid=(depth, d6 // tn),
        in_specs=[pl.BlockSpec((rows, d), lambda l, j: (0, 0)),
                  pl.BlockSpec((1, d, tn), lambda l, j: (l, 0, j)),
                  pl.BlockSpec((1, 1, tn), lambda l, j: (l, 0, j))],
        out_specs=pl.BlockSpec((1, rows, tn), lambda l, j: (l, 0, j)),
        out_shape=jax.ShapeDtypeStruct((depth, rows, d6), F32),
        compiler_params=_params(("arbitrary", "arbitrary")),
        name="modulation",
    )(c_all, w_mod, b_mod.reshape(depth, 1, d6))


def _mod_spec(mod, tm, tiles_per_group):
    _, r, d = mod.shape
    return pl.BlockSpec((1, r, d), lambda i, *_: (i // tiles_per_group, 0, 0))


TRANSPOSED_SECTIONS = (0, 2, 3, 5)


def _inproj_kernel(x_ref, g_ref, shift_ref, scale_ref, wqkv_ref, wf_ref, wg_ref, bf_ref, *refs):
    qkv_ref, qvt_ref, ksb_ref, vsb_ref, kfx_ref, vfx_ref, lf_ref, sg_ref = refs[-8:]
    tm = x_ref.shape[0]
    h = _rms(x_ref[...], g_ref[...]) * (1.0 + scale_ref[0]) + shift_ref[0]
    hb = h.astype(BF16)
    f32_outs = {1: ksb_ref, 2: vsb_ref, 4: kfx_ref, 5: vfx_ref}
    for c in range(6):
        cols = slice(c * WIDTH, (c + 1) * WIDTH)
        acc = _dot(hb, wqkv_ref[:, cols])
        if c in f32_outs:
            f32_outs[c][0] = acc.reshape(tm, N_HEADS, HEAD_DIM)
        if c in (0, 3):
            acc = acc * (HEAD_DIM ** -0.5 * (LOG2_E if c == 3 else 1.0))
        qkv_ref[:, cols] = acc.astype(BF16)
        if c in TRANSPOSED_SECTIONS:
            t = TRANSPOSED_SECTIONS.index(c)
            qvt_ref[t * WIDTH:(t + 1) * WIDTH, :] = acc.T.astype(BF16)
    f = _dot(hb, wf_ref[...])[:, :N_HEADS] + bf_ref[...]
    lf_ref[0] = -_softplus(-f)
    for c in range(wg_ref.shape[1] // WIDTH):
        cols = slice(c * WIDTH, (c + 1) * WIDTH)
        sg_ref[:, cols] = _sigmoid(_dot(hb, wg_ref[:, cols])).astype(BF16)


def _in_projection(x, g_pre, shift, scale, wqkv, wf, wg, b_forget, tm, layer, depth, stacked):
    n, d = x.shape
    tiles_per_group = (n // tm) // shift.shape[0]
    const = lambda i: (0, 0)
    row = lambda i: (i, 0)
    heads_spec = pl.BlockSpec((1, tm, N_HEADS, HEAD_DIM), lambda i: (layer, i, 0, 0))
    heads_shape = jax.ShapeDtypeStruct((depth, n, N_HEADS, HEAD_DIM), F32)
    n_in = 8
    stacked = () if stacked is None else tuple(stacked)
    return pl.pallas_call(
        _inproj_kernel,
        grid=(n // tm,),
        in_specs=[pl.BlockSpec((tm, d), row),
                  pl.BlockSpec((1, d), const),
                  _mod_spec(shift, tm, tiles_per_group),
                  _mod_spec(scale, tm, tiles_per_group),
                  pl.BlockSpec(wqkv.shape, const),
                  pl.BlockSpec(wf.shape, const),
                  pl.BlockSpec(wg.shape, const),
                  pl.BlockSpec((1, N_HEADS), const)] + [pl.BlockSpec(memory_space=pl.ANY)] * len(stacked),
        out_specs=[pl.BlockSpec((tm, 6 * WIDTH), row), pl.BlockSpec((len(TRANSPOSED_SECTIONS) * WIDTH, tm), lambda i: (0, i))]
                  + [heads_spec] * 4
                  + [pl.BlockSpec((1, tm, N_HEADS), lambda i: (layer, i, 0)), pl.BlockSpec((tm, wg.shape[1]), row)],
        out_shape=[jax.ShapeDtypeStruct((n, 6 * WIDTH), BF16),
                   jax.ShapeDtypeStruct((len(TRANSPOSED_SECTIONS) * WIDTH, n), BF16)] + [heads_shape] * 4
                  + [jax.ShapeDtypeStruct((depth, n, N_HEADS), F32), jax.ShapeDtypeStruct((n, wg.shape[1]), BF16)],
        input_output_aliases={n_in + k: 2 + k for k in range(len(stacked))},
        compiler_params=_params(("arbitrary",)),
        name="in_projection",
    )(x, g_pre, shift, scale, wqkv, wf, wg, b_forget, *stacked)


CUMSUM_CHUNK = 256


FORGET_SPLIT = 3


def _cumsum_kernel(x_ref, o_ref, kcol_ref=None):
    t = x_ref.shape[2]
    r = lax.broadcasted_iota(jnp.int32, (CUMSUM_CHUNK, CUMSUM_CHUNK), 0)
    c = lax.broadcasted_iota(jnp.int32, (CUMSUM_CHUNK, CUMSUM_CHUNK), 1)
    upper = jnp.where(r <= c, 1.0, 0.0).astype(F32)
    row = lax.broadcasted_iota(jnp.int32, (LANES, LANES), 0)
    lane = lax.broadcasted_iota(jnp.int32, (LANES, LANES), 1)
    head, term = row % N_HEADS, row // N_HEADS
    placed = jnp.logical_and(row < FORGET_SPLIT * N_HEADS, lane == (head % 2) * FORGET_SPLIT + term)
    pad_rows = jnp.zeros((LANES - FORGET_SPLIT * N_HEADS, CUMSUM_CHUNK), F32)

    def step(i, carry):
        start = pl.multiple_of(i * CUMSUM_CHUNK, CUMSUM_CHUNK)
        seg = x_ref[0, :, pl.ds(start, CUMSUM_CHUNK)]
        cs = jnp.dot(seg, upper, precision=lax.Precision.HIGHEST, preferred_element_type=F32) + carry
        f_log2 = cs * LOG2_E
        o_ref[0, :, pl.ds(start, CUMSUM_CHUNK)] = f_log2
        if kcol_ref is None:
            return cs[:, CUMSUM_CHUNK - 1:CUMSUM_CHUNK]
        rest = -f_log2
        terms = []
        for _ in range(FORGET_SPLIT):
            part = pltpu.bitcast(pltpu.bitcast(rest, jnp.uint32) & jnp.uint32(0xFFFF0000), F32)
            terms.append(part)
            rest = rest - part
        terms_t = jnp.concatenate(terms + [pad_rows], axis=0).T.astype(BF16)
        for p in range(HEAD_PAIRS):
            select = jnp.where(jnp.logical_and(placed, head // 2 == p), 1.0, 0.0).astype(BF16)
            kcol_ref[0, p, pl.ds(start, CUMSUM_CHUNK), :] = _dot(terms_t, select).astype(BF16)
        return cs[:, CUMSUM_CHUNK - 1:CUMSUM_CHUNK]

    lax.fori_loop(0, t // CUMSUM_CHUNK, step, jnp.zeros((x_ref.shape[1], 1), F32))


def _forget_cumsum(x, key_columns):
    b, r, t = x.shape
    out_specs = [pl.BlockSpec((1, r, t), lambda i: (i, 0, 0))]
    out_shape = [jax.ShapeDtypeStruct((b, r, t), F32)]
    if key_columns:
        out_specs.append(pl.BlockSpec((1, HEAD_PAIRS, t, LANES), lambda i: (i, 0, 0, 0)))
        out_shape.append(jax.ShapeDtypeStruct((b, HEAD_PAIRS, t, LANES), BF16))
    return pl.pallas_call(
        _cumsum_kernel,
        grid=(b,),
        in_specs=[pl.BlockSpec((1, r, t), lambda i: (i, 0, 0))],
        out_specs=out_specs,
        out_shape=out_shape,
        compiler_params=_params(("arbitrary",)),
        name="forget_cumsum",
    )(x)


def _pair_layouts(cum_t, t_query_start, t_query):
    b, _, tp = cum_t.shape
    f_keys = cum_t.reshape(b, HEAD_PAIRS, 2, tp)
    f_query = jnp.swapaxes(f_keys[:, :, :, t_query_start:t_query_start + t_query], 2, 3)
    return f_keys, f_query


def _head_masks():
    lane = lax.broadcasted_iota(jnp.int32, (1, LANES), 1)
    return lane < HEAD_DIM


def _split_heads(q2, first_head):
    zero = jnp.zeros_like(q2)
    return jnp.where(first_head, q2, zero), jnp.where(first_head, zero, q2)


def _strict_lower_neg(n):
    r = lax.broadcasted_iota(jnp.int32, (n, n), 0)
    c = lax.broadcasted_iota(jnp.int32, (n, n), 1)
    return jnp.where(r > c, -1.0, 0.0).astype(BF16)


def _sb_block(qh, k2, v2, neg_tri, visible, acc_ref, r_ref, hd):
    z = _dot_nt(qh, k2)
    sp = _softplus(z)
    if visible is not None:
        sp = jnp.where(visible, sp, 0.0)
    later = _dot(sp.astype(BF16), neg_tri)
    p = jnp.exp(z - sp + later)
    if visible is not None:
        p = jnp.where(visible, p, 0.0)
    r = r_ref[hd]
    acc_ref[hd] += jnp.exp(r) * _dot(p.astype(BF16), v2)
    r_ref[hd] = r + later[:, 0:1] - sp[:, 0:1]


def _fox_block(qh, k2, v2_ones, fq, fk, allowed, acc_ref, m_ref, hd):
    s = _dot_nt(qh, k2) + fq - fk
    if allowed is not None:
        s = jnp.where(allowed, s, -jnp.inf)
    m_old = m_ref[hd]
    m_new = jnp.maximum(m_old, jnp.max(s, axis=-1, keepdims=True))
    p = jnp.exp2(s - m_new)
    acc_ref[hd] = jnp.exp2(m_old - m_new) * acc_ref[hd] + _dot(p.astype(BF16), v2_ones)
    m_ref[hd] = m_new


def _fox_finish(acc_ref, first_head):
    a, b = acc_ref[0], acc_ref[1]
    num = jnp.where(first_head, a, b)
    den = jnp.where(first_head, pltpu.roll(a, HEAD_DIM, 1), pltpu.roll(b, HEAD_DIM, 1))
    return num / den


def _with_ones(v2, first_head):
    one = jnp.ones_like(v2)
    return jnp.where(first_head, v2, one), jnp.where(first_head, one, v2)


def _sb_prompt_kernel(qt_ref, k_ref, vt_ref, o_ref, w_ref, acc_ref, r_ref, tri_ref, *, tq):
    i = pl.program_id(2)
    sub = lax.broadcasted_iota(jnp.int32, (LANES, tq), 0)
    first_rows = sub < HEAD_DIM
    qt = qt_ref[...]
    zero = jnp.zeros_like(qt)
    w_ref[0] = jnp.where(first_rows, qt, zero)
    w_ref[1] = jnp.where(first_rows, zero, qt)
    acc_ref[...] = jnp.zeros_like(acc_ref)
    r_ref[...] = jnp.zeros_like(r_ref)

    @pl.when(jnp.logical_and(jnp.logical_and(pl.program_id(0) == 0, pl.program_id(1) == 0), i == 0))
    def _():
        s_idx = lax.broadcasted_iota(jnp.int32, tri_ref.shape, 0)
        j_idx = lax.broadcasted_iota(jnp.int32, tri_ref.shape, 1)
        tri_ref[...] = jnp.where(j_idx > s_idx, -1.0, 0.0).astype(BF16)

    def block(start, size, vis):
        keys = k_ref[0, pl.ds(start, size), :]
        vt = vt_ref[:, pl.ds(start, size)]
        neg_tri = tri_ref[:size, :size]
        for hd in range(2):
            z = _dot(keys, w_ref[hd])
            sp = _softplus(z)
            if vis is not None:
                sp = jnp.where(vis, sp, 0.0)
            later = _dot(neg_tri, sp.astype(BF16))
            p = jnp.exp(z - sp + later)
            if vis is not None:
                p = jnp.where(vis, p, 0.0)
            r = r_ref[hd]
            acc_ref[hd] += jnp.exp(r) * _dot(vt, p.astype(BF16))
            r_ref[hd] = r + later[0:1, :] - sp[0:1, :]

    def live():
        return jnp.max(jnp.maximum(r_ref[0], r_ref[1]))

    first = jnp.maximum(i - 1, 0)
    key_idx = lax.broadcasted_iota(jnp.int32, (2 * tq, tq), 0) + first * tq
    query_idx = lax.broadcasted_iota(jnp.int32, (2 * tq, tq), 1) + i * tq
    block(pl.multiple_of(first * tq, tq), 2 * tq, key_idx < query_idx)

    def cond(carry):
        j, r_max = carry
        return jnp.logical_and(j >= 0, r_max > SB_DEAD_LOG)

    def body(carry):
        j, _ = carry
        block(pl.multiple_of(j * tq, tq), tq, None)
        return j - 1, live()

    lax.while_loop(cond, body, (i - 2, live()))
    o_ref[0] = jnp.where(first_rows, acc_ref[0], acc_ref[1]).T.astype(o_ref.dtype)


def _sb_prompt(qkv, qv_t, tq):
    b, t, _ = qkv.shape
    nq = t // tq
    kernel = functools.partial(_sb_prompt_kernel, tq=tq)
    return pl.pallas_call(
        kernel,
        grid=(b, HEAD_PAIRS, nq),
        in_specs=[pl.BlockSpec((LANES, tq), lambda b_, p, i: (p, b_ * nq + i)),
                  pl.BlockSpec((1, t, LANES), lambda b_, p, i: (b_, 0, HEAD_PAIRS + p)),
                  pl.BlockSpec((LANES, t), lambda b_, p, i: (HEAD_PAIRS + p, b_))],
        out_specs=pl.BlockSpec((1, tq, LANES), lambda b_, p, i: (b_, i, p)),
        out_shape=jax.ShapeDtypeStruct((b, t, WIDTH), BF16),
        scratch_shapes=[pltpu.VMEM((2, LANES, tq), BF16), pltpu.VMEM((2, LANES, tq), F32),
                        pltpu.VMEM((2, 1, tq), F32), pltpu.VMEM((2 * tq, 2 * tq), BF16)],
        compiler_params=_params(("arbitrary", "arbitrary", "arbitrary")),
        name="sb_attention_prompt",
    )(qv_t, qkv, qv_t)


FOX_DEAD_LOG2 = -150.0
FOX_BOUNDED_MAX = 40.0


def _fox_prompt_kernel(kmax_ref, fend_ref, qt_ref, k_ref, kf_ref, vt_ref, fq_ref, o_ref,
                       w_ref, acc_ref, m_ref, zb_ref, *, tq):
    bi = pl.program_id(0)
    pi = pl.program_id(1)
    i = pl.program_id(2)
    sub = lax.broadcasted_iota(jnp.int32, (LANES, tq), 0)
    first_rows = sub < HEAD_DIM
    qt = qt_ref[...]
    zero = jnp.zeros_like(qt)
    for hd in range(2):
        lo = hd * FORGET_SPLIT
        select = jnp.where(jnp.logical_and(sub >= lo, sub < lo + FORGET_SPLIT), 1.0, 0.0).astype(BF16)
        q_head = jnp.where(first_rows, qt, zero) if hd == 0 else jnp.where(first_rows, zero, qt)
        w_ref[hd, :LANES, :] = q_head
        w_ref[hd, LANES:, :] = select
        q_f32 = q_head.astype(F32)
        q_norm = jnp.sqrt(jnp.sum(q_f32 * q_f32, axis=0, keepdims=True))
        zb_ref[hd] = q_norm * (kmax_ref[bi, 2 * pi + hd] * 1.01)
    acc_ref[...] = jnp.zeros_like(acc_ref)
    m_ref[...] = jnp.full_like(m_ref, -jnp.inf)
    key_idx = lax.broadcasted_iota(jnp.int32, (tq, tq), 0)
    query_idx = lax.broadcasted_iota(jnp.int32, (tq, tq), 1)
    allowed = key_idx <= query_idx
    den_row = (HEAD_DIM, 0)

    def operands(j):
        start = pl.multiple_of(j * tq, tq)
        keys = jnp.concatenate([k_ref[0, pl.ds(start, tq), :], kf_ref[0, 0, pl.ds(start, tq), :]], axis=1)
        vt = vt_ref[:, pl.ds(start, tq)]
        one = jnp.ones_like(vt)
        first_v = lax.broadcasted_iota(jnp.int32, vt.shape, 0) < HEAD_DIM
        return keys, (jnp.where(first_v, vt, one), jnp.where(first_v, one, vt))

    def block_bounded(j, mask):
        keys, vts = operands(j)
        for hd in range(2):
            u = _dot(keys, w_ref[hd])
            if mask is not None:
                u = jnp.where(mask, u, -jnp.inf)
            p = jnp.exp2(u + (fq_ref[0, 0, hd:hd + 1, :] - zb_ref[hd]))
            acc_ref[hd] += _dot(vts[hd], p.astype(BF16))

    def alive_bounded(j):
        jc = jnp.maximum(j, 0)
        reach = []
        for hd in range(2):
            den = acc_ref[hd, den_row[hd]:den_row[hd] + 1, :]
            gap = jnp.max(fq_ref[0, 0, hd:hd + 1, :] - jnp.log2(den))
            reach.append(gap - fend_ref[bi, pi, hd, jc])
        return jnp.maximum(reach[0], reach[1]) + tq.bit_length() > FOX_DEAD_LOG2

    def block_exact(j, mask):
        keys, vts = operands(j)
        for hd in range(2):
            u = _dot(keys, w_ref[hd])
            if mask is not None:
                u = jnp.where(mask, u, -jnp.inf)
            fq = fq_ref[0, 0, hd:hd + 1, :]
            m_old = m_ref[hd]
            m_new = jnp.maximum(m_old, jnp.max(u, axis=0, keepdims=True) + fq)
            p = jnp.exp2(u + (fq - m_new))
            acc_ref[hd] = jnp.exp2(m_old - m_new) * acc_ref[hd] + _dot(vts[hd], p.astype(BF16))
            m_ref[hd] = m_new

    def alive_exact(j):
        jc = jnp.maximum(j, 0)
        reach = [jnp.max(zb_ref[hd] + fq_ref[0, 0, hd:hd + 1, :] - m_ref[hd]) - fend_ref[bi, pi, hd, jc]
                 for hd in range(2)]
        return jnp.maximum(reach[0], reach[1]) > FOX_DEAD_LOG2

    def sweep(block, alive):
        block(i, allowed)

        def cond(carry):
            j, go = carry
            return jnp.logical_and(j >= 0, go)

        def body(carry):
            j, _ = carry
            block(j, None)
            return j - 1, alive(j - 1)

        lax.while_loop(cond, body, (i - 1, alive(i - 1)))

    bounded = jnp.max(jnp.maximum(zb_ref[0], zb_ref[1])) <= FOX_BOUNDED_MAX

    @pl.when(bounded)
    def _():
        sweep(block_bounded, alive_bounded)

    @pl.when(jnp.logical_not(bounded))
    def _():
        sweep(block_exact, alive_exact)

    a, b = acc_ref[0], acc_ref[1]
    out_t = jnp.where(first_rows, a / a[HEAD_DIM:HEAD_DIM + 1, :], b / b[0:1, :])
    o_ref[0] = out_t.T.astype(o_ref.dtype)


def _fox_prompt(qkv, qv_t, cum_t, key_cols, tq):
    b, t, _ = qkv.shape
    nq = t // tq
    f_rows = cum_t.reshape(b, HEAD_PAIRS, 2, t)
    f_block_end = f_rows[:, :, :, tq - 1::tq]
    k_fx = qkv[:, :, 4 * WIDTH:5 * WIDTH].astype(F32).reshape(b, t, N_HEADS, HEAD_DIM)
    k_max = jnp.sqrt(jnp.max(jnp.sum(k_fx * k_fx, axis=-1), axis=1))
    smem = pl.BlockSpec(memory_space=pltpu.SMEM)
    kernel = functools.partial(_fox_prompt_kernel, tq=tq)
    return pl.pallas_call(
        kernel,
        grid=(b, HEAD_PAIRS, nq),
        in_specs=[smem, smem,
                  pl.BlockSpec((LANES, tq), lambda b_, p, i: (2 * HEAD_PAIRS + p, b_ * nq + i)),
                  pl.BlockSpec((1, t, LANES), lambda b_, p, i: (b_, 0, 4 * HEAD_PAIRS + p)),
                  pl.BlockSpec((1, 1, t, LANES), lambda b_, p, i: (b_, p, 0, 0)),
                  pl.BlockSpec((LANES, t), lambda b_, p, i: (3 * HEAD_PAIRS + p, b_)),
                  pl.BlockSpec((1, 1, 2, tq), lambda b_, p, i: (b_, p, 0, i))],
        out_specs=pl.BlockSpec((1, tq, LANES), lambda b_, p, i: (b_, i, p)),
        out_shape=jax.ShapeDtypeStruct((b, t, WIDTH), BF16),
        scratch_shapes=[pltpu.VMEM((2, 2 * LANES, tq), BF16), pltpu.VMEM((2, LANES, tq), F32),
                        pltpu.VMEM((2, 1, tq), F32), pltpu.VMEM((2, 1, tq), F32)],
        compiler_params=_params(("arbitrary", "arbitrary", "arbitrary")),
        name="fox_attention_prompt",
    )(k_max, f_block_end, qv_t, qkv, key_cols, qv_t, f_rows)


def _sample_attn_kernel(qsb_ref, ksb_ref, vsb_ref, qfx_ref, kfx_ref, vfx_ref,
                        csk_ref, csv_ref, cfk_ref, cfv_ref, fq_ref, fk_ref,
                        osb_ref, ofx_ref, acc_ref, st_ref, *, tk):
    t_new = qsb_ref.shape[1]
    n_past = csk_ref.shape[1]
    n_blocks = n_past // tk
    first_head = _head_masks()
    row = lax.broadcasted_iota(jnp.int32, (t_new, t_new), 0)
    col = lax.broadcasted_iota(jnp.int32, (t_new, t_new), 1)

    qs = _split_heads(qsb_ref[0], first_head)
    acc_ref[...] = jnp.zeros_like(acc_ref)
    st_ref[...] = jnp.zeros_like(st_ref)
    tri_new = _strict_lower_neg(t_new)
    tri_past = _strict_lower_neg(tk)
    for hd in range(2):
        _sb_block(qs[hd], ksb_ref[0], vsb_ref[0], tri_new, col < row, acc_ref, st_ref, hd)

    def live():
        return jnp.max(jnp.maximum(st_ref[0], st_ref[1]))

    def sb_cond(carry):
        j, r_max = carry
        return jnp.logical_and(j >= 0, r_max > SB_DEAD_LOG)

    def sb_body(carry):
        j, _ = carry
        start = pl.multiple_of(j * tk, tk)
        k2 = csk_ref[0, pl.ds(start, tk), :].astype(BF16)
        v2 = csv_ref[0, pl.ds(start, tk), :].astype(BF16)
        for hd in range(2):
            _sb_block(qs[hd], k2, v2, tri_past, None, acc_ref, st_ref, hd)
        return j - 1, live()

    lax.while_loop(sb_cond, sb_body, (n_blocks - 1, live()))
    osb_ref[0] = jnp.where(first_head, acc_ref[0], acc_ref[1]).astype(osb_ref.dtype)

    qs = _split_heads(qfx_ref[0], first_head)
    acc_ref[...] = jnp.zeros_like(acc_ref)
    st_ref[...] = jnp.full_like(st_ref, -jnp.inf)
    fq = fq_ref[0, 0]
    vs = _with_ones(vfx_ref[0], first_head)
    for hd in range(2):
        fk = fk_ref[0, 0, hd:hd + 1, n_past:n_past + t_new]
        _fox_block(qs[hd], kfx_ref[0], vs[hd], fq[:, hd:hd + 1], fk, col <= row, acc_ref, st_ref, hd)

    k2 = cfk_ref[0].astype(BF16)
    vs_ = _with_ones(cfv_ref[0].astype(BF16), first_head)
    for hd in range(2):
        fk = fk_ref[0, 0, hd:hd + 1, 0:n_past]
        _fox_block(qs[hd], k2, vs_[hd], fq[:, hd:hd + 1], fk, None, acc_ref, st_ref, hd)
    ofx_ref[0] = _fox_finish(acc_ref, first_head).astype(ofx_ref.dtype)


def _sample_attention(qkv, cache_sb_k, cache_sb_v, cache_fox_k, cache_fox_v, f_query, f_keys, tk):
    b, t_new, _ = qkv.shape
    n_past = cache_sb_k.shape[1]
    tp = f_keys.shape[3]
    new = lambda c: pl.BlockSpec((1, t_new, LANES), lambda b_, p: (b_, 0, c * HEAD_PAIRS + p))
    past = pl.BlockSpec((1, n_past, LANES), lambda b_, p: (b_, 0, p))
    out = pl.BlockSpec((1, t_new, LANES), lambda b_, p: (b_, 0, p))
    kernel = functools.partial(_sample_attn_kernel, tk=tk)
    return pl.pallas_call(
        kernel,
        grid=(b, HEAD_PAIRS),
        in_specs=[new(0), new(1), new(2), new(3), new(4), new(5), past, past, past, past,
                  pl.BlockSpec((1, 1, t_new, 2), lambda b_, p: (b_, p, 0, 0)),
                  pl.BlockSpec((1, 1, 2, tp), lambda b_, p: (b_, p, 0, 0))],
        out_specs=[out, out],
        out_shape=[jax.ShapeDtypeStruct((b, t_new, WIDTH), BF16)] * 2,
        scratch_shapes=[pltpu.VMEM((2, t_new, LANES), F32), pltpu.VMEM((2, t_new, 1), F32)],
        compiler_params=_params(("arbitrary", "arbitrary")),
        name="attention_sample",
    )(qkv, qkv, qkv, qkv, qkv, qkv, cache_sb_k, cache_sb_v, cache_fox_k, cache_fox_v, f_query, f_keys)


def _mix_out_kernel(x_ref, osb_ref, ofx_ref, sg_ref, gate_ref, g_ref, wsb_ref, wfx_ref, wo_ref, o_ref):
    d = x_ref.shape[1]
    merged = (sg_ref[:, :d].astype(F32) * _dot(osb_ref[...], wsb_ref[...])
              + sg_ref[:, d:].astype(F32) * _dot(ofx_ref[...], wfx_ref[...]))
    y = _dot(merged.astype(BF16), wo_ref[...])
    o_ref[...] = x_ref[...] + gate_ref[0] * _rms(y, g_ref[...])


def _mix_out(x, o_sb, o_fx, sg, gate, g_post, w_sb_out, w_fox_out, w_out, tm):
    n, d = x.shape
    tiles_per_group = (n // tm) // gate.shape[0]
    const = lambda i: (0, 0)
    row = lambda i: (i, 0)
    return pl.pallas_call(
        _mix_out_kernel,
        grid=(n // tm,),
        in_specs=[pl.BlockSpec((tm, d), row), pl.BlockSpec((tm, WIDTH), row), pl.BlockSpec((tm, WIDTH), row),
                  pl.BlockSpec((tm, 2 * d), row), _mod_spec(gate, tm, tiles_per_group),
                  pl.BlockSpec((1, d), const), pl.BlockSpec(w_sb_out.shape, const),
                  pl.BlockSpec(w_fox_out.shape, const), pl.BlockSpec(w_out.shape, const)],
        out_specs=pl.BlockSpec((tm, d), row),
        out_shape=jax.ShapeDtypeStruct((n, d), F32),
        compiler_params=_params(("arbitrary",)),
        name="mix_out",
    )(x, o_sb, o_fx, sg, gate, g_post, w_sb_out, w_fox_out, w_out)


def _swiglu_chunk(hb, wa, wb, wd):
    a = _dot(hb, wa)
    b = _dot(hb, wb)
    return a * _sigmoid(a) * b, wd


def _ffn_kernel(x_ref, gpre_ref, shift_ref, scale_ref, gate_ref, gpost_ref, wa_ref, wb_ref, wd_ref,
                o_ref, h_ref, acc_ref):
    f = pl.program_id(1)

    @pl.when(f == 0)
    def _():
        h = _rms(x_ref[...], gpre_ref[...]) * (1.0 + scale_ref[0]) + shift_ref[0]
        h_ref[...] = h.astype(BF16)
        acc_ref[...] = jnp.zeros_like(acc_ref)

    hb = h_ref[...]
    a = _dot(hb, wa_ref[...])
    b = _dot(hb, wb_ref[...])
    act = a * _sigmoid(a) * b
    acc_ref[...] += _dot(act.astype(BF16), wd_ref[...])

    @pl.when(f == pl.num_programs(1) - 1)
    def _():
        o_ref[...] = x_ref[...] + gate_ref[0] * _rms(acc_ref[...], gpost_ref[...])


def _ffn(x, g_pre, shift, scale, gate, g_post, w_gate_up, w_down, tm, tf):
    n, d = x.shape
    d_ff = w_down.shape[0]
    nf = d_ff // tf
    tiles_per_group = (n // tm) // shift.shape[0]
    const = lambda i, f: (0, 0)
    row = lambda i, f: (i, 0)
    return pl.pallas_call(
        _ffn_kernel,
        grid=(n // tm, nf),
        in_specs=[pl.BlockSpec((tm, d), row), pl.BlockSpec((1, d), const),
                  _mod_spec(shift, tm, tiles_per_group), _mod_spec(scale, tm, tiles_per_group),
                  _mod_spec(gate, tm, tiles_per_group), pl.BlockSpec((1, d), const),
                  pl.BlockSpec((d, tf), lambda i, f: (0, f)),
                  pl.BlockSpec((d, tf), lambda i, f: (0, nf + f)),
                  pl.BlockSpec((tf, d), lambda i, f: (f, 0))],
        out_specs=pl.BlockSpec((tm, d), row),
        out_shape=jax.ShapeDtypeStruct((n, d), F32),
        scratch_shapes=[pltpu.VMEM((tm, d), BF16), pltpu.VMEM((tm, d), F32)],
        compiler_params=_params(("arbitrary", "arbitrary")),
        name="ffn_dense",
    )(x, g_pre, shift, scale, gate, g_post, w_gate_up, w_gate_up, w_down)


def _route_kernel(x_ref, gpre_ref, shift_ref, scale_ref, wr_ref, h_ref, idx_ref, w_ref, *, n_experts):
    h = _rms(x_ref[...], gpre_ref[...]) * (1.0 + scale_ref[0]) + shift_ref[0]
    h_ref[...] = h
    logits = jnp.dot(h, wr_ref[...], precision=lax.Precision.HIGHEST, preferred_element_type=F32)[:, :n_experts]
    idx = lax.broadcasted_iota(jnp.int32, logits.shape, 1)
    m1 = jnp.max(logits, axis=-1, keepdims=True)
    i1 = jnp.min(jnp.where(logits == m1, idx, n_experts), axis=-1, keepdims=True)
    rest = jnp.where(idx == i1, -jnp.inf, logits)
    m2 = jnp.max(rest, axis=-1, keepdims=True)
    i2 = jnp.min(jnp.where(rest == m2, idx, n_experts), axis=-1, keepdims=True)
    e2 = jnp.exp(m2 - m1)
    slot = lax.broadcasted_iota(jnp.int32, idx_ref.shape, 1)
    idx_ref[...] = jnp.where(slot == 0, i1, i2)
    w_ref[...] = jnp.where(slot == 0, 1.0 / (1.0 + e2), e2 / (1.0 + e2))


def _route(x, g_pre, shift, scale, w_router, n_experts, tm):
    n, d = x.shape
    tiles_per_group = (n // tm) // shift.shape[0]
    const = lambda i: (0, 0)
    row = lambda i: (i, 0)
    return pl.pallas_call(
        functools.partial(_route_kernel, n_experts=n_experts),
        grid=(n // tm,),
        in_specs=[pl.BlockSpec((tm, d), row), pl.BlockSpec((1, d), const),
                  _mod_spec(shift, tm, tiles_per_group), _mod_spec(scale, tm, tiles_per_group),
                  pl.BlockSpec(w_router.shape, const)],
        out_specs=[pl.BlockSpec((tm, d), row), pl.BlockSpec((tm, TOP_K), row), pl.BlockSpec((tm, TOP_K), row)],
        out_shape=[jax.ShapeDtypeStruct((n, d), F32), jax.ShapeDtypeStruct((n, TOP_K), jnp.int32),
                   jax.ShapeDtypeStruct((n, TOP_K), F32)],
        compiler_params=_params(("arbitrary",)),
        name="moe_route",
    )(x, g_pre, shift, scale, w_router)


def _dispatch_tables(top_idx, n_experts, tm):
    n = top_idx.shape[0]
    n_assign = n * TOP_K
    expert = top_idx.reshape(n_assign)
    onehot = (expert[:, None] == jnp.arange(n_experts, dtype=jnp.int32)[None, :]).astype(jnp.int32)
    rank = jnp.sum((jnp.cumsum(onehot, axis=0) - onehot) * onehot, axis=1)
    counts = jnp.sum(onehot, axis=0)
    padded = (counts + tm - 1) // tm * tm
    group_end = jnp.cumsum(padded)
    pos = (group_end - padded)[expert] + rank
    p_rows = n_assign + n_experts * tm
    src_token = jnp.zeros((p_rows,), jnp.int32).at[pos].set(jnp.arange(n_assign, dtype=jnp.int32) // TOP_K)
    tile_start = jnp.arange(p_rows // tm, dtype=jnp.int32) * tm
    tile_expert = jnp.minimum(jnp.sum(tile_start[:, None] >= group_end[None, :], axis=1), n_experts - 1)
    n_used = (group_end[-1] // tm).reshape(1)
    return src_token, tile_expert.astype(jnp.int32), n_used.astype(jnp.int32), pos.astype(jnp.int32)


def _gather_rows(src_hbm, dst, sem, index_of_row, n_rows):
    def issue(r, carry):
        pltpu.make_async_copy(src_hbm.at[pl.ds(index_of_row(r), 1), :], dst.at[pl.ds(r, 1), :], sem).start()
        return carry

    lax.fori_loop(0, n_rows, issue, 0, unroll=8)


def _wait_rows(src_hbm, dst, sem):
    pltpu.make_async_copy(src_hbm.at[pl.ds(0, dst.shape[0]), :], dst, sem).wait()


def _experts_kernel(src_ref, expert_ref, used_ref, h_hbm, wa_ref, wb_ref, wd_ref, o_ref,
                    rows_ref, xb_ref, acc_ref, sem, *, tm):
    i = pl.program_id(0)
    f = pl.program_id(1)
    n_used = used_ref[0]
    slot = lax.rem(i, 2)
    active = i < n_used

    def gather(tile, into):
        _gather_rows(h_hbm, rows_ref.at[into], sem.at[into], lambda r: src_ref[tile * tm + r], tm)

    @pl.when(jnp.logical_and(f == 0, active))
    def _():
        @pl.when(i == 0)
        def _():
            gather(0, 0)

        _wait_rows(h_hbm, rows_ref.at[slot], sem.at[slot])

        @pl.when(i + 1 < n_used)
        def _():
            gather(i + 1, 1 - slot)

        xb_ref[...] = rows_ref[slot].astype(BF16)
        acc_ref[...] = jnp.zeros_like(acc_ref)

    @pl.when(active)
    def _():
        xb = xb_ref[...]
        a = _dot(xb, wa_ref[0])
        b = _dot(xb, wb_ref[0])
        act = a * _sigmoid(a) * b
        acc_ref[...] += _dot(act.astype(BF16), wd_ref[0])

    last = f == pl.num_programs(1) - 1

    @pl.when(jnp.logical_and(last, active))
    def _():
        o_ref[...] = acc_ref[...]

    @pl.when(jnp.logical_and(last, jnp.logical_not(active)))
    def _():
        o_ref[...] = jnp.zeros_like(o_ref)


def _experts(h, src_token, tile_expert, n_used, w_gate_up, w_down, tm, tf):
    p_rows = src_token.shape[0]
    d = h.shape[1]
    nf = w_down.shape[1] // tf
    ff = lambda i, f, used: jnp.where(i < used[0], f, nf - 1)
    grid_spec = pltpu.PrefetchScalarGridSpec(
        num_scalar_prefetch=3,
        grid=(p_rows // tm, nf),
        in_specs=[pl.BlockSpec(memory_space=pl.ANY),
                  pl.BlockSpec((1, d, tf), lambda i, f, src, ex, used: (ex[i], 0, ff(i, f, used))),
                  pl.BlockSpec((1, d, tf), lambda i, f, src, ex, used: (ex[i], 0, nf + ff(i, f, used))),
                  pl.BlockSpec((1, tf, d), lambda i, f, src, ex, used: (ex[i], ff(i, f, used), 0))],
        out_specs=pl.BlockSpec((tm, d), lambda i, f, src, ex, used: (i, 0)),
        scratch_shapes=[pltpu.VMEM((2, tm, d), F32), pltpu.VMEM((tm, d), BF16), pltpu.VMEM((tm, d), F32),
                        pltpu.SemaphoreType.DMA((2,))])
    return pl.pallas_call(
        functools.partial(_experts_kernel, tm=tm),
        grid_spec=grid_spec,
        out_shape=jax.ShapeDtypeStruct((p_rows, d), F32),
        compiler_params=_params(("arbitrary", "arbitrary")),
        name="moe_experts",
    )(src_token, tile_expert, n_used, h, w_gate_up, w_gate_up, w_down)


def _combine_kernel(pos_ref, y_hbm, x_ref, w_ref, gate_ref, gpost_ref, o_ref, rows_ref, sem, *, tm):
    i = pl.program_id(0)
    slot = lax.rem(i, 2)

    def gather(tile, into):
        for k in range(TOP_K):
            _gather_rows(y_hbm, rows_ref.at[into, k], sem.at[into],
                         lambda r, k=k: pos_ref[(tile * tm + r) * TOP_K + k], tm)

    @pl.when(i == 0)
    def _():
        gather(0, 0)

    for k in range(TOP_K):
        _wait_rows(y_hbm, rows_ref.at[slot, k], sem.at[slot])

    @pl.when(i + 1 < pl.num_programs(0))
    def _():
        gather(i + 1, 1 - slot)

    w = w_ref[...]
    mixed = w[:, 0:1] * rows_ref[slot, 0] + w[:, 1:2] * rows_ref[slot, 1]
    o_ref[...] = x_ref[...] + gate_ref[0] * _rms(mixed, gpost_ref[...])


def _combine(y, pos, top_w, x, gate, g_post, tm):
    n, d = x.shape
    tiles_per_group = (n // tm) // gate.shape[0]
    grid_spec = pltpu.PrefetchScalarGridSpec(
        num_scalar_prefetch=1,
        grid=(n // tm,),
        in_specs=[pl.BlockSpec(memory_space=pl.ANY),
                  pl.BlockSpec((tm, d), lambda i, pos_: (i, 0)),
                  pl.BlockSpec((tm, TOP_K), lambda i, pos_: (i, 0)),
                  _mod_spec(gate, tm, tiles_per_group),
                  pl.BlockSpec((1, d), lambda i, pos_: (0, 0))],
        out_specs=pl.BlockSpec((tm, d), lambda i, pos_: (i, 0)),
        scratch_shapes=[pltpu.VMEM((2, TOP_K, tm, d), F32), pltpu.SemaphoreType.DMA((2,))])
    return pl.pallas_call(
        functools.partial(_combine_kernel, tm=tm),
        grid_spec=grid_spec,
        out_shape=jax.ShapeDtypeStruct((n, d), F32),
        compiler_params=_params(("arbitrary",)),
        name="moe_combine",
    )(pos, y, x, top_w, gate, g_post)


def _moe(x, g_pre, shift, scale, gate, g_post, w_router, w_gate_up, w_down, n_experts, tm_route, tm, tf, tm_out):
    h, top_idx, top_w = _route(x, g_pre, shift, scale, w_router, n_experts, tm_route)
    src_token, tile_expert, n_used, pos = _dispatch_tables(top_idx, n_experts, tm)
    y = _experts(h, src_token, tile_expert, n_used, w_gate_up, w_down, tm, tf)
    return _combine(y, pos, top_w, x, gate, g_post, tm_out)


def _row_tile(n, want):
    return want if n % want == 0 else n


def kernel(x_prompt, x_sample, c_prompt, c_sample, cache_sb_k, cache_sb_v, cache_fox_k, cache_fox_v, cache_fox_logf, w_mod, b_mod, g_pre_mix, g_post_mix, g_pre_ffn, g_post_ffn, w_in, b_forget, w_sb_out, w_fox_out, w_out, w_ffn_gate_up, w_ffn_down, w_router, w_moe_gate_up, w_moe_down):
    bsz, seq, d = x_prompt.shape
    dec_b, dec_t, _ = x_sample.shape
    depth = w_mod.shape[0]
    n_past = cache_sb_k.shape[2]
    n_experts = w_router.shape[2]
    n_p, n_s = bsz * seq, dec_b * dec_t
    tq = _row_tile(seq, 256)
    past_chunk = _row_tile(n_past, 256)

    c_all = jnp.concatenate([c_prompt, c_sample], axis=0)
    c_rows = -(-c_all.shape[0] // 8) * 8
    c_all = jnp.pad(c_all, ((0, c_rows - c_all.shape[0]), (0, 0)))
    mod = _modulation(c_all, w_mod, b_mod)

    xp = x_prompt.reshape(n_p, d)
    xs = x_sample.reshape(n_s, d)
    stacked_p = stacked_s = None
    for l in range(depth):
        mod_p = mod[l, :bsz].reshape(bsz, 1, 6, d)
        mod_s = jnp.repeat(mod[l, bsz:bsz + dec_b].reshape(dec_b, 6, d), dec_t, axis=0)[None]
        mp = [mod_p[:, :, i] for i in range(6)]
        ms = [mod_s[:, :, i] for i in range(6)]
        vec = lambda a: a[l].reshape(1, -1)

        wqkv = w_in[l, :, :6 * WIDTH].astype(BF16)
        wf = jnp.pad(w_in[l, :, 6 * WIDTH:6 * WIDTH + N_HEADS], ((0, 0), (0, LANES - N_HEADS))).astype(BF16)
        wg = w_in[l, :, 6 * WIDTH + N_HEADS:].astype(BF16)
        bf = b_forget[l].reshape(1, N_HEADS)
        wsb, wfx, wo = w_sb_out[l].astype(BF16), w_fox_out[l].astype(BF16), w_out[l].astype(BF16)

        qkv_p, qvt_p, *stacked_p, sg_p = _in_projection(
            xp, vec(g_pre_mix), mp[0], mp[1], wqkv, wf, wg, bf, _row_tile(seq, 256), l, depth, stacked_p)
        lf_p = stacked_p[4][l]
        cum_p, key_cols_p = _forget_cumsum(jnp.swapaxes(lf_p.reshape(bsz, seq, N_HEADS), 1, 2), True)
        qkv_p3 = qkv_p.reshape(bsz, seq, 6 * WIDTH)
        osb_p = _sb_prompt(qkv_p3, qvt_p, tq)
        ofx_p = _fox_prompt(qkv_p3, qvt_p, cum_p, key_cols_p, _row_tile(seq, 512))
        xp = _mix_out(xp, osb_p.reshape(n_p, WIDTH), ofx_p.reshape(n_p, WIDTH), sg_p, mp[2], vec(g_post_mix),
                      wsb, wfx, wo, _row_tile(seq, 512))

        qkv_s, _, *stacked_s, sg_s = _in_projection(
            xs, vec(g_pre_mix), ms[0], ms[1], wqkv, wf, wg, bf, n_s, l, depth, stacked_s)
        lf_s = stacked_s[4][l]
        t_all = n_past + dec_t
        t_pad = -(-t_all // CUMSUM_CHUNK) * CUMSUM_CHUNK
        lf_all = jnp.concatenate([cache_fox_logf[l].astype(F32), lf_s.reshape(dec_b, dec_t, N_HEADS),
                                  jnp.zeros((dec_b, t_pad - t_all, N_HEADS), F32)], axis=1)
        (cum_s,) = _forget_cumsum(jnp.swapaxes(lf_all, 1, 2), False)
        fk_s, fq_s = _pair_layouts(cum_s, n_past, dec_t)
        cache = lambda a: a[l].reshape(dec_b, n_past, WIDTH)
        osb_s, ofx_s = _sample_attention(qkv_s.reshape(dec_b, dec_t, 6 * WIDTH), cache(cache_sb_k), cache(cache_sb_v),
                                         cache(cache_fox_k), cache(cache_fox_v), fq_s, fk_s, past_chunk)
        xs = _mix_out(xs, osb_s.reshape(n_s, WIDTH), ofx_s.reshape(n_s, WIDTH), sg_s, ms[2], vec(g_post_mix),
                      wsb, wfx, wo, n_s)

        if l % 2 == 0:
            wgu, wd = w_ffn_gate_up[l // 2].astype(BF16), w_ffn_down[l // 2].astype(BF16)
            tf = _row_tile(wd.shape[0], 1408)
            xp = _ffn(xp, vec(g_pre_ffn), mp[3], mp[4], mp[5], vec(g_post_ffn), wgu, wd, _row_tile(seq, 512), tf)
            xs = _ffn(xs, vec(g_pre_ffn), ms[3], ms[4], ms[5], vec(g_post_ffn), wgu, wd, n_s, tf)
        else:
            wr = jnp.pad(w_router[l // 2], ((0, 0), (0, LANES - n_experts)))
            wgu, wd = w_moe_gate_up[l // 2].astype(BF16), w_moe_down[l // 2].astype(BF16)
            tf = _row_tile(wd.shape[1], 896)
            xp = _moe(xp, vec(g_pre_ffn), mp[3], mp[4], mp[5], vec(g_post_ffn), wr, wgu, wd, n_experts,
                      _row_tile(seq, 512), 1024, tf, _row_tile(seq, 256))
            xs = _moe(xs, vec(g_pre_ffn), ms[3], ms[4], ms[5], vec(g_post_ffn), wr, wgu, wd, n_experts,
                      n_s, 128, tf, n_s)

    split_p = [a.reshape(depth, bsz, seq, *a.shape[2:]) for a in stacked_p]
    split_s = [a.reshape(depth, dec_b, dec_t, *a.shape[2:]) for a in stacked_s]
    return (xp.reshape(bsz, seq, d), xs.reshape(dec_b, dec_t, d), *split_p, *split_s)
```

```python
import functools

import jax
import jax.numpy as jnp
from jax import lax
from jax.experimental import pallas as pl
from jax.experimental.pallas import tpu as pltpu

F32 = jnp.float32
BF16 = jnp.bfloat16

HEAD_DIM = 64
N_HEADS = 8
LANES = 128
HEAD_PAIRS = N_HEADS * HEAD_DIM // LANES
WIDTH = N_HEADS * HEAD_DIM
TOP_K = 2
RMS_EPS = 1e-6
LOG2_E = 1.4426950408889634
SB_DEAD_LOG = -104.0
VMEM_LIMIT = 56 * 1024 * 1024


def _params(sem, vmem=VMEM_LIMIT):
    return pltpu.CompilerParams(dimension_semantics=sem, vmem_limit_bytes=vmem)


def _dot(a, b):
    return jnp.dot(a, b, preferred_element_type=F32)


def _dot_nt(a, b):
    return lax.dot_general(a, b, (((1,), (1,)), ((), ())), preferred_element_type=F32)


def _sigmoid(x):
    return 1.0 / (1.0 + jnp.exp(-x))


def _softplus(x):
    return jnp.maximum(x, 0.0) + jnp.log(1.0 + jnp.exp(-jnp.abs(x)))


def _rms(x, g):
    return x * lax.rsqrt(jnp.mean(x * x, axis=-1, keepdims=True) + RMS_EPS) * g


def _mod_kernel(c_ref, w_ref, b_ref, o_ref):
    c = c_ref[...]
    s = c * _sigmoid(c)
    o_ref[0] = jnp.dot(s, w_ref[0], precision=lax.Precision.HIGHEST, preferred_element_type=F32) + b_ref[0]


def _modulation(c_all, w_mod, b_mod):
    depth, d, d6 = w_mod.shape
    rows = c_all.shape[0]
    tn = 1024
    return pl.pallas_call(
        _mod_kernel,
        grid=(depth, d6 // tn),
        in_specs=[pl.BlockSpec((rows, d), lambda l, j: (0, 0)),
                  pl.BlockSpec((1, d, tn), lambda l, j: (l, 0, j)),
                  pl.BlockSpec((1, 1, tn), lambda l, j: (l, 0, j))],
        out_specs=pl.BlockSpec((1, rows, tn), lambda l, j: (l, 0, j)),
        out_shape=jax.ShapeDtypeStruct((depth, rows, d6), F32),
        compiler_params=_params(("arbitrary", "arbitrary")),
        name="modulation",
    )(c_all, w_mod, b_mod.reshape(depth, 1, d6))


def _mod_spec(mod, tm, tiles_per_group):
    _, r, d = mod.shape
    return pl.BlockSpec((1, r, d), lambda i, *_: (i // tiles_per_group, 0, 0))


TRANSPOSED_SECTIONS = (0, 2, 3, 5)


def _inproj_kernel(x_ref, g_ref, shift_ref, scale_ref, wqkv_ref, wf_ref, wg_ref, bf_ref, *refs):
    qkv_ref, qvt_ref, ksb_ref, vsb_ref, kfx_ref, vfx_ref, lf_ref, sg_ref = refs[-8:]
    tm = x_ref.shape[0]
    h = _rms(x_ref[...], g_ref[...]) * (1.0 + scale_ref[0]) + shift_ref[0]
    hb = h.astype(BF16)
    f32_outs = {1: ksb_ref, 2: vsb_ref, 4: kfx_ref, 5: vfx_ref}
    for c in range(6):
        cols = slice(c * WIDTH, (c + 1) * WIDTH)
        acc = _dot(hb, wqkv_ref[:, cols])
        if c in f32_outs:
            f32_outs[c][0] = acc.reshape(tm, N_HEADS, HEAD_DIM)
        if c in (0, 3):
            acc = acc * (HEAD_DIM ** -0.5 * (LOG2_E if c == 3 else 1.0))
        qkv_ref[:, cols] = acc.astype(BF16)
        if c in TRANSPOSED_SECTIONS:
            t = TRANSPOSED_SECTIONS.index(c)
            qvt_ref[t * WIDTH:(t + 1) * WIDTH, :] = acc.T.astype(BF16)
    f = _dot(hb, wf_ref[...])[:, :N_HEADS] + bf_ref[...]
    lf_ref[0] = -_softplus(-f)
    for c in range(wg_ref.shape[1] // WIDTH):
        cols = slice(c * WIDTH, (c + 1) * WIDTH)
        sg_ref[:, cols] = _sigmoid(_dot(hb, wg_ref[:, cols])).astype(BF16)


def _in_projection(x, g_pre, shift, scale, wqkv, wf, wg, b_forget, tm, layer, depth, stacked):
    n, d = x.shape
    tiles_per_group = (n // tm) // shift.shape[0]
    const = lambda i: (0, 0)
    row = lambda i: (i, 0)
    heads_spec = pl.BlockSpec((1, tm, N_HEADS, HEAD_DIM), lambda i: (layer, i, 0, 0))
    heads_shape = jax.ShapeDtypeStruct((depth, n, N_HEADS, HEAD_DIM), F32)
    n_in = 8
    stacked = () if stacked is None else tuple(stacked)
    return pl.pallas_call(
        _inproj_kernel,
        grid=(n // tm,),
        in_specs=[pl.BlockSpec((tm, d), row),
                  pl.BlockSpec((1, d), const),
                  _mod_spec(shift, tm, tiles_per_group),
                  _mod_spec(scale, tm, tiles_per_group),
                  pl.BlockSpec(wqkv.shape, const),
                  pl.BlockSpec(wf.shape, const),
                  pl.BlockSpec(wg.shape, const),
                  pl.BlockSpec((1, N_HEADS), const)] + [pl.BlockSpec(memory_space=pl.ANY)] * len(stacked),
        out_specs=[pl.BlockSpec((tm, 6 * WIDTH), row), pl.BlockSpec((len(TRANSPOSED_SECTIONS) * WIDTH, tm), lambda i: (0, i))]
                  + [heads_spec] * 4
                  + [pl.BlockSpec((1, tm, N_HEADS), lambda i: (layer, i, 0)), pl.BlockSpec((tm, wg.shape[1]), row)],
        out_shape=[jax.ShapeDtypeStruct((n, 6 * WIDTH), BF16),
                   jax.ShapeDtypeStruct((len(TRANSPOSED_SECTIONS) * WIDTH, n), BF16)] + [heads_shape] * 4
                  + [jax.ShapeDtypeStruct((depth, n, N_HEADS), F32), jax.ShapeDtypeStruct((n, wg.shape[1]), BF16)],
        input_output_aliases={n_in + k: 2 + k for k in range(len(stacked))},
        compiler_params=_params(("arbitrary",)),
        name="in_projection",
    )(x, g_pre, shift, scale, wqkv, wf, wg, b_forget, *stacked)


CUMSUM_CHUNK = 256


FORGET_SPLIT = 3


def _cumsum_kernel(x_ref, o_ref, kcol_ref=None):
    t = x_ref.shape[2]
    r = lax.broadcasted_iota(jnp.int32, (CUMSUM_CHUNK, CUMSUM_CHUNK), 0)
    c = lax.broadcasted_iota(jnp.int32, (CUMSUM_CHUNK, CUMSUM_CHUNK), 1)
    upper = jnp.where(r <= c, 1.0, 0.0).astype(F32)
    row = lax.broadcasted_iota(jnp.int32, (LANES, LANES), 0)
    lane = lax.broadcasted_iota(jnp.int32, (LANES, LANES), 1)
    head, term = row % N_HEADS, row // N_HEADS
    placed = jnp.logical_and(row < FORGET_SPLIT * N_HEADS, lane == (head % 2) * FORGET_SPLIT + term)
    pad_rows = jnp.zeros((LANES - FORGET_SPLIT * N_HEADS, CUMSUM_CHUNK), F32)

    def step(i, carry):
        start = pl.multiple_of(i * CUMSUM_CHUNK, CUMSUM_CHUNK)
        seg = x_ref[0, :, pl.ds(start, CUMSUM_CHUNK)]
        cs = jnp.dot(seg, upper, precision=lax.Precision.HIGHEST, preferred_element_type=F32) + carry
        f_log2 = cs * LOG2_E
        o_ref[0, :, pl.ds(start, CUMSUM_CHUNK)] = f_log2
        if kcol_ref is None:
            return cs[:, CUMSUM_CHUNK - 1:CUMSUM_CHUNK]
        rest = -f_log2
        terms = []
        for _ in range(FORGET_SPLIT):
            part = pltpu.bitcast(pltpu.bitcast(rest, jnp.uint32) & jnp.uint32(0xFFFF0000), F32)
            terms.append(part)
            rest = rest - part
        terms_t = jnp.concatenate(terms + [pad_rows], axis=0).T.astype(BF16)
        for p in range(HEAD_PAIRS):
            select = jnp.where(jnp.logical_and(placed, head // 2 == p), 1.0, 0.0).astype(BF16)
            kcol_ref[0, p, pl.ds(start, CUMSUM_CHUNK), :] = _dot(terms_t, select).astype(BF16)
        return cs[:, CUMSUM_CHUNK - 1:CUMSUM_CHUNK]

    lax.fori_loop(0, t // CUMSUM_CHUNK, step, jnp.zeros((x_ref.shape[1], 1), F32))


def _forget_cumsum(x, key_columns):
    b, r, t = x.shape
    out_specs = [pl.BlockSpec((1, r, t), lambda i: (i, 0, 0))]
    out_shape = [jax.ShapeDtypeStruct((b, r, t), F32)]
    if key_columns:
        out_specs.append(pl.BlockSpec((1, HEAD_PAIRS, t, LANES), lambda i: (i, 0, 0, 0)))
        out_shape.append(jax.ShapeDtypeStruct((b, HEAD_PAIRS, t, LANES), BF16))
    return pl.pallas_call(
        _cumsum_kernel,
        grid=(b,),
        in_specs=[pl.BlockSpec((1, r, t), lambda i: (i, 0, 0))],
        out_specs=out_specs,
        out_shape=out_shape,
        compiler_params=_params(("arbitrary",)),
        name="forget_cumsum",
    )(x)


def _pair_layouts(cum_t, t_query_start, t_query):
    b, _, tp = cum_t.shape
    f_keys = cum_t.reshape(b, HEAD_PAIRS, 2, tp)
    f_query = jnp.swapaxes(f_keys[:, :, :, t_query_start:t_query_start + t_query], 2, 3)
    return f_keys, f_query


def _head_masks():
    lane = lax.broadcasted_iota(jnp.int32, (1, LANES), 1)
    return lane < HEAD_DIM


def _split_heads(q2, first_head):
    zero = jnp.zeros_like(q2)
    return jnp.where(first_head, q2, zero), jnp.where(first_head, zero, q2)


def _strict_lower_neg(n):
    r = lax.broadcasted_iota(jnp.int32, (n, n), 0)
    c = lax.broadcasted_iota(jnp.int32, (n, n), 1)
    return jnp.where(r > c, -1.0, 0.0).astype(BF16)


def _sb_block(qh, k2, v2, neg_tri, visible, acc_ref, r_ref, hd):
    z = _dot_nt(qh, k2)
    sp = _softplus(z)
    if visible is not None:
        sp = jnp.where(visible, sp, 0.0)
    later = _dot(sp.astype(BF16), neg_tri)
    p = jnp.exp(z - sp + later)
    if visible is not None:
        p = jnp.where(visible, p, 0.0)
    r = r_ref[hd]
    acc_ref[hd] += jnp.exp(r) * _dot(p.astype(BF16), v2)
    r_ref[hd] = r + later[:, 0:1] - sp[:, 0:1]


def _fox_block(qh, k2, v2_ones, fq, fk, allowed, acc_ref, m_ref, hd):
    s = _dot_nt(qh, k2) + fq - fk
    if allowed is not None:
        s = jnp.where(allowed, s, -jnp.inf)
    m_old = m_ref[hd]
    m_new = jnp.maximum(m_old, jnp.max(s, axis=-1, keepdims=True))
    p = jnp.exp2(s - m_new)
    acc_ref[hd] = jnp.exp2(m_old - m_new) * acc_ref[hd] + _dot(p.astype(BF16), v2_ones)
    m_ref[hd] = m_new


def _fox_finish(acc_ref, first_head):
    a, b = acc_ref[0], acc_ref[1]
    num = jnp.where(first_head, a, b)
    den = jnp.where(first_head, pltpu.roll(a, HEAD_DIM, 1), pltpu.roll(b, HEAD_DIM, 1))
    return num / den


def _with_ones(v2, first_head):
    one = jnp.ones_like(v2)
    return jnp.where(first_head, v2, one), jnp.where(first_head, one, v2)


def _sb_prompt_kernel(qt_ref, k_ref, vt_ref, o_ref, w_ref, acc_ref, r_ref, tri_ref, *, tq):
    i = pl.program_id(2)
    sub = lax.broadcasted_iota(jnp.int32, (LANES, tq), 0)
    first_rows = sub < HEAD_DIM
    qt = qt_ref[...]
    zero = jnp.zeros_like(qt)
    w_ref[0] = jnp.where(first_rows, qt, zero)
    w_ref[1] = jnp.where(first_rows, zero, qt)
    acc_ref[...] = jnp.zeros_like(acc_ref)
    r_ref[...] = jnp.zeros_like(r_ref)

    @pl.when(jnp.logical_and(jnp.logical_and(pl.program_id(0) == 0, pl.program_id(1) == 0), i == 0))
    def _():
        s_idx = lax.broadcasted_iota(jnp.int32, tri_ref.shape, 0)
        j_idx = lax.broadcasted_iota(jnp.int32, tri_ref.shape, 1)
        tri_ref[...] = jnp.where(j_idx > s_idx, -1.0, 0.0).astype(BF16)

    def block(start, size, vis):
        keys = k_ref[0, pl.ds(start, size), :]
        vt = vt_ref[:, pl.ds(start, size)]
        neg_tri = tri_ref[:size, :size]
        for hd in range(2):
            z = _dot(keys, w_ref[hd])
            sp = _softplus(z)
            if vis is not None:
                sp = jnp.where(vis, sp, 0.0)
            later = _dot(neg_tri, sp.astype(BF16))
            p = jnp.exp(z - sp + later)
            if vis is not None:
                p = jnp.where(vis, p, 0.0)
            r = r_ref[hd]
            acc_ref[hd] += jnp.exp(r) * _dot(vt, p.astype(BF16))
            r_ref[hd] = r + later[0:1, :] - sp[0:1, :]

    def live():
        return jnp.max(jnp.maximum(r_ref[0], r_ref[1]))

    first = jnp.maximum(i - 1, 0)
    key_idx = lax.broadcasted_iota(jnp.int32, (2 * tq, tq), 0) + first * tq
    query_idx = lax.broadcasted_iota(jnp.int32, (2 * tq, tq), 1) + i * tq
    block(pl.multiple_of(first * tq, tq), 2 * tq, key_idx < query_idx)

    def cond(carry):
        j, r_max = carry
        return jnp.logical_and(j >= 0, r_max > SB_DEAD_LOG)

    def body(carry):
        j, _ = carry
        block(pl.multiple_of(j * tq, tq), tq, None)
        return j - 1, live()

    lax.while_loop(cond, body, (i - 2, live()))
    o_ref[0] = jnp.where(first_rows, acc_ref[0], acc_ref[1]).T.astype(o_ref.dtype)


def _sb_prompt(qkv, qv_t, tq):
    b, t, _ = qkv.shape
    nq = t // tq
    kernel = functools.partial(_sb_prompt_kernel, tq=tq)
    return pl.pallas_call(
        kernel,
        grid=(b, HEAD_PAIRS, nq),
        in_specs=[pl.BlockSpec((LANES, tq), lambda b_, p, i: (p, b_ * nq + i)),
                  pl.BlockSpec((1, t, LANES), lambda b_, p, i: (b_, 0, HEAD_PAIRS + p)),
                  pl.BlockSpec((LANES, t), lambda b_, p, i: (HEAD_PAIRS + p, b_))],
        out_specs=pl.BlockSpec((1, tq, LANES), lambda b_, p, i: (b_, i, p)),
        out_shape=jax.ShapeDtypeStruct((b, t, WIDTH), BF16),
        scratch_shapes=[pltpu.VMEM((2, LANES, tq), BF16), pltpu.VMEM((2, LANES, tq), F32),
                        pltpu.VMEM((2, 1, tq), F32), pltpu.VMEM((2 * tq, 2 * tq), BF16)],
        compiler_params=_params(("arbitrary", "arbitrary", "arbitrary")),
        name="sb_attention_prompt",
    )(qv_t, qkv, qv_t)


FOX_DEAD_LOG2 = -150.0
FOX_BOUNDED_MAX = 40.0


def _fox_prompt_kernel(kmax_ref, fend_ref, qt_ref, k_ref, kf_ref, vt_ref, fq_ref, o_ref,
                       w_ref, acc_ref, m_ref, zb_ref, *, tq):
    bi = pl.program_id(0)
    pi = pl.program_id(1)
    i = pl.program_id(2)
    sub = lax.broadcasted_iota(jnp.int32, (LANES, tq), 0)
    first_rows = sub < HEAD_DIM
    qt = qt_ref[...]
    zero = jnp.zeros_like(qt)
    for hd in range(2):
        lo = hd * FORGET_SPLIT
        select = jnp.where(jnp.logical_and(sub >= lo, sub < lo + FORGET_SPLIT), 1.0, 0.0).astype(BF16)
        q_head = jnp.where(first_rows, qt, zero) if hd == 0 else jnp.where(first_rows, zero, qt)
        w_ref[hd, :LANES, :] = q_head
        w_ref[hd, LANES:, :] = select
        q_f32 = q_head.astype(F32)
        q_norm = jnp.sqrt(jnp.sum(q_f32 * q_f32, axis=0, keepdims=True))
        zb_ref[hd] = q_norm * (kmax_ref[bi, 2 * pi + hd] * 1.01)
    acc_ref[...] = jnp.zeros_like(acc_ref)
    m_ref[...] = jnp.full_like(m_ref, -jnp.inf)
    key_idx = lax.broadcasted_iota(jnp.int32, (tq, tq), 0)
    query_idx = lax.broadcasted_iota(jnp.int32, (tq, tq), 1)
    allowed = key_idx <= query_idx
    den_row = (HEAD_DIM, 0)

    def operands(j):
        start = pl.multiple_of(j * tq, tq)
        keys = jnp.concatenate([k_ref[0, pl.ds(start, tq), :], kf_ref[0, 0, pl.ds(start, tq), :]], axis=1)
        vt = vt_ref[:, pl.ds(start, tq)]
        one = jnp.ones_like(vt)
        first_v = lax.broadcasted_iota(jnp.int32, vt.shape, 0) < HEAD_DIM
        return keys, (jnp.where(first_v, vt, one), jnp.where(first_v, one, vt))

    def block_bounded(j, mask):
        keys, vts = operands(j)
        for hd in range(2):
            u = _dot(keys, w_ref[hd])
            if mask is not None:
                u = jnp.where(mask, u, -jnp.inf)
            p = jnp.exp2(u + (fq_ref[0, 0, hd:hd + 1, :] - zb_ref[hd]))
            acc_ref[hd] += _dot(vts[hd], p.astype(BF16))

    def alive_bounded(j):
        jc = jnp.maximum(j, 0)
        reach = []
        for hd in range(2):
            den = acc_ref[hd, den_row[hd]:den_row[hd] + 1, :]
            gap = jnp.max(fq_ref[0, 0, hd:hd + 1, :] - jnp.log2(den))
            reach.append(gap - fend_ref[bi, pi, hd, jc])
        return jnp.maximum(reach[0], reach[1]) + tq.bit_length() > FOX_DEAD_LOG2

    def block_exact(j, mask):
        keys, vts = operands(j)
        for hd in range(2):
            u = _dot(keys, w_ref[hd])
            if mask is not None:
                u = jnp.where(mask, u, -jnp.inf)
            fq = fq_ref[0, 0, hd:hd + 1, :]
            m_old = m_ref[hd]
            m_new = jnp.maximum(m_old, jnp.max(u, axis=0, keepdims=True) + fq)
            p = jnp.exp2(u + (fq - m_new))
            acc_ref[hd] = jnp.exp2(m_old - m_new) * acc_ref[hd] + _dot(vts[hd], p.astype(BF16))
            m_ref[hd] = m_new

    def alive_exact(j):
        jc = jnp.maximum(j, 0)
        reach = [jnp.max(zb_ref[hd] + fq_ref[0, 0, hd:hd + 1, :] - m_ref[hd]) - fend_ref[bi, pi, hd, jc]
                 for hd in range(2)]
        return jnp.maximum(reach[0], reach[1]) > FOX_DEAD_LOG2

    def sweep(block, alive):
        block(i, allowed)

        def cond(carry):
            j, go = carry
            return jnp.logical_and(j >= 0, go)

        def body(carry):
            j, _ = carry
            block(j, None)
            return j - 1, alive(j - 1)

        lax.while_loop(cond, body, (i - 1, alive(i - 1)))

    bounded = jnp.max(jnp.maximum(zb_ref[0], zb_ref[1])) <= FOX_BOUNDED_MAX

    @pl.when(bounded)
    def _():
        sweep(block_bounded, alive_bounded)

    @pl.when(jnp.logical_not(bounded))
    def _():
        sweep(block_exact, alive_exact)

    a, b = acc_ref[0], acc_ref[1]
    out_t = jnp.where(first_rows, a / a[HEAD_DIM:HEAD_DIM + 1, :], b / b[0:1, :])
    o_ref[0] = out_t.T.astype(o_ref.dtype)


def _fox_prompt(qkv, qv_t, cum_t, key_cols, tq):
    b, t, _ = qkv.shape
    nq = t // tq
    f_rows = cum_t.reshape(b, HEAD_PAIRS, 2, t)
    f_block_end = f_rows[:, :, :, tq - 1::tq]
    k_fx = qkv[:, :, 4 * WIDTH:5 * WIDTH].astype(F32).reshape(b, t, N_HEADS, HEAD_DIM)
    k_max = jnp.sqrt(jnp.max(jnp.sum(k_fx * k_fx, axis=-1), axis=1))
    smem = pl.BlockSpec(memory_space=pltpu.SMEM)
    kernel = functools.partial(_fox_prompt_kernel, tq=tq)
    return pl.pallas_call(
        kernel,
        grid=(b, HEAD_PAIRS, nq),
        in_specs=[smem, smem,
                  pl.BlockSpec((LANES, tq), lambda b_, p, i: (2 * HEAD_PAIRS + p, b_ * nq + i)),
                  pl.BlockSpec((1, t, LANES), lambda b_, p, i: (b_, 0, 4 * HEAD_PAIRS + p)),
                  pl.BlockSpec((1, 1, t, LANES), lambda b_, p, i: (b_, p, 0, 0)),
                  pl.BlockSpec((LANES, t), lambda b_, p, i: (3 * HEAD_PAIRS + p, b_)),
                  pl.BlockSpec((1, 1, 2, tq), lambda b_, p, i: (b_, p, 0, i))],
        out_specs=pl.BlockSpec((1, tq, LANES), lambda b_, p, i: (b_, i, p)),
        out_shape=jax.ShapeDtypeStruct((b, t, WIDTH), BF16),
        scratch_shapes=[pltpu.VMEM((2, 2 * LANES, tq), BF16), pltpu.VMEM((2, LANES, tq), F32),
                        pltpu.VMEM((2, 1, tq), F32), pltpu.VMEM((2, 1, tq), F32)],
        compiler_params=_params(("arbitrary", "arbitrary", "arbitrary")),
        name="fox_attention_prompt",
    )(k_max, f_block_end, qv_t, qkv, key_cols, qv_t, f_rows)


def _sample_attn_kernel(qsb_ref, ksb_ref, vsb_ref, qfx_ref, kfx_ref, vfx_ref,
                        csk_ref, csv_ref, cfk_ref, cfv_ref, fq_ref, fk_ref,
                        osb_ref, ofx_ref, acc_ref, st_ref, *, tk):
    t_new = qsb_ref.shape[1]
    n_past = csk_ref.shape[1]
    n_blocks = n_past // tk
    first_head = _head_masks()
    row = lax.broadcasted_iota(jnp.int32, (t_new, t_new), 0)
    col = lax.broadcasted_iota(jnp.int32, (t_new, t_new), 1)

    qs = _split_heads(qsb_ref[0], first_head)
    acc_ref[...] = jnp.zeros_like(acc_ref)
    st_ref[...] = jnp.zeros_like(st_ref)
    tri_new = _strict_lower_neg(t_new)
    tri_past = _strict_lower_neg(tk)
    for hd in range(2):
        _sb_block(qs[hd], ksb_ref[0], vsb_ref[0], tri_new, col < row, acc_ref, st_ref, hd)

    def live():
        return jnp.max(jnp.maximum(st_ref[0], st_ref[1]))

    def sb_cond(carry):
        j, r_max = carry
        return jnp.logical_and(j >= 0, r_max > SB_DEAD_LOG)

    def sb_body(carry):
        j, _ = carry
        start = pl.multiple_of(j * tk, tk)
        k2 = csk_ref[0, pl.ds(start, tk), :].astype(BF16)
        v2 = csv_ref[0, pl.ds(start, tk), :].astype(BF16)
        for hd in range(2):
            _sb_block(qs[hd], k2, v2, tri_past, None, acc_ref, st_ref, hd)
        return j - 1, live()

    lax.while_loop(sb_cond, sb_body, (n_blocks - 1, live()))
    osb_ref[0] = jnp.where(first_head, acc_ref[0], acc_ref[1]).astype(osb_ref.dtype)

    qs = _split_heads(qfx_ref[0], first_head)
    acc_ref[...] = jnp.zeros_like(acc_ref)
    st_ref[...] = jnp.full_like(st_ref, -jnp.inf)
    fq = fq_ref[0, 0]
    vs = _with_ones(vfx_ref[0], first_head)
    for hd in range(2):
        fk = fk_ref[0, 0, hd:hd + 1, n_past:n_past + t_new]
        _fox_block(qs[hd], kfx_ref[0], vs[hd], fq[:, hd:hd + 1], fk, col <= row, acc_ref, st_ref, hd)

    k2 = cfk_ref[0].astype(BF16)
    vs_ = _with_ones(cfv_ref[0].astype(BF16), first_head)
    for hd in range(2):
        fk = fk_ref[0, 0, hd:hd + 1, 0:n_past]
        _fox_block(qs[hd], k2, vs_[hd], fq[:, hd:hd + 1], fk, None, acc_ref, st_ref, hd)
    ofx_ref[0] = _fox_finish(acc_ref, first_head).astype(ofx_ref.dtype)


def _sample_attention(qkv, cache_sb_k, cache_sb_v, cache_fox_k, cache_fox_v, f_query, f_keys, tk):
    b, t_new, _ = qkv.shape
    n_past = cache_sb_k.shape[1]
    tp = f_keys.shape[3]
    new = lambda c: pl.BlockSpec((1, t_new, LANES), lambda b_, p: (b_, 0, c * HEAD_PAIRS + p))
    past = pl.BlockSpec((1, n_past, LANES), lambda b_, p: (b_, 0, p))
    out = pl.BlockSpec((1, t_new, LANES), lambda b_, p: (b_, 0, p))
    kernel = functools.partial(_sample_attn_kernel, tk=tk)
    return pl.pallas_call(
        kernel,
        grid=(b, HEAD_PAIRS),
        in_specs=[new(0), new(1), new(2), new(3), new(4), new(5), past, past, past, past,
                  pl.BlockSpec((1, 1, t_new, 2), lambda b_, p: (b_, p, 0, 0)),
                  pl.BlockSpec((1, 1, 2, tp), lambda b_, p: (b_, p, 0, 0))],
        out_specs=[out, out],
        out_shape=[jax.ShapeDtypeStruct((b, t_new, WIDTH), BF16)] * 2,
        scratch_shapes=[pltpu.VMEM((2, t_new, LANES), F32), pltpu.VMEM((2, t_new, 1), F32)],
        compiler_params=_params(("arbitrary", "arbitrary")),
        name="attention_sample",
    )(qkv, qkv, qkv, qkv, qkv, qkv, cache_sb_k, cache_sb_v, cache_fox_k, cache_fox_v, f_query, f_keys)


def _mix_out_kernel(x_ref, osb_ref, ofx_ref, sg_ref, gate_ref, g_ref, wsb_ref, wfx_ref, wo_ref, o_ref):
    d = x_ref.shape[1]
    merged = (sg_ref[:, :d].astype(F32) * _dot(osb_ref[...], wsb_ref[...])
              + sg_ref[:, d:].astype(F32) * _dot(ofx_ref[...], wfx_ref[...]))
    y = _dot(merged.astype(BF16), wo_ref[...])
    o_ref[...] = x_ref[...] + gate_ref[0] * _rms(y, g_ref[...])


def _mix_out(x, o_sb, o_fx, sg, gate, g_post, w_sb_out, w_fox_out, w_out, tm):
    n, d = x.shape
    tiles_per_group = (n // tm) // gate.shape[0]
    const = lambda i: (0, 0)
    row = lambda i: (i, 0)
    return pl.pallas_call(
        _mix_out_kernel,
        grid=(n // tm,),
        in_specs=[pl.BlockSpec((tm, d), row), pl.BlockSpec((tm, WIDTH), row), pl.BlockSpec((tm, WIDTH), row),
                  pl.BlockSpec((tm, 2 * d), row), _mod_spec(gate, tm, tiles_per_group),
                  pl.BlockSpec((1, d), const), pl.BlockSpec(w_sb_out.shape, const),
                  pl.BlockSpec(w_fox_out.shape, const), pl.BlockSpec(w_out.shape, const)],
        out_specs=pl.BlockSpec((tm, d), row),
        out_shape=jax.ShapeDtypeStruct((n, d), F32),
        compiler_params=_params(("arbitrary",)),
        name="mix_out",
    )(x, o_sb, o_fx, sg, gate, g_post, w_sb_out, w_fox_out, w_out)


def _swiglu_chunk(hb, wa, wb, wd):
    a = _dot(hb, wa)
    b = _dot(hb, wb)
    return a * _sigmoid(a) * b, wd


def _ffn_kernel(x_ref, gpre_ref, shift_ref, scale_ref, gate_ref, gpost_ref, wa_ref, wb_ref, wd_ref,
                o_ref, h_ref, acc_ref):
    f = pl.program_id(1)

    @pl.when(f == 0)
    def _():
        h = _rms(x_ref[...], gpre_ref[...]) * (1.0 + scale_ref[0]) + shift_ref[0]
        h_ref[...] = h.astype(BF16)
        acc_ref[...] = jnp.zeros_like(acc_ref)

    hb = h_ref[...]
    a = _dot(hb, wa_ref[...])
    b = _dot(hb, wb_ref[...])
    act = a * _sigmoid(a) * b
    acc_ref[...] += _dot(act.astype(BF16), wd_ref[...])

    @pl.when(f == pl.num_programs(1) - 1)
    def _():
        o_ref[...] = x_ref[...] + gate_ref[0] * _rms(acc_ref[...], gpost_ref[...])


def _ffn(x, g_pre, shift, scale, gate, g_post, w_gate_up, w_down, tm, tf):
    n, d = x.shape
    d_ff = w_down.shape[0]
    nf = d_ff // tf
    tiles_per_group = (n // tm) // shift.shape[0]
    const = lambda i, f: (0, 0)
    row = lambda i, f: (i, 0)
    return pl.pallas_call(
        _ffn_kernel,
        grid=(n // tm, nf),
        in_specs=[pl.BlockSpec((tm, d), row), pl.BlockSpec((1, d), const),
                  _mod_spec(shift, tm, tiles_per_group), _mod_spec(scale, tm, tiles_per_group),
                  _mod_spec(gate, tm, tiles_per_group), pl.BlockSpec((1, d), const),
                  pl.BlockSpec((d, tf), lambda i, f: (0, f)),
                  pl.BlockSpec((d, tf), lambda i, f: (0, nf + f)),
                  pl.BlockSpec((tf, d), lambda i, f: (f, 0))],
        out_specs=pl.BlockSpec((tm, d), row),
        out_shape=jax.ShapeDtypeStruct((n, d), F32),
        scratch_shapes=[pltpu.VMEM((tm, d), BF16), pltpu.VMEM((tm, d), F32)],
        compiler_params=_params(("arbitrary", "arbitrary")),
        name="ffn_dense",
    )(x, g_pre, shift, scale, gate, g_post, w_gate_up, w_gate_up, w_down)


def _route_kernel(x_ref, gpre_ref, shift_ref, scale_ref, wr_ref, h_ref, idx_ref, w_ref, *, n_experts):
    h = _rms(x_ref[...], gpre_ref[...]) * (1.0 + scale_ref[0]) + shift_ref[0]
    h_ref[...] = h
    logits = jnp.dot(h, wr_ref[...], precision=lax.Precision.HIGHEST, preferred_element_type=F32)[:, :n_experts]
    idx = lax.broadcasted_iota(jnp.int32, logits.shape, 1)
    m1 = jnp.max(logits, axis=-1, keepdims=True)
    i1 = jnp.min(jnp.where(logits == m1, idx, n_experts), axis=-1, keepdims=True)
    rest = jnp.where(idx == i1, -jnp.inf, logits)
    m2 = jnp.max(rest, axis=-1, keepdims=True)
    i2 = jnp.min(jnp.where(rest == m2, idx, n_experts), axis=-1, keepdims=True)
    e2 = jnp.exp(m2 - m1)
    slot = lax.broadcasted_iota(jnp.int32, idx_ref.shape, 1)
    idx_ref[...] = jnp.where(slot == 0, i1, i2)
    w_ref[...] = jnp.where(slot == 0, 1.0 / (1.0 + e2), e2 / (1.0 + e2))


def _route(x, g_pre, shift, scale, w_router, n_experts, tm):
    n, d = x.shape
    tiles_per_group = (n // tm) // shift.shape[0]
    const = lambda i: (0, 0)
    row = lambda i: (i, 0)
    return pl.pallas_call(
        functools.partial(_route_kernel, n_experts=n_experts),
        grid=(n // tm,),
        in_specs=[pl.BlockSpec((tm, d), row), pl.BlockSpec((1, d), const),
                  _mod_spec(shift, tm, tiles_per_group), _mod_spec(scale, tm, tiles_per_group),
                  pl.BlockSpec(w_router.shape, const)],
        out_specs=[pl.BlockSpec((tm, d), row), pl.BlockSpec((tm, TOP_K), row), pl.BlockSpec((tm, TOP_K), row)],
        out_shape=[jax.ShapeDtypeStruct((n, d), F32), jax.ShapeDtypeStruct((n, TOP_K), jnp.int32),
                   jax.ShapeDtypeStruct((n, TOP_K), F32)],
        compiler_params=_params(("arbitrary",)),
        name="moe_route",
    )(x, g_pre, shift, scale, w_router)


def _dispatch_tables(top_idx, n_experts, tm):
    n = top_idx.shape[0]
    n_assign = n * TOP_K
    expert = top_idx.reshape(n_assign)
    onehot = (expert[:, None] == jnp.arange(n_experts, dtype=jnp.int32)[None, :]).astype(jnp.int32)
    rank = jnp.sum((jnp.cumsum(onehot, axis=0) - onehot) * onehot, axis=1)
    counts = jnp.sum(onehot, axis=0)
    padded = (counts + tm - 1) // tm * tm
    group_end = jnp.cumsum(padded)
    pos = (group_end - padded)[expert] + rank
    p_rows = (-(-n_assign // tm) + n_experts) * tm
    src_token = jnp.zeros((p_rows,), jnp.int32).at[pos].set(jnp.arange(n_assign, dtype=jnp.int32) // TOP_K)
    tile_start = jnp.arange(p_rows // tm, dtype=jnp.int32) * tm
    tile_expert = jnp.minimum(jnp.sum(tile_start[:, None] >= group_end[None, :], axis=1), n_experts - 1)
    n_used = (group_end[-1] // tm).reshape(1)
    return src_token, tile_expert.astype(jnp.int32), n_used.astype(jnp.int32), pos.astype(jnp.int32)


def _gather_rows(src_hbm, dst, sem, index_of_row, n_rows):
    def issue(r, carry):
        pltpu.make_async_copy(src_hbm.at[pl.ds(index_of_row(r), 1), :], dst.at[pl.ds(r, 1), :], sem).start()
        return carry

    lax.fori_loop(0, n_rows, issue, 0, unroll=8)


def _wait_rows(src_hbm, dst, sem):
    pltpu.make_async_copy(src_hbm.at[pl.ds(0, dst.shape[0]), :], dst, sem).wait()


def _experts_kernel(src_ref, expert_ref, used_ref, h_hbm, wa_ref, wb_ref, wd_ref, o_ref,
                    rows_ref, xb_ref, acc_ref, sem, *, tm):
    i = pl.program_id(0)
    f = pl.program_id(1)
    nf = pl.num_programs(1)
    n_used = used_ref[0]
    slot = lax.rem(i, 2)
    active = i < n_used
    rows_per_step = tm // nf

    @pl.when(jnp.logical_and(f == 0, i == 0))
    def _():
        _gather_rows(h_hbm, rows_ref.at[0], sem.at[0], lambda r: src_ref[r], tm)

    @pl.when(jnp.logical_and(f == 0, i <= n_used))
    def _():
        _wait_rows(h_hbm, rows_ref.at[slot], sem.at[slot])

    @pl.when(jnp.logical_and(f == 0, active))
    def _():
        xb_ref[...] = rows_ref[slot].astype(BF16)
        acc_ref[...] = jnp.zeros_like(acc_ref)

    @pl.when(active)
    def _():
        base = (i + 1) * tm + f * rows_per_step
        nxt = rows_ref.at[1 - slot]
        for r in range(rows_per_step):
            row = f * rows_per_step + r
            pltpu.make_async_copy(h_hbm.at[pl.ds(src_ref[base + r], 1), :], nxt.at[pl.ds(row, 1), :],
                                  sem.at[1 - slot]).start()
        xb = xb_ref[...]
        a = _dot(xb, wa_ref[0])
        b = _dot(xb, wb_ref[0])
        act = a * _sigmoid(a) * b
        acc_ref[...] += _dot(act.astype(BF16), wd_ref[0])

    last = f == pl.num_programs(1) - 1

    @pl.when(jnp.logical_and(last, active))
    def _():
        o_ref[...] = acc_ref[...]

    @pl.when(jnp.logical_and(last, jnp.logical_not(active)))
    def _():
        o_ref[...] = jnp.zeros_like(o_ref)


def _experts(h, src_token, tile_expert, n_used, w_gate_up, w_down, tm, tf):
    p_rows = src_token.shape[0]
    d = h.shape[1]
    nf = w_down.shape[1] // tf
    ff = lambda i, f, used: jnp.where(i < used[0], f, nf - 1)
    grid_spec = pltpu.PrefetchScalarGridSpec(
        num_scalar_prefetch=3,
        grid=(p_rows // tm, nf),
        in_specs=[pl.BlockSpec(memory_space=pl.ANY),
                  pl.BlockSpec((1, d, tf), lambda i, f, src, ex, used: (ex[i], 0, ff(i, f, used))),
                  pl.BlockSpec((1, d, tf), lambda i, f, src, ex, used: (ex[i], 0, nf + ff(i, f, used))),
                  pl.BlockSpec((1, tf, d), lambda i, f, src, ex, used: (ex[i], ff(i, f, used), 0))],
        out_specs=pl.BlockSpec((tm, d), lambda i, f, src, ex, used: (i, 0)),
        scratch_shapes=[pltpu.VMEM((2, tm, d), F32), pltpu.VMEM((tm, d), BF16), pltpu.VMEM((tm, d), F32),
                        pltpu.SemaphoreType.DMA((2,))])
    return pl.pallas_call(
        functools.partial(_experts_kernel, tm=tm),
        grid_spec=grid_spec,
        out_shape=jax.ShapeDtypeStruct((p_rows, d), F32),
        compiler_params=_params(("arbitrary", "arbitrary")),
        name="moe_experts",
    )(src_token, tile_expert, n_used, h, w_gate_up, w_gate_up, w_down)


def _combine_kernel(pos_ref, y_hbm, x_ref, w_ref, gate_ref, gpost_ref, o_ref, rows_ref, sem, *, tm):
    i = pl.program_id(0)
    slot = lax.rem(i, 2)

    def gather(tile, into):
        for k in range(TOP_K):
            _gather_rows(y_hbm, rows_ref.at[into, k], sem.at[into],
                         lambda r, k=k: pos_ref[(tile * tm + r) * TOP_K + k], tm)

    @pl.when(i == 0)
    def _():
        gather(0, 0)

    for k in range(TOP_K):
        _wait_rows(y_hbm, rows_ref.at[slot, k], sem.at[slot])

    @pl.when(i + 1 < pl.num_programs(0))
    def _():
        gather(i + 1, 1 - slot)

    w = w_ref[...]
    mixed = w[:, 0:1] * rows_ref[slot, 0] + w[:, 1:2] * rows_ref[slot, 1]
    o_ref[...] = x_ref[...] + gate_ref[0] * _rms(mixed, gpost_ref[...])


def _combine(y, pos, top_w, x, gate, g_post, tm):
    n, d = x.shape
    tiles_per_group = (n // tm) // gate.shape[0]
    grid_spec = pltpu.PrefetchScalarGridSpec(
        num_scalar_prefetch=1,
        grid=(n // tm,),
        in_specs=[pl.BlockSpec(memory_space=pl.ANY),
                  pl.BlockSpec((tm, d), lambda i, pos_: (i, 0)),
                  pl.BlockSpec((tm, TOP_K), lambda i, pos_: (i, 0)),
                  _mod_spec(gate, tm, tiles_per_group),
                  pl.BlockSpec((1, d), lambda i, pos_: (0, 0))],
        out_specs=pl.BlockSpec((tm, d), lambda i, pos_: (i, 0)),
        scratch_shapes=[pltpu.VMEM((2, TOP_K, tm, d), F32), pltpu.SemaphoreType.DMA((2,))])
    return pl.pallas_call(
        functools.partial(_combine_kernel, tm=tm),
        grid_spec=grid_spec,
        out_shape=jax.ShapeDtypeStruct((n, d), F32),
        compiler_params=_params(("arbitrary",)),
        name="moe_combine",
    )(pos, y, x, top_w, gate, g_post)


def _moe(x, g_pre, shift, scale, gate, g_post, w_router, w_gate_up, w_down, n_experts, tm_route, tm, tf, tm_out):
    h, top_idx, top_w = _route(x, g_pre, shift, scale, w_router, n_experts, tm_route)
    src_token, tile_expert, n_used, pos = _dispatch_tables(top_idx, n_experts, tm)
    y = _experts(h, src_token, tile_expert, n_used, w_gate_up, w_down, tm, tf)
    return _combine(y, pos, top_w, x, gate, g_post, tm_out)


def _row_tile(n, want):
    return want if n % want == 0 else n


def kernel(x_prompt, x_sample, c_prompt, c_sample, cache_sb_k, cache_sb_v, cache_fox_k, cache_fox_v, cache_fox_logf, w_mod, b_mod, g_pre_mix, g_post_mix, g_pre_ffn, g_post_ffn, w_in, b_forget, w_sb_out, w_fox_out, w_out, w_ffn_gate_up, w_ffn_down, w_router, w_moe_gate_up, w_moe_down):
    bsz, seq, d = x_prompt.shape
    dec_b, dec_t, _ = x_sample.shape
    depth = w_mod.shape[0]
    n_past = cache_sb_k.shape[2]
    n_experts = w_router.shape[2]
    n_p, n_s = bsz * seq, dec_b * dec_t
    tq = _row_tile(seq, 256)
    past_chunk = _row_tile(n_past, 256)

    c_all = jnp.concatenate([c_prompt, c_sample], axis=0)
    c_rows = -(-c_all.shape[0] // 8) * 8
    c_all = jnp.pad(c_all, ((0, c_rows - c_all.shape[0]), (0, 0)))
    mod = _modulation(c_all, w_mod, b_mod)

    xp = x_prompt.reshape(n_p, d)
    xs = x_sample.reshape(n_s, d)
    stacked_p = stacked_s = None
    for l in range(depth):
        mod_p = mod[l, :bsz].reshape(bsz, 1, 6, d)
        mod_s = jnp.repeat(mod[l, bsz:bsz + dec_b].reshape(dec_b, 6, d), dec_t, axis=0)[None]
        mp = [mod_p[:, :, i] for i in range(6)]
        ms = [mod_s[:, :, i] for i in range(6)]
        vec = lambda a: a[l].reshape(1, -1)

        wqkv = w_in[l, :, :6 * WIDTH].astype(BF16)
        wf = jnp.pad(w_in[l, :, 6 * WIDTH:6 * WIDTH + N_HEADS], ((0, 0), (0, LANES - N_HEADS))).astype(BF16)
        wg = w_in[l, :, 6 * WIDTH + N_HEADS:].astype(BF16)
        bf = b_forget[l].reshape(1, N_HEADS)
        wsb, wfx, wo = w_sb_out[l].astype(BF16), w_fox_out[l].astype(BF16), w_out[l].astype(BF16)

        qkv_p, qvt_p, *stacked_p, sg_p = _in_projection(
            xp, vec(g_pre_mix), mp[0], mp[1], wqkv, wf, wg, bf, _row_tile(seq, 256), l, depth, stacked_p)
        lf_p = stacked_p[4][l]
        cum_p, key_cols_p = _forget_cumsum(jnp.swapaxes(lf_p.reshape(bsz, seq, N_HEADS), 1, 2), True)
        qkv_p3 = qkv_p.reshape(bsz, seq, 6 * WIDTH)
        osb_p = _sb_prompt(qkv_p3, qvt_p, tq)
        ofx_p = _fox_prompt(qkv_p3, qvt_p, cum_p, key_cols_p, _row_tile(seq, 512))
        xp = _mix_out(xp, osb_p.reshape(n_p, WIDTH), ofx_p.reshape(n_p, WIDTH), sg_p, mp[2], vec(g_post_mix),
                      wsb, wfx, wo, _row_tile(seq, 512))

        qkv_s, _, *stacked_s, sg_s = _in_projection(
            xs, vec(g_pre_mix), ms[0], ms[1], wqkv, wf, wg, bf, n_s, l, depth, stacked_s)
        lf_s = stacked_s[4][l]
        t_all = n_past + dec_t
        t_pad = -(-t_all // CUMSUM_CHUNK) * CUMSUM_CHUNK
        lf_all = jnp.concatenate([cache_fox_logf[l].astype(F32), lf_s.reshape(dec_b, dec_t, N_HEADS),
                                  jnp.zeros((dec_b, t_pad - t_all, N_HEADS), F32)], axis=1)
        (cum_s,) = _forget_cumsum(jnp.swapaxes(lf_all, 1, 2), False)
        fk_s, fq_s = _pair_layouts(cum_s, n_past, dec_t)
        cache = lambda a: a[l].reshape(dec_b, n_past, WIDTH)
        osb_s, ofx_s = _sample_attention(qkv_s.reshape(dec_b, dec_t, 6 * WIDTH), cache(cache_sb_k), cache(cache_sb_v),
                                         cache(cache_fox_k), cache(cache_fox_v), fq_s, fk_s, past_chunk)
        xs = _mix_out(xs, osb_s.reshape(n_s, WIDTH), ofx_s.reshape(n_s, WIDTH), sg_s, ms[2], vec(g_post_mix),
                      wsb, wfx, wo, n_s)

        if l % 2 == 0:
            wgu, wd = w_ffn_gate_up[l // 2].astype(BF16), w_ffn_down[l // 2].astype(BF16)
            tf = _row_tile(wd.shape[0], 1408)
            xp = _ffn(xp, vec(g_pre_ffn), mp[3], mp[4], mp[5], vec(g_post_ffn), wgu, wd, _row_tile(seq, 512), tf)
            xs = _ffn(xs, vec(g_pre_ffn), ms[3], ms[4], ms[5], vec(g_post_ffn), wgu, wd, n_s, tf)
        else:
            wr = jnp.pad(w_router[l // 2], ((0, 0), (0, LANES - n_experts)))
            wgu, wd = w_moe_gate_up[l // 2].astype(BF16), w_moe_down[l // 2].astype(BF16)
            tf = _row_tile(wd.shape[1], 896)
            xp = _moe(xp, vec(g_pre_ffn), mp[3], mp[4], mp[5], vec(g_post_ffn), wr, wgu, wd, n_experts,
                      _row_tile(seq, 512), 1024, tf, _row_tile(seq, 256))
            xs = _moe(xs, vec(g_pre_ffn), ms[3], ms[4], ms[5], vec(g_post_ffn), wr, wgu, wd, n_experts,
                      n_s, 128, tf, n_s)

    split_p = [a.reshape(depth, bsz, seq, *a.shape[2:]) for a in stacked_p]
    split_s = [a.reshape(depth, dec_b, dec_t, *a.shape[2:]) for a in stacked_s]
    return (xp.reshape(bsz, seq, d), xs.reshape(dec_b, dec_t, d), *split_p, *split_s)
```

```python
import functools

import jax
import jax.numpy as jnp
from jax import lax
from jax.experimental import pallas as pl
from jax.experimental.pallas import tpu as pltpu

F32 = jnp.float32
BF16 = jnp.bfloat16

HEAD_DIM = 64
N_HEADS = 8
LANES = 128
HEAD_PAIRS = N_HEADS * HEAD_DIM // LANES
WIDTH = N_HEADS * HEAD_DIM
TOP_K = 2
RMS_EPS = 1e-6
LOG2_E = 1.4426950408889634
SB_DEAD_LOG = -104.0
VMEM_LIMIT = 56 * 1024 * 1024


def _params(sem, vmem=VMEM_LIMIT):
    return pltpu.CompilerParams(dimension_semantics=sem, vmem_limit_bytes=vmem)


def _dot(a, b):
    return jnp.dot(a, b, preferred_element_type=F32)


def _dot_nt(a, b):
    return lax.dot_general(a, b, (((1,), (1,)), ((), ())), preferred_element_type=F32)


def _sigmoid(x):
    return 1.0 / (1.0 + jnp.exp(-x))


def _softplus(x):
    return jnp.maximum(x, 0.0) + jnp.log(1.0 + jnp.exp(-jnp.abs(x)))


def _rms(x, g):
    return x * lax.rsqrt(jnp.mean(x * x, axis=-1, keepdims=True) + RMS_EPS) * g


def _mod_kernel(c_ref, w_ref, b_ref, o_ref):
    c = c_ref[...]
    s = c * _sigmoid(c)
    o_ref[0] = jnp.dot(s, w_ref[0], precision=lax.Precision.HIGHEST, preferred_element_type=F32) + b_ref[0]


def _modulation(c_all, w_mod, b_mod):
    depth, d, d6 = w_mod.shape
    rows = c_all.shape[0]
    tn = 1024
    return pl.pallas_call(
        _mod_kernel,
        grid=(depth, d6 // tn),
        in_specs=[pl.BlockSpec((rows, d), lambda l, j: (0, 0)),
                  pl.BlockSpec((1, d, tn), lambda l, j: (l, 0, j)),
                  pl.BlockSpec((1, 1, tn), lambda l, j: (l, 0, j))],
        out_specs=pl.BlockSpec((1, rows, tn), lambda l, j: (l, 0, j)),
        out_shape=jax.ShapeDtypeStruct((depth, rows, d6), F32),
        compiler_params=_params(("arbitrary", "arbitrary")),
        name="modulation",
    )(c_all, w_mod, b_mod.reshape(depth, 1, d6))


def _mod_spec(mod, tm, tiles_per_group):
    _, r, d = mod.shape
    return pl.BlockSpec((1, r, d), lambda i, *_: (i // tiles_per_group, 0, 0))


TRANSPOSED_SECTIONS = (0, 2, 3, 5)


def _inproj_kernel(x_ref, g_ref, shift_ref, scale_ref, wqkv_ref, wf_ref, wg_ref, bf_ref, *refs):
    qkv_ref, qvt_ref, ksb_ref, vsb_ref, kfx_ref, vfx_ref, lf_ref, sg_ref = refs[-8:]
    tm = x_ref.shape[0]
    h = _rms(x_ref[...], g_ref[...]) * (1.0 + scale_ref[0]) + shift_ref[0]
    hb = h.astype(BF16)
    f32_outs = {1: ksb_ref, 2: vsb_ref, 4: kfx_ref, 5: vfx_ref}
    for c in range(6):
        cols = slice(c * WIDTH, (c + 1) * WIDTH)
        acc = _dot(hb, wqkv_ref[:, cols])
        if c in f32_outs:
            f32_outs[c][0] = acc.reshape(tm, N_HEADS, HEAD_DIM)
        if c in (0, 3):
            acc = acc * (HEAD_DIM ** -0.5 * (LOG2_E if c == 3 else 1.0))
        qkv_ref[:, cols] = acc.astype(BF16)
        if c in TRANSPOSED_SECTIONS:
            t = TRANSPOSED_SECTIONS.index(c)
            qvt_ref[t * WIDTH:(t + 1) * WIDTH, :] = acc.T.astype(BF16)
    f = _dot(hb, wf_ref[...])[:, :N_HEADS] + bf_ref[...]
    lf_ref[0] = -_softplus(-f)
    for c in range(wg_ref.shape[1] // WIDTH):
        cols = slice(c * WIDTH, (c + 1) * WIDTH)
        sg_ref[:, cols] = _sigmoid(_dot(hb, wg_ref[:, cols])).astype(BF16)


def _in_projection(x, g_pre, shift, scale, wqkv, wf, wg, b_forget, tm, layer, depth, stacked):
    n, d = x.shape
    tiles_per_group = (n // tm) // shift.shape[0]
    const = lambda i: (0, 0)
    row = lambda i: (i, 0)
    heads_spec = pl.BlockSpec((1, tm, N_HEADS, HEAD_DIM), lambda i: (layer, i, 0, 0))
    heads_shape = jax.ShapeDtypeStruct((depth, n, N_HEADS, HEAD_DIM), F32)
    n_in = 8
    stacked = () if stacked is None else tuple(stacked)
    return pl.pallas_call(
        _inproj_kernel,
        grid=(n // tm,),
        in_specs=[pl.BlockSpec((tm, d), row),
                  pl.BlockSpec((1, d), const),
                  _mod_spec(shift, tm, tiles_per_group),
                  _mod_spec(scale, tm, tiles_per_group),
                  pl.BlockSpec(wqkv.shape, const),
                  pl.BlockSpec(wf.shape, const),
                  pl.BlockSpec(wg.shape, const),
                  pl.BlockSpec((1, N_HEADS), const)] + [pl.BlockSpec(memory_space=pl.ANY)] * len(stacked),
        out_specs=[pl.BlockSpec((tm, 6 * WIDTH), row), pl.BlockSpec((len(TRANSPOSED_SECTIONS) * WIDTH, tm), lambda i: (0, i))]
                  + [heads_spec] * 4
                  + [pl.BlockSpec((1, tm, N_HEADS), lambda i: (layer, i, 0)), pl.BlockSpec((tm, wg.shape[1]), row)],
        out_shape=[jax.ShapeDtypeStruct((n, 6 * WIDTH), BF16),
                   jax.ShapeDtypeStruct((len(TRANSPOSED_SECTIONS) * WIDTH, n), BF16)] + [heads_shape] * 4
                  + [jax.ShapeDtypeStruct((depth, n, N_HEADS), F32), jax.ShapeDtypeStruct((n, wg.shape[1]), BF16)],
        input_output_aliases={n_in + k: 2 + k for k in range(len(stacked))},
        compiler_params=_params(("arbitrary",)),
        name="in_projection",
    )(x, g_pre, shift, scale, wqkv, wf, wg, b_forget, *stacked)


CUMSUM_CHUNK = 256


FORGET_SPLIT = 3


def _cumsum_kernel(x_ref, o_ref, kcol_ref=None):
    t = x_ref.shape[2]
    r = lax.broadcasted_iota(jnp.int32, (CUMSUM_CHUNK, CUMSUM_CHUNK), 0)
    c = lax.broadcasted_iota(jnp.int32, (CUMSUM_CHUNK, CUMSUM_CHUNK), 1)
    upper = jnp.where(r <= c, 1.0, 0.0).astype(F32)
    row = lax.broadcasted_iota(jnp.int32, (LANES, LANES), 0)
    lane = lax.broadcasted_iota(jnp.int32, (LANES, LANES), 1)
    head, term = row % N_HEADS, row // N_HEADS
    placed = jnp.logical_and(row < FORGET_SPLIT * N_HEADS, lane == (head % 2) * FORGET_SPLIT + term)
    pad_rows = jnp.zeros((LANES - FORGET_SPLIT * N_HEADS, CUMSUM_CHUNK), F32)

    def step(i, carry):
        start = pl.multiple_of(i * CUMSUM_CHUNK, CUMSUM_CHUNK)
        seg = x_ref[0, :, pl.ds(start, CUMSUM_CHUNK)]
        cs = jnp.dot(seg, upper, precision=lax.Precision.HIGHEST, preferred_element_type=F32) + carry
        f_log2 = cs * LOG2_E
        o_ref[0, :, pl.ds(start, CUMSUM_CHUNK)] = f_log2
        if kcol_ref is None:
            return cs[:, CUMSUM_CHUNK - 1:CUMSUM_CHUNK]
        rest = -f_log2
        terms = []
        for _ in range(FORGET_SPLIT):
            part = pltpu.bitcast(pltpu.bitcast(rest, jnp.uint32) & jnp.uint32(0xFFFF0000), F32)
            terms.append(part)
            rest = rest - part
        terms_t = jnp.concatenate(terms + [pad_rows], axis=0).T.astype(BF16)
        for p in range(HEAD_PAIRS):
            select = jnp.where(jnp.logical_and(placed, head // 2 == p), 1.0, 0.0).astype(BF16)
            kcol_ref[0, p, pl.ds(start, CUMSUM_CHUNK), :] = _dot(terms_t, select).astype(BF16)
        return cs[:, CUMSUM_CHUNK - 1:CUMSUM_CHUNK]

    lax.fori_loop(0, t // CUMSUM_CHUNK, step, jnp.zeros((x_ref.shape[1], 1), F32))


def _forget_cumsum(x, key_columns):
    b, r, t = x.shape
    out_specs = [pl.BlockSpec((1, r, t), lambda i: (i, 0, 0))]
    out_shape = [jax.ShapeDtypeStruct((b, r, t), F32)]
    if key_columns:
        out_specs.append(pl.BlockSpec((1, HEAD_PAIRS, t, LANES), lambda i: (i, 0, 0, 0)))
        out_shape.append(jax.ShapeDtypeStruct((b, HEAD_PAIRS, t, LANES), BF16))
    return pl.pallas_call(
        _cumsum_kernel,
        grid=(b,),
        in_specs=[pl.BlockSpec((1, r, t), lambda i: (i, 0, 0))],
        out_specs=out_specs,
        out_shape=out_shape,
        compiler_params=_params(("arbitrary",)),
        name="forget_cumsum",
    )(x)


def _pair_layouts(cum_t, t_query_start, t_query):
    b, _, tp = cum_t.shape
    f_keys = cum_t.reshape(b, HEAD_PAIRS, 2, tp)
    f_query = jnp.swapaxes(f_keys[:, :, :, t_query_start:t_query_start + t_query], 2, 3)
    return f_keys, f_query


def _head_masks():
    lane = lax.broadcasted_iota(jnp.int32, (1, LANES), 1)
    return lane < HEAD_DIM


def _split_heads(q2, first_head):
    zero = jnp.zeros_like(q2)
    return jnp.where(first_head, q2, zero), jnp.where(first_head, zero, q2)


def _strict_lower_neg(n):
    r = lax.broadcasted_iota(jnp.int32, (n, n), 0)
    c = lax.broadcasted_iota(jnp.int32, (n, n), 1)
    return jnp.where(r > c, -1.0, 0.0).astype(BF16)


def _sb_block(qh, k2, v2, neg_tri, visible, acc_ref, r_ref, hd):
    z = _dot_nt(qh, k2)
    sp = _softplus(z)
    if visible is not None:
        sp = jnp.where(visible, sp, 0.0)
    later = _dot(sp.astype(BF16), neg_tri)
    p = jnp.exp(z - sp + later)
    if visible is not None:
        p = jnp.where(visible, p, 0.0)
    r = r_ref[hd]
    acc_ref[hd] += jnp.exp(r) * _dot(p.astype(BF16), v2)
    r_ref[hd] = r + later[:, 0:1] - sp[:, 0:1]


def _fox_block(qh, k2, v2_ones, fq, fk, allowed, acc_ref, m_ref, hd):
    s = _dot_nt(qh, k2) + fq - fk
    if allowed is not None:
        s = jnp.where(allowed, s, -jnp.inf)
    m_old = m_ref[hd]
    m_new = jnp.maximum(m_old, jnp.max(s, axis=-1, keepdims=True))
    p = jnp.exp2(s - m_new)
    acc_ref[hd] = jnp.exp2(m_old - m_new) * acc_ref[hd] + _dot(p.astype(BF16), v2_ones)
    m_ref[hd] = m_new


def _fox_finish(acc_ref, first_head):
    a, b = acc_ref[0], acc_ref[1]
    num = jnp.where(first_head, a, b)
    den = jnp.where(first_head, pltpu.roll(a, HEAD_DIM, 1), pltpu.roll(b, HEAD_DIM, 1))
    return num / den


def _with_ones(v2, first_head):
    one = jnp.ones_like(v2)
    return jnp.where(first_head, v2, one), jnp.where(first_head, one, v2)


def _sb_prompt_kernel(qt_ref, k_ref, vt_ref, o_ref, w_ref, acc_ref, r_ref, tri_ref, *, tq):
    i = pl.program_id(2)
    n_pairs = qt_ref.shape[0] // LANES
    sub = lax.broadcasted_iota(jnp.int32, (LANES, tq), 0)
    first_rows = sub < HEAD_DIM
    for pp in range(n_pairs):
        qt = qt_ref[pp * LANES:(pp + 1) * LANES, :]
        zero = jnp.zeros_like(qt)
        w_ref[2 * pp] = jnp.where(first_rows, qt, zero)
        w_ref[2 * pp + 1] = jnp.where(first_rows, zero, qt)
    acc_ref[...] = jnp.zeros_like(acc_ref)
    r_ref[...] = jnp.zeros_like(r_ref)

    @pl.when(jnp.logical_and(jnp.logical_and(pl.program_id(0) == 0, pl.program_id(1) == 0), i == 0))
    def _():
        s_idx = lax.broadcasted_iota(jnp.int32, tri_ref.shape, 0)
        j_idx = lax.broadcasted_iota(jnp.int32, tri_ref.shape, 1)
        tri_ref[...] = jnp.where(j_idx > s_idx, -1.0, 0.0).astype(BF16)

    def block(start, size, vis):
        neg_tri = tri_ref[:size, :size]
        for hd in range(2 * n_pairs):
            lanes = slice((hd // 2) * LANES, (hd // 2 + 1) * LANES)
            keys = k_ref[0, pl.ds(start, size), lanes]
            vt = vt_ref[lanes, pl.ds(start, size)]
            z = _dot(keys, w_ref[hd])
            sp = _softplus(z)
            if vis is not None:
                sp = jnp.where(vis, sp, 0.0)
            later = _dot(neg_tri, sp.astype(BF16))
            p = jnp.exp(z - sp + later)
            if vis is not None:
                p = jnp.where(vis, p, 0.0)
            r = r_ref[hd]
            acc_ref[hd] += jnp.exp(r) * _dot(vt, p.astype(BF16))
            r_ref[hd] = r + later[0:1, :] - sp[0:1, :]

    def live():
        worst = r_ref[0]
        for hd in range(1, 2 * n_pairs):
            worst = jnp.maximum(worst, r_ref[hd])
        return jnp.max(worst)

    first = jnp.maximum(i - 1, 0)
    ahead = lax.broadcasted_iota(jnp.int32, (2 * tq, tq), 0) - lax.broadcasted_iota(jnp.int32, (2 * tq, tq), 1)
    block(pl.multiple_of(first * tq, tq), 2 * tq, ahead < (i - first) * tq)

    def cond(carry):
        j, r_max = carry
        return jnp.logical_and(j >= 0, r_max > SB_DEAD_LOG)

    def body(carry):
        j, _ = carry
        block(pl.multiple_of(j * tq, tq), tq, None)
        return j - 1, live()

    lax.while_loop(cond, body, (i - 2, live()))
    for pp in range(n_pairs):
        out_t = jnp.where(first_rows, acc_ref[2 * pp], acc_ref[2 * pp + 1])
        o_ref[0, :, pp * LANES:(pp + 1) * LANES] = out_t.T.astype(o_ref.dtype)


SB_PAIRS_PER_STEP = 2


def _sb_prompt(qkv, qv_t, tq):
    b, t, _ = qkv.shape
    nq = t // tq
    group = SB_PAIRS_PER_STEP * LANES
    n_groups = WIDTH // group
    heads = 2 * SB_PAIRS_PER_STEP
    kernel = functools.partial(_sb_prompt_kernel, tq=tq)
    return pl.pallas_call(
        kernel,
        grid=(b, n_groups, nq),
        in_specs=[pl.BlockSpec((group, tq), lambda b_, p, i: (p, b_ * nq + i)),
                  pl.BlockSpec((1, t, group), lambda b_, p, i: (b_, 0, n_groups + p)),
                  pl.BlockSpec((group, t), lambda b_, p, i: (n_groups + p, b_))],
        out_specs=pl.BlockSpec((1, tq, group), lambda b_, p, i: (b_, i, p)),
        out_shape=jax.ShapeDtypeStruct((b, t, WIDTH), BF16),
        scratch_shapes=[pltpu.VMEM((heads, LANES, tq), BF16), pltpu.VMEM((heads, LANES, tq), F32),
                        pltpu.VMEM((heads, 1, tq), F32), pltpu.VMEM((2 * tq, 2 * tq), BF16)],
        compiler_params=_params(("arbitrary", "arbitrary", "arbitrary")),
        name="sb_attention_prompt",
    )(qv_t, qkv, qv_t)


FOX_DEAD_LOG2 = -150.0
FOX_BOUNDED_MAX = 40.0


def _fox_prompt_kernel(kmax_ref, fend_ref, qt_ref, k_ref, kf_ref, vt_ref, fq_ref, o_ref,
                       w_ref, acc_ref, m_ref, zb_ref, *, tq):
    bi = pl.program_id(0)
    pi = pl.program_id(1)
    i = pl.program_id(2)
    sub = lax.broadcasted_iota(jnp.int32, (LANES, tq), 0)
    first_rows = sub < HEAD_DIM
    qt = qt_ref[...]
    zero = jnp.zeros_like(qt)
    for hd in range(2):
        lo = hd * FORGET_SPLIT
        select = jnp.where(jnp.logical_and(sub >= lo, sub < lo + FORGET_SPLIT), 1.0, 0.0).astype(BF16)
        q_head = jnp.where(first_rows, qt, zero) if hd == 0 else jnp.where(first_rows, zero, qt)
        w_ref[hd, :LANES, :] = q_head
        w_ref[hd, LANES:, :] = select
        q_f32 = q_head.astype(F32)
        q_norm = jnp.sqrt(jnp.sum(q_f32 * q_f32, axis=0, keepdims=True))
        zb_ref[hd] = q_norm * (kmax_ref[bi, 2 * pi + hd] * 1.01)
    acc_ref[...] = jnp.zeros_like(acc_ref)
    m_ref[...] = jnp.full_like(m_ref, -jnp.inf)
    key_idx = lax.broadcasted_iota(jnp.int32, (tq, tq), 0)
    query_idx = lax.broadcasted_iota(jnp.int32, (tq, tq), 1)
    allowed = key_idx <= query_idx
    den_row = (HEAD_DIM, 0)

    def operands(j):
        start = pl.multiple_of(j * tq, tq)
        keys = jnp.concatenate([k_ref[0, pl.ds(start, tq), :], kf_ref[0, 0, pl.ds(start, tq), :]], axis=1)
        vt = vt_ref[:, pl.ds(start, tq)]
        one = jnp.ones_like(vt)
        first_v = lax.broadcasted_iota(jnp.int32, vt.shape, 0) < HEAD_DIM
        return keys, (jnp.where(first_v, vt, one), jnp.where(first_v, one, vt))

    def block_bounded(j, mask):
        keys, vts = operands(j)
        for hd in range(2):
            u = _dot(keys, w_ref[hd])
            if mask is not None:
                u = jnp.where(mask, u, -jnp.inf)
            p = jnp.exp2(u + (fq_ref[0, 0, hd:hd + 1, :] - zb_ref[hd]))
            acc_ref[hd] += _dot(vts[hd], p.astype(BF16))

    def alive_bounded(j):
        jc = jnp.maximum(j, 0)
        reach = []
        for hd in range(2):
            den = acc_ref[hd, den_row[hd]:den_row[hd] + 1, :]
            gap = jnp.max(fq_ref[0, 0, hd:hd + 1, :] - jnp.log2(den))
            reach.append(gap - fend_ref[bi, pi, hd, jc])
        return jnp.maximum(reach[0], reach[1]) + tq.bit_length() > FOX_DEAD_LOG2

    def block_exact(j, mask):
        keys, vts = operands(j)
        for hd in range(2):
            u = _dot(keys, w_ref[hd])
            if mask is not None:
                u = jnp.where(mask, u, -jnp.inf)
            fq = fq_ref[0, 0, hd:hd + 1, :]
            m_old = m_ref[hd]
            m_new = jnp.maximum(m_old, jnp.max(u, axis=0, keepdims=True) + fq)
            p = jnp.exp2(u + (fq - m_new))
            acc_ref[hd] = jnp.exp2(m_old - m_new) * acc_ref[hd] + _dot(vts[hd], p.astype(BF16))
            m_ref[hd] = m_new

    def alive_exact(j):
        jc = jnp.maximum(j, 0)
        reach = [jnp.max(zb_ref[hd] + fq_ref[0, 0, hd:hd + 1, :] - m_ref[hd]) - fend_ref[bi, pi, hd, jc]
                 for hd in range(2)]
        return jnp.maximum(reach[0], reach[1]) > FOX_DEAD_LOG2

    def sweep(block, alive):
        block(i, allowed)

        def cond(carry):
            j, go = carry
            return jnp.logical_and(j >= 0, go)

        def body(carry):
            j, _ = carry
            block(j, None)
            return j - 1, alive(j - 1)

        lax.while_loop(cond, body, (i - 1, alive(i - 1)))

    bounded = jnp.max(jnp.maximum(zb_ref[0], zb_ref[1])) <= FOX_BOUNDED_MAX

    @pl.when(bounded)
    def _():
        sweep(block_bounded, alive_bounded)

    @pl.when(jnp.logical_not(bounded))
    def _():
        sweep(block_exact, alive_exact)

    a, b = acc_ref[0], acc_ref[1]
    out_t = jnp.where(first_rows, a / a[HEAD_DIM:HEAD_DIM + 1, :], b / b[0:1, :])
    o_ref[0] = out_t.T.astype(o_ref.dtype)


def _fox_prompt(qkv, qv_t, cum_t, key_cols, tq):
    b, t, _ = qkv.shape
    nq = t // tq
    f_rows = cum_t.reshape(b, HEAD_PAIRS, 2, t)
    f_block_end = f_rows[:, :, :, tq - 1::tq]
    k_fx = qkv[:, :, 4 * WIDTH:5 * WIDTH].astype(F32).reshape(b, t, N_HEADS, HEAD_DIM)
    k_max = jnp.sqrt(jnp.max(jnp.sum(k_fx * k_fx, axis=-1), axis=1))
    smem = pl.BlockSpec(memory_space=pltpu.SMEM)
    kernel = functools.partial(_fox_prompt_kernel, tq=tq)
    return pl.pallas_call(
        kernel,
        grid=(b, HEAD_PAIRS, nq),
        in_specs=[smem, smem,
                  pl.BlockSpec((LANES, tq), lambda b_, p, i: (2 * HEAD_PAIRS + p, b_ * nq + i)),
                  pl.BlockSpec((1, t, LANES), lambda b_, p, i: (b_, 0, 4 * HEAD_PAIRS + p)),
                  pl.BlockSpec((1, 1, t, LANES), lambda b_, p, i: (b_, p, 0, 0)),
                  pl.BlockSpec((LANES, t), lambda b_, p, i: (3 * HEAD_PAIRS + p, b_)),
                  pl.BlockSpec((1, 1, 2, tq), lambda b_, p, i: (b_, p, 0, i))],
        out_specs=pl.BlockSpec((1, tq, LANES), lambda b_, p, i: (b_, i, p)),
        out_shape=jax.ShapeDtypeStruct((b, t, WIDTH), BF16),
        scratch_shapes=[pltpu.VMEM((2, 2 * LANES, tq), BF16), pltpu.VMEM((2, LANES, tq), F32),
                        pltpu.VMEM((2, 1, tq), F32), pltpu.VMEM((2, 1, tq), F32)],
        compiler_params=_params(("arbitrary", "arbitrary", "arbitrary")),
        name="fox_attention_prompt",
    )(k_max, f_block_end, qv_t, qkv, key_cols, qv_t, f_rows)


def _sample_attn_kernel(qsb_ref, ksb_ref, vsb_ref, qfx_ref, kfx_ref, vfx_ref,
                        csk_ref, csv_ref, cfk_ref, cfv_ref, fq_ref, fk_ref,
                        osb_ref, ofx_ref, acc_ref, st_ref, *, tk):
    t_new = qsb_ref.shape[1]
    n_past = csk_ref.shape[1]
    n_blocks = n_past // tk
    first_head = _head_masks()
    row = lax.broadcasted_iota(jnp.int32, (t_new, t_new), 0)
    col = lax.broadcasted_iota(jnp.int32, (t_new, t_new), 1)

    qs = _split_heads(qsb_ref[0], first_head)
    acc_ref[...] = jnp.zeros_like(acc_ref)
    st_ref[...] = jnp.zeros_like(st_ref)
    tri_new = _strict_lower_neg(t_new)
    tri_past = _strict_lower_neg(tk)
    for hd in range(2):
        _sb_block(qs[hd], ksb_ref[0], vsb_ref[0], tri_new, col < row, acc_ref, st_ref, hd)

    def live():
        return jnp.max(jnp.maximum(st_ref[0], st_ref[1]))

    def sb_cond(carry):
        j, r_max = carry
        return jnp.logical_and(j >= 0, r_max > SB_DEAD_LOG)

    def sb_body(carry):
        j, _ = carry
        start = pl.multiple_of(j * tk, tk)
        k2 = csk_ref[0, pl.ds(start, tk), :].astype(BF16)
        v2 = csv_ref[0, pl.ds(start, tk), :].astype(BF16)
        for hd in range(2):
            _sb_block(qs[hd], k2, v2, tri_past, None, acc_ref, st_ref, hd)
        return j - 1, live()

    lax.while_loop(sb_cond, sb_body, (n_blocks - 1, live()))
    osb_ref[0] = jnp.where(first_head, acc_ref[0], acc_ref[1]).astype(osb_ref.dtype)

    qs = _split_heads(qfx_ref[0], first_head)
    acc_ref[...] = jnp.zeros_like(acc_ref)
    st_ref[...] = jnp.full_like(st_ref, -jnp.inf)
    fq = fq_ref[0, 0]
    vs = _with_ones(vfx_ref[0], first_head)
    for hd in range(2):
        fk = fk_ref[0, 0, hd:hd + 1, n_past:n_past + t_new]
        _fox_block(qs[hd], kfx_ref[0], vs[hd], fq[:, hd:hd + 1], fk, col <= row, acc_ref, st_ref, hd)

    k2 = cfk_ref[0].astype(BF16)
    vs_ = _with_ones(cfv_ref[0].astype(BF16), first_head)
    for hd in range(2):
        fk = fk_ref[0, 0, hd:hd + 1, 0:n_past]
        _fox_block(qs[hd], k2, vs_[hd], fq[:, hd:hd + 1], fk, None, acc_ref, st_ref, hd)
    ofx_ref[0] = _fox_finish(acc_ref, first_head).astype(ofx_ref.dtype)


def _sample_attention(qkv, cache_sb_k, cache_sb_v, cache_fox_k, cache_fox_v, f_query, f_keys, tk):
    b, t_new, _ = qkv.shape
    n_past = cache_sb_k.shape[1]
    tp = f_keys.shape[3]
    new = lambda c: pl.BlockSpec((1, t_new, LANES), lambda b_, p: (b_, 0, c * HEAD_PAIRS + p))
    past = pl.BlockSpec((1, n_past, LANES), lambda b_, p: (b_, 0, p))
    out = pl.BlockSpec((1, t_new, LANES), lambda b_, p: (b_, 0, p))
    kernel = functools.partial(_sample_attn_kernel, tk=tk)
    return pl.pallas_call(
        kernel,
        grid=(b, HEAD_PAIRS),
        in_specs=[new(0), new(1), new(2), new(3), new(4), new(5), past, past, past, past,
                  pl.BlockSpec((1, 1, t_new, 2), lambda b_, p: (b_, p, 0, 0)),
                  pl.BlockSpec((1, 1, 2, tp), lambda b_, p: (b_, p, 0, 0))],
        out_specs=[out, out],
        out_shape=[jax.ShapeDtypeStruct((b, t_new, WIDTH), BF16)] * 2,
        scratch_shapes=[pltpu.VMEM((2, t_new, LANES), F32), pltpu.VMEM((2, t_new, 1), F32)],
        compiler_params=_params(("arbitrary", "arbitrary")),
        name="attention_sample",
    )(qkv, qkv, qkv, qkv, qkv, qkv, cache_sb_k, cache_sb_v, cache_fox_k, cache_fox_v, f_query, f_keys)


def _mix_out_kernel(x_ref, osb_ref, ofx_ref, sg_ref, gate_ref, g_ref, wsb_ref, wfx_ref, wo_ref, o_ref):
    d = x_ref.shape[1]
    merged = (sg_ref[:, :d].astype(F32) * _dot(osb_ref[...], wsb_ref[...])
              + sg_ref[:, d:].astype(F32) * _dot(ofx_ref[...], wfx_ref[...]))
    y = _dot(merged.astype(BF16), wo_ref[...])
    o_ref[...] = x_ref[...] + gate_ref[0] * _rms(y, g_ref[...])


def _mix_out(x, o_sb, o_fx, sg, gate, g_post, w_sb_out, w_fox_out, w_out, tm):
    n, d = x.shape
    tiles_per_group = (n // tm) // gate.shape[0]
    const = lambda i: (0, 0)
    row = lambda i: (i, 0)
    return pl.pallas_call(
        _mix_out_kernel,
        grid=(n // tm,),
        in_specs=[pl.BlockSpec((tm, d), row), pl.BlockSpec((tm, WIDTH), row), pl.BlockSpec((tm, WIDTH), row),
                  pl.BlockSpec((tm, 2 * d), row), _mod_spec(gate, tm, tiles_per_group),
                  pl.BlockSpec((1, d), const), pl.BlockSpec(w_sb_out.shape, const),
                  pl.BlockSpec(w_fox_out.shape, const), pl.BlockSpec(w_out.shape, const)],
        out_specs=pl.BlockSpec((tm, d), row),
        out_shape=jax.ShapeDtypeStruct((n, d), F32),
        compiler_params=_params(("arbitrary",)),
        name="mix_out",
    )(x, o_sb, o_fx, sg, gate, g_post, w_sb_out, w_fox_out, w_out)


def _swiglu_chunk(hb, wa, wb, wd):
    a = _dot(hb, wa)
    b = _dot(hb, wb)
    return a * _sigmoid(a) * b, wd


def _ffn_kernel(x_ref, gpre_ref, shift_ref, scale_ref, gate_ref, gpost_ref, wa_ref, wb_ref, wd_ref,
                o_ref, h_ref, acc_ref):
    f = pl.program_id(1)

    @pl.when(f == 0)
    def _():
        h = _rms(x_ref[...], gpre_ref[...]) * (1.0 + scale_ref[0]) + shift_ref[0]
        h_ref[...] = h.astype(BF16)
        acc_ref[...] = jnp.zeros_like(acc_ref)

    hb = h_ref[...]
    a = _dot(hb, wa_ref[...])
    b = _dot(hb, wb_ref[...])
    act = a * _sigmoid(a) * b
    acc_ref[...] += _dot(act.astype(BF16), wd_ref[...])

    @pl.when(f == pl.num_programs(1) - 1)
    def _():
        o_ref[...] = x_ref[...] + gate_ref[0] * _rms(acc_ref[...], gpost_ref[...])


def _ffn(x, g_pre, shift, scale, gate, g_post, w_gate_up, w_down, tm, tf):
    n, d = x.shape
    d_ff = w_down.shape[0]
    nf = d_ff // tf
    tiles_per_group = (n // tm) // shift.shape[0]
    const = lambda i, f: (0, 0)
    row = lambda i, f: (i, 0)
    return pl.pallas_call(
        _ffn_kernel,
        grid=(n // tm, nf),
        in_specs=[pl.BlockSpec((tm, d), row), pl.BlockSpec((1, d), const),
                  _mod_spec(shift, tm, tiles_per_group), _mod_spec(scale, tm, tiles_per_group),
                  _mod_spec(gate, tm, tiles_per_group), pl.BlockSpec((1, d), const),
                  pl.BlockSpec((d, tf), lambda i, f: (0, f)),
                  pl.BlockSpec((d, tf), lambda i, f: (0, nf + f)),
                  pl.BlockSpec((tf, d), lambda i, f: (f, 0))],
        out_specs=pl.BlockSpec((tm, d), row),
        out_shape=jax.ShapeDtypeStruct((n, d), F32),
        scratch_shapes=[pltpu.VMEM((tm, d), BF16), pltpu.VMEM((tm, d), F32)],
        compiler_params=_params(("arbitrary", "arbitrary")),
        name="ffn_dense",
    )(x, g_pre, shift, scale, gate, g_post, w_gate_up, w_gate_up, w_down)


def _route_kernel(x_ref, gpre_ref, shift_ref, scale_ref, wr_ref, h_ref, idx_ref, w_ref, rank_ref, count_ref,
                  *, n_experts):
    tm = x_ref.shape[0]

    @pl.when(pl.program_id(0) == 0)
    def _():
        count_ref[...] = jnp.zeros_like(count_ref)

    h = _rms(x_ref[...], gpre_ref[...]) * (1.0 + scale_ref[0]) + shift_ref[0]
    h_ref[...] = h
    logits = jnp.dot(h, wr_ref[...], precision=lax.Precision.HIGHEST, preferred_element_type=F32)[:, :n_experts]
    idx = lax.broadcasted_iota(jnp.int32, logits.shape, 1)
    m1 = jnp.max(logits, axis=-1, keepdims=True)
    i1 = jnp.min(jnp.where(logits == m1, idx, n_experts), axis=-1, keepdims=True)
    rest = jnp.where(idx == i1, -jnp.inf, logits)
    m2 = jnp.max(rest, axis=-1, keepdims=True)
    i2 = jnp.min(jnp.where(rest == m2, idx, n_experts), axis=-1, keepdims=True)
    e2 = jnp.exp(m2 - m1)
    slot = lax.broadcasted_iota(jnp.int32, idx_ref.shape, 1)
    idx_ref[...] = jnp.where(slot == 0, i1, i2)
    w_ref[...] = jnp.where(slot == 0, 1.0 / (1.0 + e2), e2 / (1.0 + e2))
    lane = lax.broadcasted_iota(jnp.int32, (tm, LANES), 1)
    first, second = lane == i1, lane == i2
    chosen = jnp.where(jnp.logical_or(first, second), 1.0, 0.0)
    t_row = lax.broadcasted_iota(jnp.int32, (tm, tm), 0)
    t_col = lax.broadcasted_iota(jnp.int32, (tm, tm), 1)
    earlier = jnp.where(t_col < t_row, 1.0, 0.0).astype(BF16)
    before = count_ref[0:1, :] + _dot(earlier, chosen.astype(BF16))
    rank1 = jnp.sum(jnp.where(first, before, 0.0), axis=-1, keepdims=True)
    rank2 = jnp.sum(jnp.where(second, before, 0.0), axis=-1, keepdims=True)
    rank_ref[...] = jnp.where(slot == 0, rank1, rank2).astype(jnp.int32)
    count_ref[...] = count_ref[...] + jnp.sum(chosen, axis=0, keepdims=True)


def _route(x, g_pre, shift, scale, w_router, n_experts, tm):
    n, d = x.shape
    tiles_per_group = (n // tm) // shift.shape[0]
    const = lambda i: (0, 0)
    row = lambda i: (i, 0)
    return pl.pallas_call(
        functools.partial(_route_kernel, n_experts=n_experts),
        grid=(n // tm,),
        in_specs=[pl.BlockSpec((tm, d), row), pl.BlockSpec((1, d), const),
                  _mod_spec(shift, tm, tiles_per_group), _mod_spec(scale, tm, tiles_per_group),
                  pl.BlockSpec(w_router.shape, const)],
        out_specs=[pl.BlockSpec((tm, d), row), pl.BlockSpec((tm, TOP_K), row), pl.BlockSpec((tm, TOP_K), row),
                   pl.BlockSpec((tm, TOP_K), row), pl.BlockSpec((8, LANES), const)],
        out_shape=[jax.ShapeDtypeStruct((n, d), F32), jax.ShapeDtypeStruct((n, TOP_K), jnp.int32),
                   jax.ShapeDtypeStruct((n, TOP_K), F32), jax.ShapeDtypeStruct((n, TOP_K), jnp.int32),
                   jax.ShapeDtypeStruct((8, LANES), F32)],
        compiler_params=_params(("arbitrary",)),
        name="moe_route",
    )(x, g_pre, shift, scale, w_router)


def _dispatch_tables(top_idx, rank, counts, n_experts, tm):
    n = top_idx.shape[0]
    n_assign = n * TOP_K
    expert = top_idx.reshape(n_assign)
    padded = (counts + tm - 1) // tm * tm
    group_end = jnp.cumsum(padded)
    group_start = group_end - padded
    onehot = expert[:, None] == jnp.arange(n_experts, dtype=jnp.int32)[None, :]
    pos = jnp.sum(jnp.where(onehot, group_start[None, :], 0), axis=1) + rank.reshape(n_assign)
    p_rows = (-(-n_assign // tm) + n_experts) * tm
    src_token = jnp.zeros((p_rows,), jnp.int32).at[pos].set(jnp.arange(n_assign, dtype=jnp.int32) // TOP_K)
    tile_start = jnp.arange(p_rows // tm, dtype=jnp.int32) * tm
    tile_expert = jnp.minimum(jnp.sum(tile_start[:, None] >= group_end[None, :], axis=1), n_experts - 1)
    n_used = (group_end[-1] // tm).reshape(1)
    return src_token, tile_expert.astype(jnp.int32), n_used.astype(jnp.int32), pos.astype(jnp.int32)


def _gather_rows(src_hbm, dst, sem, index_of_row, n_rows):
    def issue(r, carry):
        pltpu.make_async_copy(src_hbm.at[pl.ds(index_of_row(r), 1), :], dst.at[pl.ds(r, 1), :], sem).start()
        return carry

    lax.fori_loop(0, n_rows, issue, 0, unroll=8)


def _wait_rows(src_hbm, dst, sem):
    pltpu.make_async_copy(src_hbm.at[pl.ds(0, dst.shape[0]), :], dst, sem).wait()


def _experts_kernel(src_ref, expert_ref, used_ref, h_hbm, wa_ref, wb_ref, wd_ref, o_ref,
                    rows_ref, xb_ref, acc_ref, sem, *, tm):
    i = pl.program_id(0)
    f = pl.program_id(1)
    nf = pl.num_programs(1)
    n_used = used_ref[0]
    slot = lax.rem(i, 2)
    active = i < n_used
    rows_per_step = tm // nf

    @pl.when(jnp.logical_and(f == 0, i == 0))
    def _():
        _gather_rows(h_hbm, rows_ref.at[0], sem.at[0], lambda r: src_ref[r], tm)

    @pl.when(jnp.logical_and(f == 0, i <= n_used))
    def _():
        _wait_rows(h_hbm, rows_ref.at[slot], sem.at[slot])

    @pl.when(jnp.logical_and(f == 0, active))
    def _():
        xb_ref[...] = rows_ref[slot].astype(BF16)
        acc_ref[...] = jnp.zeros_like(acc_ref)

    @pl.when(active)
    def _():
        base = (i + 1) * tm + f * rows_per_step
        nxt = rows_ref.at[1 - slot]
        for r in range(rows_per_step):
            row = f * rows_per_step + r
            pltpu.make_async_copy(h_hbm.at[pl.ds(src_ref[base + r], 1), :], nxt.at[pl.ds(row, 1), :],
                                  sem.at[1 - slot]).start()
        xb = xb_ref[...]
        a = _dot(xb, wa_ref[0])
        b = _dot(xb, wb_ref[0])
        act = a * _sigmoid(a) * b
        acc_ref[...] += _dot(act.astype(BF16), wd_ref[0])

    last = f == pl.num_programs(1) - 1

    @pl.when(jnp.logical_and(last, active))
    def _():
        o_ref[...] = acc_ref[...]

    @pl.when(jnp.logical_and(last, jnp.logical_not(active)))
    def _():
        o_ref[...] = jnp.zeros_like(o_ref)


def _experts(h, src_token, tile_expert, n_used, w_gate_up, w_down, tm, tf):
    p_rows = src_token.shape[0]
    d = h.shape[1]
    nf = w_down.shape[1] // tf
    ff = lambda i, f, used: jnp.where(i < used[0], f, nf - 1)
    grid_spec = pltpu.PrefetchScalarGridSpec(
        num_scalar_prefetch=3,
        grid=(p_rows // tm, nf),
        in_specs=[pl.BlockSpec(memory_space=pl.ANY),
                  pl.BlockSpec((1, d, tf), lambda i, f, src, ex, used: (ex[i], 0, ff(i, f, used))),
                  pl.BlockSpec((1, d, tf), lambda i, f, src, ex, used: (ex[i], 0, nf + ff(i, f, used))),
                  pl.BlockSpec((1, tf, d), lambda i, f, src, ex, used: (ex[i], ff(i, f, used), 0))],
        out_specs=pl.BlockSpec((tm, d), lambda i, f, src, ex, used: (i, 0)),
        scratch_shapes=[pltpu.VMEM((2, tm, d), F32), pltpu.VMEM((tm, d), BF16), pltpu.VMEM((tm, d), F32),
                        pltpu.SemaphoreType.DMA((2,))])
    return pl.pallas_call(
        functools.partial(_experts_kernel, tm=tm),
        grid_spec=grid_spec,
        out_shape=jax.ShapeDtypeStruct((p_rows, d), F32),
        compiler_params=_params(("arbitrary", "arbitrary")),
        name="moe_experts",
    )(src_token, tile_expert, n_used, h, w_gate_up, w_gate_up, w_down)


def _combine_kernel(pos_ref, y_hbm, x_ref, w_ref, gate_ref, gpost_ref, o_ref, rows_ref, sem, *, tm):
    i = pl.program_id(0)
    slot = lax.rem(i, 2)

    def gather(tile, into):
        for k in range(TOP_K):
            _gather_rows(y_hbm, rows_ref.at[into, k], sem.at[into],
                         lambda r, k=k: pos_ref[(tile * tm + r) * TOP_K + k], tm)

    @pl.when(i == 0)
    def _():
        gather(0, 0)

    for k in range(TOP_K):
        _wait_rows(y_hbm, rows_ref.at[slot, k], sem.at[slot])

    @pl.when(i + 1 < pl.num_programs(0))
    def _():
        gather(i + 1, 1 - slot)

    w = w_ref[...]
    mixed = w[:, 0:1] * rows_ref[slot, 0] + w[:, 1:2] * rows_ref[slot, 1]
    o_ref[...] = x_ref[...] + gate_ref[0] * _rms(mixed, gpost_ref[...])


def _combine(y, pos, top_w, x, gate, g_post, tm):
    n, d = x.shape
    tiles_per_group = (n // tm) // gate.shape[0]
    grid_spec = pltpu.PrefetchScalarGridSpec(
        num_scalar_prefetch=1,
        grid=(n // tm,),
        in_specs=[pl.BlockSpec(memory_space=pl.ANY),
                  pl.BlockSpec((tm, d), lambda i, pos_: (i, 0)),
                  pl.BlockSpec((tm, TOP_K), lambda i, pos_: (i, 0)),
                  _mod_spec(gate, tm, tiles_per_group),
                  pl.BlockSpec((1, d), lambda i, pos_: (0, 0))],
        out_specs=pl.BlockSpec((tm, d), lambda i, pos_: (i, 0)),
        scratch_shapes=[pltpu.VMEM((2, TOP_K, tm, d), F32), pltpu.SemaphoreType.DMA((2,))])
    return pl.pallas_call(
        functools.partial(_combine_kernel, tm=tm),
        grid_spec=grid_spec,
        out_shape=jax.ShapeDtypeStruct((n, d), F32),
        compiler_params=_params(("arbitrary",)),
        name="moe_combine",
    )(pos, y, x, top_w, gate, g_post)


def _moe(x, g_pre, shift, scale, gate, g_post, w_router, w_gate_up, w_down, n_experts, tm_route, tm, tf, tm_out):
    h, top_idx, top_w, rank, counts = _route(x, g_pre, shift, scale, w_router, n_experts, tm_route)
    counts = counts[0, :n_experts].astype(jnp.int32)
    src_token, tile_expert, n_used, pos = _dispatch_tables(top_idx, rank, counts, n_experts, tm)
    y = _experts(h, src_token, tile_expert, n_used, w_gate_up, w_down, tm, tf)
    return _combine(y, pos, top_w, x, gate, g_post, tm_out)


def _row_tile(n, want):
    return want if n % want == 0 else n


def kernel(x_prompt, x_sample, c_prompt, c_sample, cache_sb_k, cache_sb_v, cache_fox_k, cache_fox_v, cache_fox_logf, w_mod, b_mod, g_pre_mix, g_post_mix, g_pre_ffn, g_post_ffn, w_in, b_forget, w_sb_out, w_fox_out, w_out, w_ffn_gate_up, w_ffn_down, w_router, w_moe_gate_up, w_moe_down):
    bsz, seq, d = x_prompt.shape
    dec_b, dec_t, _ = x_sample.shape
    depth = w_mod.shape[0]
    n_past = cache_sb_k.shape[2]
    n_experts = w_router.shape[2]
    n_p, n_s = bsz * seq, dec_b * dec_t
    tq = _row_tile(seq, 256)
    past_chunk = _row_tile(n_past, 256)

    c_all = jnp.concatenate([c_prompt, c_sample], axis=0)
    c_rows = -(-c_all.shape[0] // 8) * 8
    c_all = jnp.pad(c_all, ((0, c_rows - c_all.shape[0]), (0, 0)))
    mod = _modulation(c_all, w_mod, b_mod)

    xp = x_prompt.reshape(n_p, d)
    xs = x_sample.reshape(n_s, d)
    stacked_p = stacked_s = None
    for l in range(depth):
        mod_p = mod[l, :bsz].reshape(bsz, 1, 6, d)
        mod_s = jnp.repeat(mod[l, bsz:bsz + dec_b].reshape(dec_b, 6, d), dec_t, axis=0)[None]
        mp = [mod_p[:, :, i] for i in range(6)]
        ms = [mod_s[:, :, i] for i in range(6)]
        vec = lambda a: a[l].reshape(1, -1)

        wqkv = w_in[l, :, :6 * WIDTH].astype(BF16)
        wf = jnp.pad(w_in[l, :, 6 * WIDTH:6 * WIDTH + N_HEADS], ((0, 0), (0, LANES - N_HEADS))).astype(BF16)
        wg = w_in[l, :, 6 * WIDTH + N_HEADS:].astype(BF16)
        bf = b_forget[l].reshape(1, N_HEADS)
        wsb, wfx, wo = w_sb_out[l].astype(BF16), w_fox_out[l].astype(BF16), w_out[l].astype(BF16)

        qkv_p, qvt_p, *stacked_p, sg_p = _in_projection(
            xp, vec(g_pre_mix), mp[0], mp[1], wqkv, wf, wg, bf, _row_tile(seq, 256), l, depth, stacked_p)
        lf_p = stacked_p[4][l]
        cum_p, key_cols_p = _forget_cumsum(jnp.swapaxes(lf_p.reshape(bsz, seq, N_HEADS), 1, 2), True)
        qkv_p3 = qkv_p.reshape(bsz, seq, 6 * WIDTH)
        osb_p = _sb_prompt(qkv_p3, qvt_p, tq)
        ofx_p = _fox_prompt(qkv_p3, qvt_p, cum_p, key_cols_p, _row_tile(seq, 512))
        xp = _mix_out(xp, osb_p.reshape(n_p, WIDTH), ofx_p.reshape(n_p, WIDTH), sg_p, mp[2], vec(g_post_mix),
                      wsb, wfx, wo, _row_tile(seq, 512))

        qkv_s, _, *stacked_s, sg_s = _in_projection(
            xs, vec(g_pre_mix), ms[0], ms[1], wqkv, wf, wg, bf, n_s, l, depth, stacked_s)
        lf_s = stacked_s[4][l]
        t_all = n_past + dec_t
        t_pad = -(-t_all // CUMSUM_CHUNK) * CUMSUM_CHUNK
        lf_all = jnp.concatenate([cache_fox_logf[l].astype(F32), lf_s.reshape(dec_b, dec_t, N_HEADS),
                                  jnp.zeros((dec_b, t_pad - t_all, N_HEADS), F32)], axis=1)
        (cum_s,) = _forget_cumsum(jnp.swapaxes(lf_all, 1, 2), False)
        fk_s, fq_s = _pair_layouts(cum_s, n_past, dec_t)
        cache = lambda a: a[l].reshape(dec_b, n_past, WIDTH)
        osb_s, ofx_s = _sample_attention(qkv_s.reshape(dec_b, dec_t, 6 * WIDTH), cache(cache_sb_k), cache(cache_sb_v),
                                         cache(cache_fox_k), cache(cache_fox_v), fq_s, fk_s, past_chunk)
        xs = _mix_out(xs, osb_s.reshape(n_s, WIDTH), ofx_s.reshape(n_s, WIDTH), sg_s, ms[2], vec(g_post_mix),
                      wsb, wfx, wo, n_s)

        if l % 2 == 0:
            wgu, wd = w_ffn_gate_up[l // 2].astype(BF16), w_ffn_down[l // 2].astype(BF16)
            tf = _row_tile(wd.shape[0], 1408)
            xp = _ffn(xp, vec(g_pre_ffn), mp[3], mp[4], mp[5], vec(g_post_ffn), wgu, wd, _row_tile(seq, 512), tf)
            xs = _ffn(xs, vec(g_pre_ffn), ms[3], ms[4], ms[5], vec(g_post_ffn), wgu, wd, n_s, tf)
        else:
            wr = jnp.pad(w_router[l // 2], ((0, 0), (0, LANES - n_experts)))
            wgu, wd = w_moe_gate_up[l // 2].astype(BF16), w_moe_down[l // 2].astype(BF16)
            tf = _row_tile(wd.shape[1], 896)
            xp = _moe(xp, vec(g_pre_ffn), mp[3], mp[4], mp[5], vec(g_post_ffn), wr, wgu, wd, n_experts,
                      _row_tile(seq, 512), 1024, tf, _row_tile(seq, 256))
            xs = _moe(xs, vec(g_pre_ffn), ms[3], ms[4], ms[5], vec(g_post_ffn), wr, wgu, wd, n_experts,
                      n_s, 128, tf, n_s)

    split_p = [a.reshape(depth, bsz, seq, *a.shape[2:]) for a in stacked_p]
    split_s = [a.reshape(depth, dec_b, dec_t, *a.shape[2:]) for a in stacked_s]
    return (xp.reshape(bsz, seq, d), xs.reshape(dec_b, dec_t, d), *split_p, *split_s)
```

```python
import functools

import jax
import jax.numpy as jnp
from jax import lax
from jax.experimental import pallas as pl
from jax.experimental.pallas import tpu as pltpu

F32 = jnp.float32
BF16 = jnp.bfloat16

HEAD_DIM = 64
N_HEADS = 8
LANES = 128
HEAD_PAIRS = N_HEADS * HEAD_DIM // LANES
WIDTH = N_HEADS * HEAD_DIM
TOP_K = 2
RMS_EPS = 1e-6
LOG2_E = 1.4426950408889634
SB_DEAD_LOG = -104.0
VMEM_LIMIT = 56 * 1024 * 1024


def _params(sem, vmem=VMEM_LIMIT):
    return pltpu.CompilerParams(dimension_semantics=sem, vmem_limit_bytes=vmem)


def _dot(a, b):
    return jnp.dot(a, b, preferred_element_type=F32)


def _dot_nt(a, b):
    return lax.dot_general(a, b, (((1,), (1,)), ((), ())), preferred_element_type=F32)


def _sigmoid(x):
    return 1.0 / (1.0 + jnp.exp(-x))


def _softplus(x):
    return jnp.maximum(x, 0.0) + jnp.log(1.0 + jnp.exp(-jnp.abs(x)))


def _rms(x, g):
    return x * lax.rsqrt(jnp.mean(x * x, axis=-1, keepdims=True) + RMS_EPS) * g


def _mod_kernel(c_ref, w_ref, b_ref, o_ref):
    c = c_ref[...]
    s = c * _sigmoid(c)
    o_ref[0] = jnp.dot(s, w_ref[0], precision=lax.Precision.HIGHEST, preferred_element_type=F32) + b_ref[0]


def _modulation(c_all, w_mod, b_mod):
    depth, d, d6 = w_mod.shape
    rows = c_all.shape[0]
    tn = 1024
    return pl.pallas_call(
        _mod_kernel,
        grid=(depth, d6 // tn),
        in_specs=[pl.BlockSpec((rows, d), lambda l, j: (0, 0)),
                  pl.BlockSpec((1, d, tn), lambda l, j: (l, 0, j)),
                  pl.BlockSpec((1, 1, tn), lambda l, j: (l, 0, j))],
        out_specs=pl.BlockSpec((1, rows, tn), lambda l, j: (l, 0, j)),
        out_shape=jax.ShapeDtypeStruct((depth, rows, d6), F32),
        compiler_params=_params(("arbitrary", "arbitrary")),
        name="modulation",
    )(c_all, w_mod, b_mod.reshape(depth, 1, d6))


def _mod_spec(mod, tm, tiles_per_group):
    _, r, d = mod.shape
    return pl.BlockSpec((1, r, d), lambda i, *_: (i // tiles_per_group, 0, 0))


TRANSPOSED_SECTIONS = (0, 2, 3, 5)


def _inproj_kernel(x_ref, g_ref, shift_ref, scale_ref, wqkv_ref, wf_ref, wg_ref, bf_ref, *refs, tiles_per_group):
    qkv_ref, qvt_ref, ksb_ref, vsb_ref, kfx_ref, vfx_ref, lf_ref, sg_ref, ksq_ref = refs[-9:]
    tm = x_ref.shape[0]
    h = _rms(x_ref[...], g_ref[...]) * (1.0 + scale_ref[0]) + shift_ref[0]
    hb = h.astype(BF16)
    f32_outs = {1: ksb_ref, 2: vsb_ref, 4: kfx_ref, 5: vfx_ref}
    for c in range(6):
        cols = slice(c * WIDTH, (c + 1) * WIDTH)
        acc = _dot(hb, wqkv_ref[:, cols])
        if c in f32_outs:
            by_head = acc.reshape(tm, N_HEADS, HEAD_DIM)
            f32_outs[c][0] = by_head
            if c == 4:
                ksq = jnp.max(jnp.sum(by_head * by_head, axis=-1), axis=0, keepdims=True)
                first_tile = pl.program_id(0) % tiles_per_group == 0

                @pl.when(first_tile)
                def _():
                    ksq_ref[0] = ksq

                @pl.when(jnp.logical_not(first_tile))
                def _():
                    ksq_ref[0] = jnp.maximum(ksq_ref[0], ksq)
        if c in (0, 3):
            acc = acc * (HEAD_DIM ** -0.5 * (LOG2_E if c == 3 else 1.0))
        qkv_ref[:, cols] = acc.astype(BF16)
        if c in TRANSPOSED_SECTIONS:
            t = TRANSPOSED_SECTIONS.index(c)
            qvt_ref[t * WIDTH:(t + 1) * WIDTH, :] = acc.T.astype(BF16)
    f = _dot(hb, wf_ref[...])[:, :N_HEADS] + bf_ref[...]
    lf_ref[0] = -_softplus(-f)
    for c in range(wg_ref.shape[1] // WIDTH):
        cols = slice(c * WIDTH, (c + 1) * WIDTH)
        sg_ref[:, cols] = _sigmoid(_dot(hb, wg_ref[:, cols])).astype(BF16)


def _in_projection(x, g_pre, shift, scale, wqkv, wf, wg, b_forget, tm, layer, depth, stacked):
    n, d = x.shape
    tiles_per_group = (n // tm) // shift.shape[0]
    const = lambda i: (0, 0)
    row = lambda i: (i, 0)
    heads_spec = pl.BlockSpec((1, tm, N_HEADS, HEAD_DIM), lambda i: (layer, i, 0, 0))
    heads_shape = jax.ShapeDtypeStruct((depth, n, N_HEADS, HEAD_DIM), F32)
    n_in = 8
    stacked = () if stacked is None else tuple(stacked)
    n_groups = shift.shape[0]
    return pl.pallas_call(
        functools.partial(_inproj_kernel, tiles_per_group=tiles_per_group),
        grid=(n // tm,),
        in_specs=[pl.BlockSpec((tm, d), row),
                  pl.BlockSpec((1, d), const),
                  _mod_spec(shift, tm, tiles_per_group),
                  _mod_spec(scale, tm, tiles_per_group),
                  pl.BlockSpec(wqkv.shape, const),
                  pl.BlockSpec(wf.shape, const),
                  pl.BlockSpec(wg.shape, const),
                  pl.BlockSpec((1, N_HEADS), const)] + [pl.BlockSpec(memory_space=pl.ANY)] * len(stacked),
        out_specs=[pl.BlockSpec((tm, 6 * WIDTH), row), pl.BlockSpec((len(TRANSPOSED_SECTIONS) * WIDTH, tm), lambda i: (0, i))]
                  + [heads_spec] * 4
                  + [pl.BlockSpec((1, tm, N_HEADS), lambda i: (layer, i, 0)), pl.BlockSpec((tm, wg.shape[1]), row),
                     pl.BlockSpec((1, 1, N_HEADS), lambda i: (i // tiles_per_group, 0, 0))],
        out_shape=[jax.ShapeDtypeStruct((n, 6 * WIDTH), BF16),
                   jax.ShapeDtypeStruct((len(TRANSPOSED_SECTIONS) * WIDTH, n), BF16)] + [heads_shape] * 4
                  + [jax.ShapeDtypeStruct((depth, n, N_HEADS), F32), jax.ShapeDtypeStruct((n, wg.shape[1]), BF16),
                     jax.ShapeDtypeStruct((n_groups, 1, N_HEADS), F32)],
        input_output_aliases={n_in + k: 2 + k for k in range(len(stacked))},
        compiler_params=_params(("arbitrary",)),
        name="in_projection",
    )(x, g_pre, shift, scale, wqkv, wf, wg, b_forget, *stacked)


CUMSUM_CHUNK = 256


FORGET_SPLIT = 3


def _cumsum_kernel(x_ref, o_ref, kcol_ref=None):
    t = x_ref.shape[2]
    r = lax.broadcasted_iota(jnp.int32, (CUMSUM_CHUNK, CUMSUM_CHUNK), 0)
    c = lax.broadcasted_iota(jnp.int32, (CUMSUM_CHUNK, CUMSUM_CHUNK), 1)
    upper = jnp.where(r <= c, 1.0, 0.0).astype(F32)
    row = lax.broadcasted_iota(jnp.int32, (LANES, LANES), 0)
    lane = lax.broadcasted_iota(jnp.int32, (LANES, LANES), 1)
    head, term = row % N_HEADS, row // N_HEADS
    placed = jnp.logical_and(row < FORGET_SPLIT * N_HEADS, lane == (head % 2) * FORGET_SPLIT + term)
    pad_rows = jnp.zeros((LANES - FORGET_SPLIT * N_HEADS, CUMSUM_CHUNK), F32)

    def step(i, carry):
        start = pl.multiple_of(i * CUMSUM_CHUNK, CUMSUM_CHUNK)
        seg = x_ref[0, :, pl.ds(start, CUMSUM_CHUNK)]
        cs = jnp.dot(seg, upper, precision=lax.Precision.HIGHEST, preferred_element_type=F32) + carry
        f_log2 = cs * LOG2_E
        o_ref[0, :, pl.ds(start, CUMSUM_CHUNK)] = f_log2
        if kcol_ref is None:
            return cs[:, CUMSUM_CHUNK - 1:CUMSUM_CHUNK]
        rest = -f_log2
        terms = []
        for _ in range(FORGET_SPLIT):
            part = pltpu.bitcast(pltpu.bitcast(rest, jnp.uint32) & jnp.uint32(0xFFFF0000), F32)
            terms.append(part)
            rest = rest - part
        terms_t = jnp.concatenate(terms + [pad_rows], axis=0).T.astype(BF16)
        for p in range(HEAD_PAIRS):
            select = jnp.where(jnp.logical_and(placed, head // 2 == p), 1.0, 0.0).astype(BF16)
            kcol_ref[0, p, pl.ds(start, CUMSUM_CHUNK), :] = _dot(terms_t, select).astype(BF16)
        return cs[:, CUMSUM_CHUNK - 1:CUMSUM_CHUNK]

    lax.fori_loop(0, t // CUMSUM_CHUNK, step, jnp.zeros((x_ref.shape[1], 1), F32))


def _forget_cumsum(x, key_columns):
    b, r, t = x.shape
    out_specs = [pl.BlockSpec((1, r, t), lambda i: (i, 0, 0))]
    out_shape = [jax.ShapeDtypeStruct((b, r, t), F32)]
    if key_columns:
        out_specs.append(pl.BlockSpec((1, HEAD_PAIRS, t, LANES), lambda i: (i, 0, 0, 0)))
        out_shape.append(jax.ShapeDtypeStruct((b, HEAD_PAIRS, t, LANES), BF16))
    return pl.pallas_call(
        _cumsum_kernel,
        grid=(b,),
        in_specs=[pl.BlockSpec((1, r, t), lambda i: (i, 0, 0))],
        out_specs=out_specs,
        out_shape=out_shape,
        compiler_params=_params(("arbitrary",)),
        name="forget_cumsum",
    )(x)


def _pair_layouts(cum_t, t_query_start, t_query):
    b, _, tp = cum_t.shape
    f_keys = cum_t.reshape(b, HEAD_PAIRS, 2, tp)
    f_query = jnp.swapaxes(f_keys[:, :, :, t_query_start:t_query_start + t_query], 2, 3)
    return f_keys, f_query


def _head_masks():
    lane = lax.broadcasted_iota(jnp.int32, (1, LANES), 1)
    return lane < HEAD_DIM


def _split_heads(q2, first_head):
    zero = jnp.zeros_like(q2)
    return jnp.where(first_head, q2, zero), jnp.where(first_head, zero, q2)


def _strict_lower_neg(n):
    r = lax.broadcasted_iota(jnp.int32, (n, n), 0)
    c = lax.broadcasted_iota(jnp.int32, (n, n), 1)
    return jnp.where(r > c, -1.0, 0.0).astype(BF16)


def _sb_block(qh, k2, v2, neg_tri, visible, acc_ref, r_ref, hd):
    z = _dot_nt(qh, k2)
    sp = _softplus(z)
    if visible is not None:
        sp = jnp.where(visible, sp, 0.0)
    later = _dot(sp.astype(BF16), neg_tri)
    p = jnp.exp(z - sp + later)
    if visible is not None:
        p = jnp.where(visible, p, 0.0)
    r = r_ref[hd]
    acc_ref[hd] += jnp.exp(r) * _dot(p.astype(BF16), v2)
    r_ref[hd] = r + later[:, 0:1] - sp[:, 0:1]


def _fox_block(qh, k2, v2_ones, fq, fk, allowed, acc_ref, m_ref, hd):
    s = _dot_nt(qh, k2) + fq - fk
    if allowed is not None:
        s = jnp.where(allowed, s, -jnp.inf)
    m_old = m_ref[hd]
    m_new = jnp.maximum(m_old, jnp.max(s, axis=-1, keepdims=True))
    p = jnp.exp2(s - m_new)
    acc_ref[hd] = jnp.exp2(m_old - m_new) * acc_ref[hd] + _dot(p.astype(BF16), v2_ones)
    m_ref[hd] = m_new


def _fox_finish(acc_ref, first_head):
    a, b = acc_ref[0], acc_ref[1]
    num = jnp.where(first_head, a, b)
    den = jnp.where(first_head, pltpu.roll(a, HEAD_DIM, 1), pltpu.roll(b, HEAD_DIM, 1))
    return num / den


def _with_ones(v2, first_head):
    one = jnp.ones_like(v2)
    return jnp.where(first_head, v2, one), jnp.where(first_head, one, v2)


def _sb_prompt_kernel(qt_ref, k_ref, vt_ref, o_ref, w_ref, acc_ref, r_ref, tri_ref, *, tq):
    i = pl.program_id(2)
    n_pairs = qt_ref.shape[0] // LANES
    sub = lax.broadcasted_iota(jnp.int32, (LANES, tq), 0)
    first_rows = sub < HEAD_DIM
    for pp in range(n_pairs):
        qt = qt_ref[pp * LANES:(pp + 1) * LANES, :]
        zero = jnp.zeros_like(qt)
        w_ref[2 * pp] = jnp.where(first_rows, qt, zero)
        w_ref[2 * pp + 1] = jnp.where(first_rows, zero, qt)
    acc_ref[...] = jnp.zeros_like(acc_ref)
    r_ref[...] = jnp.zeros_like(r_ref)

    @pl.when(jnp.logical_and(jnp.logical_and(pl.program_id(0) == 0, pl.program_id(1) == 0), i == 0))
    def _():
        s_idx = lax.broadcasted_iota(jnp.int32, tri_ref.shape, 0)
        j_idx = lax.broadcasted_iota(jnp.int32, tri_ref.shape, 1)
        tri_ref[...] = jnp.where(j_idx > s_idx, -1.0, 0.0).astype(BF16)

    def block(start, size, vis):
        neg_tri = tri_ref[:size, :size]
        for hd in range(2 * n_pairs):
            lanes = slice((hd // 2) * LANES, (hd // 2 + 1) * LANES)
            keys = k_ref[0, pl.ds(start, size), lanes]
            vt = vt_ref[lanes, pl.ds(start, size)]
            z = _dot(keys, w_ref[hd])
            sp = _softplus(z)
            if vis is not None:
                sp = jnp.where(vis, sp, 0.0)
            later = _dot(neg_tri, sp.astype(BF16))
            p = jnp.exp(z - sp + later)
            if vis is not None:
                p = jnp.where(vis, p, 0.0)
            r = r_ref[hd]
            acc_ref[hd] += jnp.exp(r) * _dot(vt, p.astype(BF16))
            r_ref[hd] = r + later[0:1, :] - sp[0:1, :]

    def live():
        worst = r_ref[0]
        for hd in range(1, 2 * n_pairs):
            worst = jnp.maximum(worst, r_ref[hd])
        return jnp.max(worst)

    first = jnp.maximum(i - 1, 0)
    ahead = lax.broadcasted_iota(jnp.int32, (2 * tq, tq), 0) - lax.broadcasted_iota(jnp.int32, (2 * tq, tq), 1)
    block(pl.multiple_of(first * tq, tq), 2 * tq, ahead < (i - first) * tq)

    def cond(carry):
        j, r_max = carry
        return jnp.logical_and(j >= 0, r_max > SB_DEAD_LOG)

    def body(carry):
        j, _ = carry
        block(pl.multiple_of(j * tq, tq), tq, None)
        return j - 1, live()

    lax.while_loop(cond, body, (i - 2, live()))
    for pp in range(n_pairs):
        out_t = jnp.where(first_rows, acc_ref[2 * pp], acc_ref[2 * pp + 1])
        o_ref[0, :, pp * LANES:(pp + 1) * LANES] = out_t.T.astype(o_ref.dtype)


SB_PAIRS_PER_STEP = 2


def _sb_prompt(qkv, qv_t, tq):
    b, t, _ = qkv.shape
    nq = t // tq
    group = SB_PAIRS_PER_STEP * LANES
    n_groups = WIDTH // group
    heads = 2 * SB_PAIRS_PER_STEP
    kernel = functools.partial(_sb_prompt_kernel, tq=tq)
    return pl.pallas_call(
        kernel,
        grid=(b, n_groups, nq),
        in_specs=[pl.BlockSpec((group, tq), lambda b_, p, i: (p, b_ * nq + i)),
                  pl.BlockSpec((1, t, group), lambda b_, p, i: (b_, 0, n_groups + p)),
                  pl.BlockSpec((group, t), lambda b_, p, i: (n_groups + p, b_))],
        out_specs=pl.BlockSpec((1, tq, group), lambda b_, p, i: (b_, i, p)),
        out_shape=jax.ShapeDtypeStruct((b, t, WIDTH), BF16),
        scratch_shapes=[pltpu.VMEM((heads, LANES, tq), BF16), pltpu.VMEM((heads, LANES, tq), F32),
                        pltpu.VMEM((heads, 1, tq), F32), pltpu.VMEM((2 * tq, 2 * tq), BF16)],
        compiler_params=_params(("arbitrary", "arbitrary", "arbitrary")),
        name="sb_attention_prompt",
    )(qv_t, qkv, qv_t)


FOX_DEAD_LOG2 = -150.0
FOX_BOUNDED_MAX = 40.0


def _fox_prompt_kernel(kmax_ref, fend_ref, qt_ref, k_ref, kf_ref, vt_ref, fq_ref, o_ref,
                       w_ref, acc_ref, m_ref, zb_ref, *, tq):
    bi = pl.program_id(0)
    pi = pl.program_id(1)
    i = pl.program_id(2)
    sub = lax.broadcasted_iota(jnp.int32, (LANES, tq), 0)
    first_rows = sub < HEAD_DIM
    qt = qt_ref[...]
    zero = jnp.zeros_like(qt)
    for hd in range(2):
        lo = hd * FORGET_SPLIT
        select = jnp.where(jnp.logical_and(sub >= lo, sub < lo + FORGET_SPLIT), 1.0, 0.0).astype(BF16)
        q_head = jnp.where(first_rows, qt, zero) if hd == 0 else jnp.where(first_rows, zero, qt)
        w_ref[hd, :LANES, :] = q_head
        w_ref[hd, LANES:, :] = select
        q_f32 = q_head.astype(F32)
        q_norm = jnp.sqrt(jnp.sum(q_f32 * q_f32, axis=0, keepdims=True))
        zb_ref[hd] = q_norm * (kmax_ref[bi, 2 * pi + hd] * 1.01)
    acc_ref[...] = jnp.zeros_like(acc_ref)
    m_ref[...] = jnp.full_like(m_ref, -jnp.inf)
    key_idx = lax.broadcasted_iota(jnp.int32, (tq, tq), 0)
    query_idx = lax.broadcasted_iota(jnp.int32, (tq, tq), 1)
    allowed = key_idx <= query_idx
    den_row = (HEAD_DIM, 0)

    def operands(j):
        start = pl.multiple_of(j * tq, tq)
        keys = jnp.concatenate([k_ref[0, pl.ds(start, tq), :], kf_ref[0, 0, pl.ds(start, tq), :]], axis=1)
        vt = vt_ref[:, pl.ds(start, tq)]
        one = jnp.ones_like(vt)
        first_v = lax.broadcasted_iota(jnp.int32, vt.shape, 0) < HEAD_DIM
        return keys, (jnp.where(first_v, vt, one), jnp.where(first_v, one, vt))

    def block_bounded(j, mask):
        keys, vts = operands(j)
        for hd in range(2):
            u = _dot(keys, w_ref[hd])
            if mask is not None:
                u = jnp.where(mask, u, -jnp.inf)
            p = jnp.exp2(u + (fq_ref[0, 0, hd:hd + 1, :] - zb_ref[hd]))
            acc_ref[hd] += _dot(vts[hd], p.astype(BF16))

    def alive_bounded(j):
        jc = jnp.maximum(j, 0)
        reach = []
        for hd in range(2):
            den = acc_ref[hd, den_row[hd]:den_row[hd] + 1, :]
            gap = jnp.max(fq_ref[0, 0, hd:hd + 1, :] - jnp.log2(den))
            reach.append(gap - fend_ref[bi, pi, hd, jc])
        return jnp.maximum(reach[0], reach[1]) + tq.bit_length() > FOX_DEAD_LOG2

    def block_exact(j, mask):
        keys, vts = operands(j)
        for hd in range(2):
            u = _dot(keys, w_ref[hd])
            if mask is not None:
                u = jnp.where(mask, u, -jnp.inf)
            fq = fq_ref[0, 0, hd:hd + 1, :]
            m_old = m_ref[hd]
            m_new = jnp.maximum(m_old, jnp.max(u, axis=0, keepdims=True) + fq)
            p = jnp.exp2(u + (fq - m_new))
            acc_ref[hd] = jnp.exp2(m_old - m_new) * acc_ref[hd] + _dot(vts[hd], p.astype(BF16))
            m_ref[hd] = m_new

    def alive_exact(j):
        jc = jnp.maximum(j, 0)
        reach = [jnp.max(zb_ref[hd] + fq_ref[0, 0, hd:hd + 1, :] - m_ref[hd]) - fend_ref[bi, pi, hd, jc]
                 for hd in range(2)]
        return jnp.maximum(reach[0], reach[1]) > FOX_DEAD_LOG2

    def sweep(block, alive):
        block(i, allowed)

        def cond(carry):
            j, go = carry
            return jnp.logical_and(j >= 0, go)

        def body(carry):
            j, _ = carry
            block(j, None)
            return j - 1, alive(j - 1)

        lax.while_loop(cond, body, (i - 1, alive(i - 1)))

    bounded = jnp.max(jnp.maximum(zb_ref[0], zb_ref[1])) <= FOX_BOUNDED_MAX

    @pl.when(bounded)
    def _():
        sweep(block_bounded, alive_bounded)

    @pl.when(jnp.logical_not(bounded))
    def _():
        sweep(block_exact, alive_exact)

    a, b = acc_ref[0], acc_ref[1]
    out_t = jnp.where(first_rows, a / a[HEAD_DIM:HEAD_DIM + 1, :], b / b[0:1, :])
    o_ref[0] = out_t.T.astype(o_ref.dtype)


def _fox_prompt(qkv, qv_t, cum_t, key_cols, k_max, tq):
    b, t, _ = qkv.shape
    nq = t // tq
    f_rows = cum_t.reshape(b, HEAD_PAIRS, 2, t)
    f_block_end = f_rows[:, :, :, tq - 1::tq]
    smem = pl.BlockSpec(memory_space=pltpu.SMEM)
    kernel = functools.partial(_fox_prompt_kernel, tq=tq)
    return pl.pallas_call(
        kernel,
        grid=(b, HEAD_PAIRS, nq),
        in_specs=[smem, smem,
                  pl.BlockSpec((LANES, tq), lambda b_, p, i: (2 * HEAD_PAIRS + p, b_ * nq + i)),
                  pl.BlockSpec((1, t, LANES), lambda b_, p, i: (b_, 0, 4 * HEAD_PAIRS + p)),
                  pl.BlockSpec((1, 1, t, LANES), lambda b_, p, i: (b_, p, 0, 0)),
                  pl.BlockSpec((LANES, t), lambda b_, p, i: (3 * HEAD_PAIRS + p, b_)),
                  pl.BlockSpec((1, 1, 2, tq), lambda b_, p, i: (b_, p, 0, i))],
        out_specs=pl.BlockSpec((1, tq, LANES), lambda b_, p, i: (b_, i, p)),
        out_shape=jax.ShapeDtypeStruct((b, t, WIDTH), BF16),
        scratch_shapes=[pltpu.VMEM((2, 2 * LANES, tq), BF16), pltpu.VMEM((2, LANES, tq), F32),
                        pltpu.VMEM((2, 1, tq), F32), pltpu.VMEM((2, 1, tq), F32)],
        compiler_params=_params(("arbitrary", "arbitrary", "arbitrary")),
        name="fox_attention_prompt",
    )(k_max, f_block_end, qv_t, qkv, key_cols, qv_t, f_rows)


def _sample_attn_kernel(qsb_ref, ksb_ref, vsb_ref, qfx_ref, kfx_ref, vfx_ref,
                        csk_ref, csv_ref, cfk_ref, cfv_ref, fq_ref, fk_ref,
                        osb_ref, ofx_ref, acc_ref, st_ref, *, tk):
    t_new = qsb_ref.shape[1]
    n_past = csk_ref.shape[1]
    n_blocks = n_past // tk
    first_head = _head_masks()
    row = lax.broadcasted_iota(jnp.int32, (t_new, t_new), 0)
    col = lax.broadcasted_iota(jnp.int32, (t_new, t_new), 1)

    qs = _split_heads(qsb_ref[0], first_head)
    acc_ref[...] = jnp.zeros_like(acc_ref)
    st_ref[...] = jnp.zeros_like(st_ref)
    tri_new = _strict_lower_neg(t_new)
    tri_past = _strict_lower_neg(tk)
    for hd in range(2):
        _sb_block(qs[hd], ksb_ref[0], vsb_ref[0], tri_new, col < row, acc_ref, st_ref, hd)

    def live():
        return jnp.max(jnp.maximum(st_ref[0], st_ref[1]))

    def sb_cond(carry):
        j, r_max = carry
        return jnp.logical_and(j >= 0, r_max > SB_DEAD_LOG)

    def sb_body(carry):
        j, _ = carry
        start = pl.multiple_of(j * tk, tk)
        k2 = csk_ref[0, pl.ds(start, tk), :].astype(BF16)
        v2 = csv_ref[0, pl.ds(start, tk), :].astype(BF16)
        for hd in range(2):
            _sb_block(qs[hd], k2, v2, tri_past, None, acc_ref, st_ref, hd)
        return j - 1, live()

    lax.while_loop(sb_cond, sb_body, (n_blocks - 1, live()))
    osb_ref[0] = jnp.where(first_head, acc_ref[0], acc_ref[1]).astype(osb_ref.dtype)

    qs = _split_heads(qfx_ref[0], first_head)
    acc_ref[...] = jnp.zeros_like(acc_ref)
    st_ref[...] = jnp.full_like(st_ref, -jnp.inf)
    fq = fq_ref[0, 0]
    vs = _with_ones(vfx_ref[0], first_head)
    for hd in range(2):
        fk = fk_ref[0, 0, hd:hd + 1, n_past:n_past + t_new]
        _fox_block(qs[hd], kfx_ref[0], vs[hd], fq[:, hd:hd + 1], fk, col <= row, acc_ref, st_ref, hd)

    k2 = cfk_ref[0].astype(BF16)
    vs_ = _with_ones(cfv_ref[0].astype(BF16), first_head)
    for hd in range(2):
        fk = fk_ref[0, 0, hd:hd + 1, 0:n_past]
        _fox_block(qs[hd], k2, vs_[hd], fq[:, hd:hd + 1], fk, None, acc_ref, st_ref, hd)
    ofx_ref[0] = _fox_finish(acc_ref, first_head).astype(ofx_ref.dtype)


def _sample_attention(qkv, cache_sb_k, cache_sb_v, cache_fox_k, cache_fox_v, f_query, f_keys, tk):
    b, t_new, _ = qkv.shape
    n_past = cache_sb_k.shape[1]
    tp = f_keys.shape[3]
    new = lambda c: pl.BlockSpec((1, t_new, LANES), lambda b_, p: (b_, 0, c * HEAD_PAIRS + p))
    past = pl.BlockSpec((1, n_past, LANES), lambda b_, p: (b_, 0, p))
    out = pl.BlockSpec((1, t_new, LANES), lambda b_, p: (b_, 0, p))
    kernel = functools.partial(_sample_attn_kernel, tk=tk)
    return pl.pallas_call(
        kernel,
        grid=(b, HEAD_PAIRS),
        in_specs=[new(0), new(1), new(2), new(3), new(4), new(5), past, past, past, past,
                  pl.BlockSpec((1, 1, t_new, 2), lambda b_, p: (b_, p, 0, 0)),
                  pl.BlockSpec((1, 1, 2, tp), lambda b_, p: (b_, p, 0, 0))],
        out_specs=[out, out],
        out_shape=[jax.ShapeDtypeStruct((b, t_new, WIDTH), BF16)] * 2,
        scratch_shapes=[pltpu.VMEM((2, t_new, LANES), F32), pltpu.VMEM((2, t_new, 1), F32)],
        compiler_params=_params(("arbitrary", "arbitrary")),
        name="attention_sample",
    )(qkv, qkv, qkv, qkv, qkv, qkv, cache_sb_k, cache_sb_v, cache_fox_k, cache_fox_v, f_query, f_keys)


def _mix_out_kernel(x_ref, osb_ref, ofx_ref, sg_ref, gate_ref, g_ref, wsb_ref, wfx_ref, wo_ref, o_ref):
    d = x_ref.shape[1]
    merged = (sg_ref[:, :d].astype(F32) * _dot(osb_ref[...], wsb_ref[...])
              + sg_ref[:, d:].astype(F32) * _dot(ofx_ref[...], wfx_ref[...]))
    y = _dot(merged.astype(BF16), wo_ref[...])
    o_ref[...] = x_ref[...] + gate_ref[0] * _rms(y, g_ref[...])


def _mix_out(x, o_sb, o_fx, sg, gate, g_post, w_sb_out, w_fox_out, w_out, tm):
    n, d = x.shape
    tiles_per_group = (n // tm) // gate.shape[0]
    const = lambda i: (0, 0)
    row = lambda i: (i, 0)
    return pl.pallas_call(
        _mix_out_kernel,
        grid=(n // tm,),
        in_specs=[pl.BlockSpec((tm, d), row), pl.BlockSpec((tm, WIDTH), row), pl.BlockSpec((tm, WIDTH), row),
                  pl.BlockSpec((tm, 2 * d), row), _mod_spec(gate, tm, tiles_per_group),
                  pl.BlockSpec((1, d), const), pl.BlockSpec(w_sb_out.shape, const),
                  pl.BlockSpec(w_fox_out.shape, const), pl.BlockSpec(w_out.shape, const)],
        out_specs=pl.BlockSpec((tm, d), row),
        out_shape=jax.ShapeDtypeStruct((n, d), F32),
        compiler_params=_params(("arbitrary",)),
        name="mix_out",
    )(x, o_sb, o_fx, sg, gate, g_post, w_sb_out, w_fox_out, w_out)


def _swiglu_chunk(hb, wa, wb, wd):
    a = _dot(hb, wa)
    b = _dot(hb, wb)
    return a * _sigmoid(a) * b, wd


def _ffn_kernel(x_ref, gpre_ref, shift_ref, scale_ref, gate_ref, gpost_ref, wa_ref, wb_ref, wd_ref,
                o_ref, h_ref, acc_ref):
    f = pl.program_id(1)

    @pl.when(f == 0)
    def _():
        h = _rms(x_ref[...], gpre_ref[...]) * (1.0 + scale_ref[0]) + shift_ref[0]
        h_ref[...] = h.astype(BF16)
        acc_ref[...] = jnp.zeros_like(acc_ref)

    hb = h_ref[...]
    a = _dot(hb, wa_ref[...])
    b = _dot(hb, wb_ref[...])
    act = a * _sigmoid(a) * b
    acc_ref[...] += _dot(act.astype(BF16), wd_ref[...])

    @pl.when(f == pl.num_programs(1) - 1)
    def _():
        o_ref[...] = x_ref[...] + gate_ref[0] * _rms(acc_ref[...], gpost_ref[...])


def _ffn(x, g_pre, shift, scale, gate, g_post, w_gate_up, w_down, tm, tf):
    n, d = x.shape
    d_ff = w_down.shape[0]
    nf = d_ff // tf
    tiles_per_group = (n // tm) // shift.shape[0]
    const = lambda i, f: (0, 0)
    row = lambda i, f: (i, 0)
    return pl.pallas_call(
        _ffn_kernel,
        grid=(n // tm, nf),
        in_specs=[pl.BlockSpec((tm, d), row), pl.BlockSpec((1, d), const),
                  _mod_spec(shift, tm, tiles_per_group), _mod_spec(scale, tm, tiles_per_group),
                  _mod_spec(gate, tm, tiles_per_group), pl.BlockSpec((1, d), const),
                  pl.BlockSpec((d, tf), lambda i, f: (0, f)),
                  pl.BlockSpec((d, tf), lambda i, f: (0, nf + f)),
                  pl.BlockSpec((tf, d), lambda i, f: (f, 0))],
        out_specs=pl.BlockSpec((tm, d), row),
        out_shape=jax.ShapeDtypeStruct((n, d), F32),
        scratch_shapes=[pltpu.VMEM((tm, d), BF16), pltpu.VMEM((tm, d), F32)],
        compiler_params=_params(("arbitrary", "arbitrary")),
        name="ffn_dense",
    )(x, g_pre, shift, scale, gate, g_post, w_gate_up, w_gate_up, w_down)


def _route_kernel(x_ref, gpre_ref, shift_ref, scale_ref, wr_ref, h_ref, idx_ref, w_ref, rank_ref, count_ref,
                  *, n_experts):
    tm = x_ref.shape[0]

    @pl.when(pl.program_id(0) == 0)
    def _():
        count_ref[...] = jnp.zeros_like(count_ref)

    h = _rms(x_ref[...], gpre_ref[...]) * (1.0 + scale_ref[0]) + shift_ref[0]
    h_ref[...] = h
    logits = jnp.dot(h, wr_ref[...], precision=lax.Precision.HIGHEST, preferred_element_type=F32)[:, :n_experts]
    idx = lax.broadcasted_iota(jnp.int32, logits.shape, 1)
    m1 = jnp.max(logits, axis=-1, keepdims=True)
    i1 = jnp.min(jnp.where(logits == m1, idx, n_experts), axis=-1, keepdims=True)
    rest = jnp.where(idx == i1, -jnp.inf, logits)
    m2 = jnp.max(rest, axis=-1, keepdims=True)
    i2 = jnp.min(jnp.where(rest == m2, idx, n_experts), axis=-1, keepdims=True)
    e2 = jnp.exp(m2 - m1)
    slot = lax.broadcasted_iota(jnp.int32, idx_ref.shape, 1)
    idx_ref[...] = jnp.where(slot == 0, i1, i2)
    w_ref[...] = jnp.where(slot == 0, 1.0 / (1.0 + e2), e2 / (1.0 + e2))
    lane = lax.broadcasted_iota(jnp.int32, (tm, LANES), 1)
    first, second = lane == i1, lane == i2
    chosen = jnp.where(jnp.logical_or(first, second), 1.0, 0.0)
    t_row = lax.broadcasted_iota(jnp.int32, (tm, tm), 0)
    t_col = lax.broadcasted_iota(jnp.int32, (tm, tm), 1)
    earlier = jnp.where(t_col < t_row, 1.0, 0.0).astype(BF16)
    before = count_ref[0:1, :] + _dot(earlier, chosen.astype(BF16))
    rank1 = jnp.sum(jnp.where(first, before, 0.0), axis=-1, keepdims=True)
    rank2 = jnp.sum(jnp.where(second, before, 0.0), axis=-1, keepdims=True)
    rank_ref[...] = jnp.where(slot == 0, rank1, rank2).astype(jnp.int32)
    count_ref[...] = count_ref[...] + jnp.sum(chosen, axis=0, keepdims=True)


def _route(x, g_pre, shift, scale, w_router, n_experts, tm):
    n, d = x.shape
    tiles_per_group = (n // tm) // shift.shape[0]
    const = lambda i: (0, 0)
    row = lambda i: (i, 0)
    return pl.pallas_call(
        functools.partial(_route_kernel, n_experts=n_experts),
        grid=(n // tm,),
        in_specs=[pl.BlockSpec((tm, d), row), pl.BlockSpec((1, d), const),
                  _mod_spec(shift, tm, tiles_per_group), _mod_spec(scale, tm, tiles_per_group),
                  pl.BlockSpec(w_router.shape, const)],
        out_specs=[pl.BlockSpec((tm, d), row), pl.BlockSpec((tm, TOP_K), row), pl.BlockSpec((tm, TOP_K), row),
                   pl.BlockSpec((tm, TOP_K), row), pl.BlockSpec((8, LANES), const)],
        out_shape=[jax.ShapeDtypeStruct((n, d), F32), jax.ShapeDtypeStruct((n, TOP_K), jnp.int32),
                   jax.ShapeDtypeStruct((n, TOP_K), F32), jax.ShapeDtypeStruct((n, TOP_K), jnp.int32),
                   jax.ShapeDtypeStruct((8, LANES), F32)],
        compiler_params=_params(("arbitrary",)),
        name="moe_route",
    )(x, g_pre, shift, scale, w_router)


def _dispatch_tables(top_idx, rank, counts, n_experts, tm):
    n = top_idx.shape[0]
    n_assign = n * TOP_K
    expert = top_idx.reshape(n_assign)
    padded = (counts + tm - 1) // tm * tm
    group_end = jnp.cumsum(padded)
    group_start = group_end - padded
    onehot = expert[:, None] == jnp.arange(n_experts, dtype=jnp.int32)[None, :]
    pos = jnp.sum(jnp.where(onehot, group_start[None, :], 0), axis=1) + rank.reshape(n_assign)
    p_rows = (-(-n_assign // tm) + n_experts) * tm
    src_token = jnp.zeros((p_rows,), jnp.int32).at[pos].set(jnp.arange(n_assign, dtype=jnp.int32) // TOP_K)
    tile_start = jnp.arange(p_rows // tm, dtype=jnp.int32) * tm
    tile_expert = jnp.minimum(jnp.sum(tile_start[:, None] >= group_end[None, :], axis=1), n_experts - 1)
    n_used = (group_end[-1] // tm).reshape(1)
    return src_token, tile_expert.astype(jnp.int32), n_used.astype(jnp.int32), pos.astype(jnp.int32)


def _gather_rows(src_hbm, dst, sem, index_of_row, n_rows):
    def issue(r, carry):
        pltpu.make_async_copy(src_hbm.at[pl.ds(index_of_row(r), 1), :], dst.at[pl.ds(r, 1), :], sem).start()
        return carry

    lax.fori_loop(0, n_rows, issue, 0, unroll=8)


def _wait_rows(src_hbm, dst, sem):
    pltpu.make_async_copy(src_hbm.at[pl.ds(0, dst.shape[0]), :], dst, sem).wait()


def _experts_kernel(src_ref, expert_ref, used_ref, h_hbm, wa_ref, wb_ref, wd_ref, o_ref,
                    rows_ref, xb_ref, acc_ref, sem, *, tm):
    i = pl.program_id(0)
    f = pl.program_id(1)
    nf = pl.num_programs(1)
    n_used = used_ref[0]
    slot = lax.rem(i, 2)
    active = i < n_used
    rows_per_step = tm // nf

    @pl.when(jnp.logical_and(f == 0, i == 0))
    def _():
        _gather_rows(h_hbm, rows_ref.at[0], sem.at[0], lambda r: src_ref[r], tm)

    @pl.when(jnp.logical_and(f == 0, i <= n_used))
    def _():
        _wait_rows(h_hbm, rows_ref.at[slot], sem.at[slot])

    @pl.when(jnp.logical_and(f == 0, active))
    def _():
        xb_ref[...] = rows_ref[slot].astype(BF16)
        acc_ref[...] = jnp.zeros_like(acc_ref)

    @pl.when(active)
    def _():
        base = (i + 1) * tm + f * rows_per_step
        nxt = rows_ref.at[1 - slot]
        for r in range(rows_per_step):
            row = f * rows_per_step + r
            pltpu.make_async_copy(h_hbm.at[pl.ds(src_ref[base + r], 1), :], nxt.at[pl.ds(row, 1), :],
                                  sem.at[1 - slot]).start()
        xb = xb_ref[...]
        a = _dot(xb, wa_ref[0])
        b = _dot(xb, wb_ref[0])
        act = a * _sigmoid(a) * b
        acc_ref[...] += _dot(act.astype(BF16), wd_ref[0])

    last = f == pl.num_programs(1) - 1

    @pl.when(jnp.logical_and(last, active))
    def _():
        o_ref[...] = acc_ref[...]

    @pl.when(jnp.logical_and(last, jnp.logical_not(active)))
    def _():
        o_ref[...] = jnp.zeros_like(o_ref)


def _experts(h, src_token, tile_expert, n_used, w_gate_up, w_down, tm, tf):
    p_rows = src_token.shape[0]
    d = h.shape[1]
    nf = w_down.shape[1] // tf
    ff = lambda i, f, used: jnp.where(i < used[0], f, nf - 1)
    grid_spec = pltpu.PrefetchScalarGridSpec(
        num_scalar_prefetch=3,
        grid=(p_rows // tm, nf),
        in_specs=[pl.BlockSpec(memory_space=pl.ANY),
                  pl.BlockSpec((1, d, tf), lambda i, f, src, ex, used: (ex[i], 0, ff(i, f, used))),
                  pl.BlockSpec((1, d, tf), lambda i, f, src, ex, used: (ex[i], 0, nf + ff(i, f, used))),
                  pl.BlockSpec((1, tf, d), lambda i, f, src, ex, used: (ex[i], ff(i, f, used), 0))],
        out_specs=pl.BlockSpec((tm, d), lambda i, f, src, ex, used: (i, 0)),
        scratch_shapes=[pltpu.VMEM((2, tm, d), F32), pltpu.VMEM((tm, d), BF16), pltpu.VMEM((tm, d), F32),
                        pltpu.SemaphoreType.DMA((2,))])
    return pl.pallas_call(
        functools.partial(_experts_kernel, tm=tm),
        grid_spec=grid_spec,
        out_shape=jax.ShapeDtypeStruct((p_rows, d), F32),
        compiler_params=_params(("arbitrary", "arbitrary")),
        name="moe_experts",
    )(src_token, tile_expert, n_used, h, w_gate_up, w_gate_up, w_down)


def _combine_kernel(pos_ref, y_hbm, x_ref, w_ref, gate_ref, gpost_ref, o_ref, rows_ref, sem, *, tm):
    i = pl.program_id(0)
    slot = lax.rem(i, 2)

    def gather(tile, into):
        for k in range(TOP_K):
            _gather_rows(y_hbm, rows_ref.at[into, k], sem.at[into],
                         lambda r, k=k: pos_ref[(tile * tm + r) * TOP_K + k], tm)

    @pl.when(i == 0)
    def _():
        gather(0, 0)

    for k in range(TOP_K):
        _wait_rows(y_hbm, rows_ref.at[slot, k], sem.at[slot])

    @pl.when(i + 1 < pl.num_programs(0))
    def _():
        gather(i + 1, 1 - slot)

    w = w_ref[...]
    mixed = w[:, 0:1] * rows_ref[slot, 0] + w[:, 1:2] * rows_ref[slot, 1]
    o_ref[...] = x_ref[...] + gate_ref[0] * _rms(mixed, gpost_ref[...])


def _combine(y, pos, top_w, x, gate, g_post, tm):
    n, d = x.shape
    tiles_per_group = (n // tm) // gate.shape[0]
    grid_spec = pltpu.PrefetchScalarGridSpec(
        num_scalar_prefetch=1,
        grid=(n // tm,),
        in_specs=[pl.BlockSpec(memory_space=pl.ANY),
                  pl.BlockSpec((tm, d), lambda i, pos_: (i, 0)),
                  pl.BlockSpec((tm, TOP_K), lambda i, pos_: (i, 0)),
                  _mod_spec(gate, tm, tiles_per_group),
                  pl.BlockSpec((1, d), lambda i, pos_: (0, 0))],
        out_specs=pl.BlockSpec((tm, d), lambda i, pos_: (i, 0)),
        scratch_shapes=[pltpu.VMEM((2, TOP_K, tm, d), F32), pltpu.SemaphoreType.DMA((2,))])
    return pl.pallas_call(
        functools.partial(_combine_kernel, tm=tm),
        grid_spec=grid_spec,
        out_shape=jax.ShapeDtypeStruct((n, d), F32),
        compiler_params=_params(("arbitrary",)),
        name="moe_combine",
    )(pos, y, x, top_w, gate, g_post)


def _moe(x, g_pre, shift, scale, gate, g_post, w_router, w_gate_up, w_down, n_experts, tm_route, tm, tf, tm_out):
    h, top_idx, top_w, rank, counts = _route(x, g_pre, shift, scale, w_router, n_experts, tm_route)
    counts = counts[0, :n_experts].astype(jnp.int32)
    src_token, tile_expert, n_used, pos = _dispatch_tables(top_idx, rank, counts, n_experts, tm)
    y = _experts(h, src_token, tile_expert, n_used, w_gate_up, w_down, tm, tf)
    return _combine(y, pos, top_w, x, gate, g_post, tm_out)


def _row_tile(n, want):
    return want if n % want == 0 else n


def kernel(x_prompt, x_sample, c_prompt, c_sample, cache_sb_k, cache_sb_v, cache_fox_k, cache_fox_v, cache_fox_logf, w_mod, b_mod, g_pre_mix, g_post_mix, g_pre_ffn, g_post_ffn, w_in, b_forget, w_sb_out, w_fox_out, w_out, w_ffn_gate_up, w_ffn_down, w_router, w_moe_gate_up, w_moe_down):
    bsz, seq, d = x_prompt.shape
    dec_b, dec_t, _ = x_sample.shape
    depth = w_mod.shape[0]
    n_past = cache_sb_k.shape[2]
    n_experts = w_router.shape[2]
    n_p, n_s = bsz * seq, dec_b * dec_t
    tq = _row_tile(seq, 256)
    past_chunk = _row_tile(n_past, 256)

    c_all = jnp.concatenate([c_prompt, c_sample], axis=0)
    c_rows = -(-c_all.shape[0] // 8) * 8
    c_all = jnp.pad(c_all, ((0, c_rows - c_all.shape[0]), (0, 0)))
    mod = _modulation(c_all, w_mod, b_mod)

    xp = x_prompt.reshape(n_p, d)
    xs = x_sample.reshape(n_s, d)
    stacked_p = stacked_s = None
    for l in range(depth):
        mod_p = mod[l, :bsz].reshape(bsz, 1, 6, d)
        mod_s = jnp.repeat(mod[l, bsz:bsz + dec_b].reshape(dec_b, 6, d), dec_t, axis=0)[None]
        mp = [mod_p[:, :, i] for i in range(6)]
        ms = [mod_s[:, :, i] for i in range(6)]
        vec = lambda a: a[l].reshape(1, -1)

        wqkv = w_in[l, :, :6 * WIDTH].astype(BF16)
        wf = jnp.pad(w_in[l, :, 6 * WIDTH:6 * WIDTH + N_HEADS], ((0, 0), (0, LANES - N_HEADS))).astype(BF16)
        wg = w_in[l, :, 6 * WIDTH + N_HEADS:].astype(BF16)
        bf = b_forget[l].reshape(1, N_HEADS)
        wsb, wfx, wo = w_sb_out[l].astype(BF16), w_fox_out[l].astype(BF16), w_out[l].astype(BF16)

        qkv_p, qvt_p, *stacked_p, sg_p, ksq_p = _in_projection(
            xp, vec(g_pre_mix), mp[0], mp[1], wqkv, wf, wg, bf, _row_tile(seq, 256), l, depth, stacked_p)
        lf_p = stacked_p[4][l]
        cum_p, key_cols_p = _forget_cumsum(jnp.swapaxes(lf_p.reshape(bsz, seq, N_HEADS), 1, 2), True)
        qkv_p3 = qkv_p.reshape(bsz, seq, 6 * WIDTH)
        osb_p = _sb_prompt(qkv_p3, qvt_p, tq)
        ofx_p = _fox_prompt(qkv_p3, qvt_p, cum_p, key_cols_p, jnp.sqrt(ksq_p.reshape(bsz, N_HEADS)),
                            _row_tile(seq, 512))
        xp = _mix_out(xp, osb_p.reshape(n_p, WIDTH), ofx_p.reshape(n_p, WIDTH), sg_p, mp[2], vec(g_post_mix),
                      wsb, wfx, wo, _row_tile(seq, 512))

        qkv_s, _, *stacked_s, sg_s, _ = _in_projection(
            xs, vec(g_pre_mix), ms[0], ms[1], wqkv, wf, wg, bf, n_s, l, depth, stacked_s)
        lf_s = stacked_s[4][l]
        t_all = n_past + dec_t
        t_pad = -(-t_all // CUMSUM_CHUNK) * CUMSUM_CHUNK
        lf_all = jnp.concatenate([cache_fox_logf[l].astype(F32), lf_s.reshape(dec_b, dec_t, N_HEADS),
                                  jnp.zeros((dec_b, t_pad - t_all, N_HEADS), F32)], axis=1)
        (cum_s,) = _forget_cumsum(jnp.swapaxes(lf_all, 1, 2), False)
        fk_s, fq_s = _pair_layouts(cum_s, n_past, dec_t)
        cache = lambda a: a[l].reshape(dec_b, n_past, WIDTH)
        osb_s, ofx_s = _sample_attention(qkv_s.reshape(dec_b, dec_t, 6 * WIDTH), cache(cache_sb_k), cache(cache_sb_v),
                                         cache(cache_fox_k), cache(cache_fox_v), fq_s, fk_s, past_chunk)
        xs = _mix_out(xs, osb_s.reshape(n_s, WIDTH), ofx_s.reshape(n_s, WIDTH), sg_s, ms[2], vec(g_post_mix),
                      wsb, wfx, wo, n_s)

        if l % 2 == 0:
            wgu, wd = w_ffn_gate_up[l // 2].astype(BF16), w_ffn_down[l // 2].astype(BF16)
            tf = _row_tile(wd.shape[0], 1408)
            xp = _ffn(xp, vec(g_pre_ffn), mp[3], mp[4], mp[5], vec(g_post_ffn), wgu, wd, _row_tile(seq, 512), tf)
            xs = _ffn(xs, vec(g_pre_ffn), ms[3], ms[4], ms[5], vec(g_post_ffn), wgu, wd, n_s, tf)
        else:
            wr = jnp.pad(w_router[l // 2], ((0, 0), (0, LANES - n_experts)))
            wgu, wd = w_moe_gate_up[l // 2].astype(BF16), w_moe_down[l // 2].astype(BF16)
            tf = _row_tile(wd.shape[1], 896)
            xp = _moe(xp, vec(g_pre_ffn), mp[3], mp[4], mp[5], vec(g_post_ffn), wr, wgu, wd, n_experts,
                      _row_tile(seq, 512), 1024, tf, _row_tile(seq, 256))
            xs = _moe(xs, vec(g_pre_ffn), ms[3], ms[4], ms[5], vec(g_post_ffn), wr, wgu, wd, n_experts,
                      n_s, 128, tf, n_s)

    split_p = [a.reshape(depth, bsz, seq, *a.shape[2:]) for a in stacked_p]
    split_s = [a.reshape(depth, dec_b, dec_t, *a.shape[2:]) for a in stacked_s]
    return (xp.reshape(bsz, seq, d), xs.reshape(dec_b, dec_t, d), *split_p, *split_s)
```

```python
import functools

import jax
import jax.numpy as jnp
from jax import lax
from jax.experimental import pallas as pl
from jax.experimental.pallas import tpu as pltpu

F32 = jnp.float32
BF16 = jnp.bfloat16

HEAD_DIM = 64
N_HEADS = 8
LANES = 128
HEAD_PAIRS = N_HEADS * HEAD_DIM // LANES
WIDTH = N_HEADS * HEAD_DIM
TOP_K = 2
RMS_EPS = 1e-6
LOG2_E = 1.4426950408889634
SB_DEAD_LOG = -104.0
VMEM_LIMIT = 56 * 1024 * 1024


def _params(sem, vmem=VMEM_LIMIT):
    return pltpu.CompilerParams(dimension_semantics=sem, vmem_limit_bytes=vmem)


def _dot(a, b):
    return jnp.dot(a, b, preferred_element_type=F32)


def _dot_nt(a, b):
    return lax.dot_general(a, b, (((1,), (1,)), ((), ())), preferred_element_type=F32)


def _sigmoid(x):
    return 1.0 / (1.0 + jnp.exp(-x))


def _softplus(x):
    return jnp.maximum(x, 0.0) + jnp.log(1.0 + jnp.exp(-jnp.abs(x)))


def _rms(x, g):
    return x * lax.rsqrt(jnp.mean(x * x, axis=-1, keepdims=True) + RMS_EPS) * g


def _mod_kernel(c_ref, w_ref, b_ref, o_ref):
    c = c_ref[...]
    s = c * _sigmoid(c)
    o_ref[0] = jnp.dot(s, w_ref[0], precision=lax.Precision.HIGHEST, preferred_element_type=F32) + b_ref[0]


def _modulation(c_all, w_mod, b_mod):
    depth, d, d6 = w_mod.shape
    rows = c_all.shape[0]
    tn = 1024
    return pl.pallas_call(
        _mod_kernel,
        grid=(depth, d6 // tn),
        in_specs=[pl.BlockSpec((rows, d), lambda l, j: (0, 0)),
                  pl.BlockSpec((1, d, tn), lambda l, j: (l, 0, j)),
                  pl.BlockSpec((1, 1, tn), lambda l, j: (l, 0, j))],
        out_specs=pl.BlockSpec((1, rows, tn), lambda l, j: (l, 0, j)),
        out_shape=jax.ShapeDtypeStruct((depth, rows, d6), F32),
        compiler_params=_params(("arbitrary", "arbitrary")),
        name="modulation",
    )(c_all, w_mod, b_mod.reshape(depth, 1, d6))


def _mod_spec(mod, tm, tiles_per_group):
    _, r, d = mod.shape
    return pl.BlockSpec((1, r, d), lambda i, *_: (i // tiles_per_group, 0, 0))


TRANSPOSED_SECTIONS = (0, 2, 3, 5)


def _inproj_kernel(x_ref, g_ref, shift_ref, scale_ref, wqkv_ref, wf_ref, wg_ref, bf_ref, *refs, tiles_per_group):
    qkv_ref, qvt_ref, ksb_ref, vsb_ref, kfx_ref, vfx_ref, lf_ref, sg_ref, ksq_ref = refs[-9:]
    tm = x_ref.shape[0]
    h = _rms(x_ref[...], g_ref[...]) * (1.0 + scale_ref[0]) + shift_ref[0]
    hb = h.astype(BF16)
    f32_outs = {1: ksb_ref, 2: vsb_ref, 4: kfx_ref, 5: vfx_ref}
    for c in range(6):
        cols = slice(c * WIDTH, (c + 1) * WIDTH)
        acc = _dot(hb, wqkv_ref[:, cols])
        if c in f32_outs:
            by_head = acc.reshape(tm, N_HEADS, HEAD_DIM)
            f32_outs[c][0] = by_head
            if c == 4:
                col = lax.broadcasted_iota(jnp.int32, (WIDTH, LANES), 0)
                out = lax.broadcasted_iota(jnp.int32, (WIDTH, LANES), 1)
                head_of = jnp.where(col // HEAD_DIM == out, 1.0, 0.0).astype(BF16)
                per_head = _dot((acc * acc).astype(BF16), head_of)
                ksq = jnp.max(per_head, axis=0, keepdims=True)[:, :N_HEADS]
                first_tile = pl.program_id(0) % tiles_per_group == 0

                @pl.when(first_tile)
                def _():
                    ksq_ref[0] = ksq

                @pl.when(jnp.logical_not(first_tile))
                def _():
                    ksq_ref[0] = jnp.maximum(ksq_ref[0], ksq)
        if c in (0, 3):
            acc = acc * (HEAD_DIM ** -0.5 * (LOG2_E if c == 3 else 1.0))
        qkv_ref[:, cols] = acc.astype(BF16)
        if c in TRANSPOSED_SECTIONS:
            t = TRANSPOSED_SECTIONS.index(c)
            qvt_ref[t * WIDTH:(t + 1) * WIDTH, :] = acc.T.astype(BF16)
    f = _dot(hb, wf_ref[...])[:, :N_HEADS] + bf_ref[...]
    lf_ref[0] = -_softplus(-f)
    for c in range(wg_ref.shape[1] // WIDTH):
        cols = slice(c * WIDTH, (c + 1) * WIDTH)
        sg_ref[:, cols] = _sigmoid(_dot(hb, wg_ref[:, cols])).astype(BF16)


def _in_projection(x, g_pre, shift, scale, wqkv, wf, wg, b_forget, tm, layer, depth, stacked):
    n, d = x.shape
    tiles_per_group = (n // tm) // shift.shape[0]
    const = lambda i: (0, 0)
    row = lambda i: (i, 0)
    heads_spec = pl.BlockSpec((1, tm, N_HEADS, HEAD_DIM), lambda i: (layer, i, 0, 0))
    heads_shape = jax.ShapeDtypeStruct((depth, n, N_HEADS, HEAD_DIM), F32)
    n_in = 8
    stacked = () if stacked is None else tuple(stacked)
    n_groups = shift.shape[0]
    return pl.pallas_call(
        functools.partial(_inproj_kernel, tiles_per_group=tiles_per_group),
        grid=(n // tm,),
        in_specs=[pl.BlockSpec((tm, d), row),
                  pl.BlockSpec((1, d), const),
                  _mod_spec(shift, tm, tiles_per_group),
                  _mod_spec(scale, tm, tiles_per_group),
                  pl.BlockSpec(wqkv.shape, const),
                  pl.BlockSpec(wf.shape, const),
                  pl.BlockSpec(wg.shape, const),
                  pl.BlockSpec((1, N_HEADS), const)] + [pl.BlockSpec(memory_space=pl.ANY)] * len(stacked),
        out_specs=[pl.BlockSpec((tm, 6 * WIDTH), row), pl.BlockSpec((len(TRANSPOSED_SECTIONS) * WIDTH, tm), lambda i: (0, i))]
                  + [heads_spec] * 4
                  + [pl.BlockSpec((1, tm, N_HEADS), lambda i: (layer, i, 0)), pl.BlockSpec((tm, wg.shape[1]), row),
                     pl.BlockSpec((1, 1, N_HEADS), lambda i: (i // tiles_per_group, 0, 0))],
        out_shape=[jax.ShapeDtypeStruct((n, 6 * WIDTH), BF16),
                   jax.ShapeDtypeStruct((len(TRANSPOSED_SECTIONS) * WIDTH, n), BF16)] + [heads_shape] * 4
                  + [jax.ShapeDtypeStruct((depth, n, N_HEADS), F32), jax.ShapeDtypeStruct((n, wg.shape[1]), BF16),
                     jax.ShapeDtypeStruct((n_groups, 1, N_HEADS), F32)],
        input_output_aliases={n_in + k: 2 + k for k in range(len(stacked))},
        compiler_params=_params(("arbitrary",)),
        name="in_projection",
    )(x, g_pre, shift, scale, wqkv, wf, wg, b_forget, *stacked)


CUMSUM_CHUNK = 256


FORGET_SPLIT = 3


def _cumsum_kernel(x_ref, o_ref, kcol_ref=None):
    t = x_ref.shape[2]
    r = lax.broadcasted_iota(jnp.int32, (CUMSUM_CHUNK, CUMSUM_CHUNK), 0)
    c = lax.broadcasted_iota(jnp.int32, (CUMSUM_CHUNK, CUMSUM_CHUNK), 1)
    upper = jnp.where(r <= c, 1.0, 0.0).astype(F32)
    row = lax.broadcasted_iota(jnp.int32, (LANES, LANES), 0)
    lane = lax.broadcasted_iota(jnp.int32, (LANES, LANES), 1)
    head, term = row % N_HEADS, row // N_HEADS
    placed = jnp.logical_and(row < FORGET_SPLIT * N_HEADS, lane == (head % 2) * FORGET_SPLIT + term)
    pad_rows = jnp.zeros((LANES - FORGET_SPLIT * N_HEADS, CUMSUM_CHUNK), F32)

    def step(i, carry):
        start = pl.multiple_of(i * CUMSUM_CHUNK, CUMSUM_CHUNK)
        seg = x_ref[0, :, pl.ds(start, CUMSUM_CHUNK)]
        cs = jnp.dot(seg, upper, precision=lax.Precision.HIGHEST, preferred_element_type=F32) + carry
        f_log2 = cs * LOG2_E
        o_ref[0, :, pl.ds(start, CUMSUM_CHUNK)] = f_log2
        if kcol_ref is None:
            return cs[:, CUMSUM_CHUNK - 1:CUMSUM_CHUNK]
        rest = -f_log2
        terms = []
        for _ in range(FORGET_SPLIT):
            part = pltpu.bitcast(pltpu.bitcast(rest, jnp.uint32) & jnp.uint32(0xFFFF0000), F32)
            terms.append(part)
            rest = rest - part
        terms_t = jnp.concatenate(terms + [pad_rows], axis=0).T.astype(BF16)
        for p in range(HEAD_PAIRS):
            select = jnp.where(jnp.logical_and(placed, head // 2 == p), 1.0, 0.0).astype(BF16)
            kcol_ref[0, p, pl.ds(start, CUMSUM_CHUNK), :] = _dot(terms_t, select).astype(BF16)
        return cs[:, CUMSUM_CHUNK - 1:CUMSUM_CHUNK]

    lax.fori_loop(0, t // CUMSUM_CHUNK, step, jnp.zeros((x_ref.shape[1], 1), F32))


def _forget_cumsum(x, key_columns):
    b, r, t = x.shape
    out_specs = [pl.BlockSpec((1, r, t), lambda i: (i, 0, 0))]
    out_shape = [jax.ShapeDtypeStruct((b, r, t), F32)]
    if key_columns:
        out_specs.append(pl.BlockSpec((1, HEAD_PAIRS, t, LANES), lambda i: (i, 0, 0, 0)))
        out_shape.append(jax.ShapeDtypeStruct((b, HEAD_PAIRS, t, LANES), BF16))
    return pl.pallas_call(
        _cumsum_kernel,
        grid=(b,),
        in_specs=[pl.BlockSpec((1, r, t), lambda i: (i, 0, 0))],
        out_specs=out_specs,
        out_shape=out_shape,
        compiler_params=_params(("arbitrary",)),
        name="forget_cumsum",
    )(x)


def _pair_layouts(cum_t, t_query_start, t_query):
    b, _, tp = cum_t.shape
    f_keys = cum_t.reshape(b, HEAD_PAIRS, 2, tp)
    f_query = jnp.swapaxes(f_keys[:, :, :, t_query_start:t_query_start + t_query], 2, 3)
    return f_keys, f_query


def _head_masks():
    lane = lax.broadcasted_iota(jnp.int32, (1, LANES), 1)
    return lane < HEAD_DIM


def _split_heads(q2, first_head):
    zero = jnp.zeros_like(q2)
    return jnp.where(first_head, q2, zero), jnp.where(first_head, zero, q2)


def _strict_lower_neg(n):
    r = lax.broadcasted_iota(jnp.int32, (n, n), 0)
    c = lax.broadcasted_iota(jnp.int32, (n, n), 1)
    return jnp.where(r > c, -1.0, 0.0).astype(BF16)


def _sb_block(qh, k2, v2, neg_tri, visible, acc_ref, r_ref, hd):
    z = _dot_nt(qh, k2)
    sp = _softplus(z)
    if visible is not None:
        sp = jnp.where(visible, sp, 0.0)
    later = _dot(sp.astype(BF16), neg_tri)
    p = jnp.exp(z - sp + later)
    if visible is not None:
        p = jnp.where(visible, p, 0.0)
    r = r_ref[hd]
    acc_ref[hd] += jnp.exp(r) * _dot(p.astype(BF16), v2)
    r_ref[hd] = r + later[:, 0:1] - sp[:, 0:1]


def _fox_block(qh, k2, v2_ones, fq, fk, allowed, acc_ref, m_ref, hd):
    s = _dot_nt(qh, k2) + fq - fk
    if allowed is not None:
        s = jnp.where(allowed, s, -jnp.inf)
    m_old = m_ref[hd]
    m_new = jnp.maximum(m_old, jnp.max(s, axis=-1, keepdims=True))
    p = jnp.exp2(s - m_new)
    acc_ref[hd] = jnp.exp2(m_old - m_new) * acc_ref[hd] + _dot(p.astype(BF16), v2_ones)
    m_ref[hd] = m_new


def _fox_finish(acc_ref, first_head):
    a, b = acc_ref[0], acc_ref[1]
    num = jnp.where(first_head, a, b)
    den = jnp.where(first_head, pltpu.roll(a, HEAD_DIM, 1), pltpu.roll(b, HEAD_DIM, 1))
    return num / den


def _with_ones(v2, first_head):
    one = jnp.ones_like(v2)
    return jnp.where(first_head, v2, one), jnp.where(first_head, one, v2)


def _sb_prompt_kernel(qt_ref, k_ref, vt_ref, o_ref, w_ref, acc_ref, r_ref, tri_ref, *, tq):
    i = pl.program_id(2)
    n_pairs = qt_ref.shape[0] // LANES
    sub = lax.broadcasted_iota(jnp.int32, (LANES, tq), 0)
    first_rows = sub < HEAD_DIM
    for pp in range(n_pairs):
        qt = qt_ref[pp * LANES:(pp + 1) * LANES, :]
        zero = jnp.zeros_like(qt)
        w_ref[2 * pp] = jnp.where(first_rows, qt, zero)
        w_ref[2 * pp + 1] = jnp.where(first_rows, zero, qt)
    acc_ref[...] = jnp.zeros_like(acc_ref)
    r_ref[...] = jnp.zeros_like(r_ref)

    @pl.when(jnp.logical_and(jnp.logical_and(pl.program_id(0) == 0, pl.program_id(1) == 0), i == 0))
    def _():
        s_idx = lax.broadcasted_iota(jnp.int32, tri_ref.shape, 0)
        j_idx = lax.broadcasted_iota(jnp.int32, tri_ref.shape, 1)
        tri_ref[...] = jnp.where(j_idx > s_idx, -1.0, 0.0).astype(BF16)

    def block(start, size, vis):
        neg_tri = tri_ref[:size, :size]
        for hd in range(2 * n_pairs):
            lanes = slice((hd // 2) * LANES, (hd // 2 + 1) * LANES)
            keys = k_ref[0, pl.ds(start, size), lanes]
            vt = vt_ref[lanes, pl.ds(start, size)]
            z = _dot(keys, w_ref[hd])
            sp = _softplus(z)
            if vis is not None:
                sp = jnp.where(vis, sp, 0.0)
            later = _dot(neg_tri, sp.astype(BF16))
            p = jnp.exp(z - sp + later)
            if vis is not None:
                p = jnp.where(vis, p, 0.0)
            r = r_ref[hd]
            acc_ref[hd] += jnp.exp(r) * _dot(vt, p.astype(BF16))
            r_ref[hd] = r + later[0:1, :] - sp[0:1, :]

    def live():
        worst = r_ref[0]
        for hd in range(1, 2 * n_pairs):
            worst = jnp.maximum(worst, r_ref[hd])
        return jnp.max(worst)

    first = jnp.maximum(i - 1, 0)
    ahead = lax.broadcasted_iota(jnp.int32, (2 * tq, tq), 0) - lax.broadcasted_iota(jnp.int32, (2 * tq, tq), 1)
    block(pl.multiple_of(first * tq, tq), 2 * tq, ahead < (i - first) * tq)

    def cond(carry):
        j, r_max = carry
        return jnp.logical_and(j >= 0, r_max > SB_DEAD_LOG)

    def body(carry):
        j, _ = carry
        block(pl.multiple_of(j * tq, tq), tq, None)
        return j - 1, live()

    lax.while_loop(cond, body, (i - 2, live()))
    for pp in range(n_pairs):
        out_t = jnp.where(first_rows, acc_ref[2 * pp], acc_ref[2 * pp + 1])
        o_ref[0, :, pp * LANES:(pp + 1) * LANES] = out_t.T.astype(o_ref.dtype)


SB_PAIRS_PER_STEP = 2


def _sb_prompt(qkv, qv_t, tq):
    b, t, _ = qkv.shape
    nq = t // tq
    group = SB_PAIRS_PER_STEP * LANES
    n_groups = WIDTH // group
    heads = 2 * SB_PAIRS_PER_STEP
    kernel = functools.partial(_sb_prompt_kernel, tq=tq)
    return pl.pallas_call(
        kernel,
        grid=(b, n_groups, nq),
        in_specs=[pl.BlockSpec((group, tq), lambda b_, p, i: (p, b_ * nq + i)),
                  pl.BlockSpec((1, t, group), lambda b_, p, i: (b_, 0, n_groups + p)),
                  pl.BlockSpec((group, t), lambda b_, p, i: (n_groups + p, b_))],
        out_specs=pl.BlockSpec((1, tq, group), lambda b_, p, i: (b_, i, p)),
        out_shape=jax.ShapeDtypeStruct((b, t, WIDTH), BF16),
        scratch_shapes=[pltpu.VMEM((heads, LANES, tq), BF16), pltpu.VMEM((heads, LANES, tq), F32),
                        pltpu.VMEM((heads, 1, tq), F32), pltpu.VMEM((2 * tq, 2 * tq), BF16)],
        compiler_params=_params(("arbitrary", "arbitrary", "arbitrary")),
        name="sb_attention_prompt",
    )(qv_t, qkv, qv_t)


FOX_DEAD_LOG2 = -150.0
FOX_BOUNDED_MAX = 40.0


def _fox_prompt_kernel(kmax_ref, fend_ref, qt_ref, k_ref, kf_ref, vt_ref, fq_ref, o_ref,
                       w_ref, acc_ref, m_ref, zb_ref, *, tq):
    bi = pl.program_id(0)
    pi = pl.program_id(1)
    i = pl.program_id(2)
    sub = lax.broadcasted_iota(jnp.int32, (LANES, tq), 0)
    first_rows = sub < HEAD_DIM
    qt = qt_ref[...]
    zero = jnp.zeros_like(qt)
    for hd in range(2):
        lo = hd * FORGET_SPLIT
        select = jnp.where(jnp.logical_and(sub >= lo, sub < lo + FORGET_SPLIT), 1.0, 0.0).astype(BF16)
        q_head = jnp.where(first_rows, qt, zero) if hd == 0 else jnp.where(first_rows, zero, qt)
        w_ref[hd, :LANES, :] = q_head
        w_ref[hd, LANES:, :] = select
        q_f32 = q_head.astype(F32)
        q_norm = jnp.sqrt(jnp.sum(q_f32 * q_f32, axis=0, keepdims=True))
        zb_ref[hd] = q_norm * (kmax_ref[bi, 2 * pi + hd] * 1.01)
    acc_ref[...] = jnp.zeros_like(acc_ref)
    m_ref[...] = jnp.full_like(m_ref, -jnp.inf)
    key_idx = lax.broadcasted_iota(jnp.int32, (tq, tq), 0)
    query_idx = lax.broadcasted_iota(jnp.int32, (tq, tq), 1)
    allowed = key_idx <= query_idx
    den_row = (HEAD_DIM, 0)

    def operands(j):
        start = pl.multiple_of(j * tq, tq)
        keys = jnp.concatenate([k_ref[0, pl.ds(start, tq), :], kf_ref[0, 0, pl.ds(start, tq), :]], axis=1)
        vt = vt_ref[:, pl.ds(start, tq)]
        one = jnp.ones_like(vt)
        first_v = lax.broadcasted_iota(jnp.int32, vt.shape, 0) < HEAD_DIM
        return keys, (jnp.where(first_v, vt, one), jnp.where(first_v, one, vt))

    def block_bounded(j, mask):
        keys, vts = operands(j)
        for hd in range(2):
            u = _dot(keys, w_ref[hd])
            if mask is not None:
                u = jnp.where(mask, u, -jnp.inf)
            p = jnp.exp2(u + (fq_ref[0, 0, hd:hd + 1, :] - zb_ref[hd]))
            acc_ref[hd] += _dot(vts[hd], p.astype(BF16))

    def alive_bounded(j):
        jc = jnp.maximum(j, 0)
        reach = []
        for hd in range(2):
            den = acc_ref[hd, den_row[hd]:den_row[hd] + 1, :]
            gap = jnp.max(fq_ref[0, 0, hd:hd + 1, :] - jnp.log2(den))
            reach.append(gap - fend_ref[bi, pi, hd, jc])
        return jnp.maximum(reach[0], reach[1]) + tq.bit_length() > FOX_DEAD_LOG2

    def block_exact(j, mask):
        keys, vts = operands(j)
        for hd in range(2):
            u = _dot(keys, w_ref[hd])
            if mask is not None:
                u = jnp.where(mask, u, -jnp.inf)
            fq = fq_ref[0, 0, hd:hd + 1, :]
            m_old = m_ref[hd]
            m_new = jnp.maximum(m_old, jnp.max(u, axis=0, keepdims=True) + fq)
            p = jnp.exp2(u + (fq - m_new))
            acc_ref[hd] = jnp.exp2(m_old - m_new) * acc_ref[hd] + _dot(vts[hd], p.astype(BF16))
            m_ref[hd] = m_new

    def alive_exact(j):
        jc = jnp.maximum(j, 0)
        reach = [jnp.max(zb_ref[hd] + fq_ref[0, 0, hd:hd + 1, :] - m_ref[hd]) - fend_ref[bi, pi, hd, jc]
                 for hd in range(2)]
        return jnp.maximum(reach[0], reach[1]) > FOX_DEAD_LOG2

    def sweep(block, alive):
        block(i, allowed)

        def cond(carry):
            j, go = carry
            return jnp.logical_and(j >= 0, go)

        def body(carry):
            j, _ = carry
            block(j, None)
            return j - 1, alive(j - 1)

        lax.while_loop(cond, body, (i - 1, alive(i - 1)))

    bounded = jnp.max(jnp.maximum(zb_ref[0], zb_ref[1])) <= FOX_BOUNDED_MAX

    @pl.when(bounded)
    def _():
        sweep(block_bounded, alive_bounded)

    @pl.when(jnp.logical_not(bounded))
    def _():
        sweep(block_exact, alive_exact)

    a, b = acc_ref[0], acc_ref[1]
    out_t = jnp.where(first_rows, a / a[HEAD_DIM:HEAD_DIM + 1, :], b / b[0:1, :])
    o_ref[0] = out_t.T.astype(o_ref.dtype)


def _fox_prompt(qkv, qv_t, cum_t, key_cols, k_max, tq):
    b, t, _ = qkv.shape
    nq = t // tq
    f_rows = cum_t.reshape(b, HEAD_PAIRS, 2, t)
    f_block_end = f_rows[:, :, :, tq - 1::tq]
    smem = pl.BlockSpec(memory_space=pltpu.SMEM)
    kernel = functools.partial(_fox_prompt_kernel, tq=tq)
    return pl.pallas_call(
        kernel,
        grid=(b, HEAD_PAIRS, nq),
        in_specs=[smem, smem,
                  pl.BlockSpec((LANES, tq), lambda b_, p, i: (2 * HEAD_PAIRS + p, b_ * nq + i)),
                  pl.BlockSpec((1, t, LANES), lambda b_, p, i: (b_, 0, 4 * HEAD_PAIRS + p)),
                  pl.BlockSpec((1, 1, t, LANES), lambda b_, p, i: (b_, p, 0, 0)),
                  pl.BlockSpec((LANES, t), lambda b_, p, i: (3 * HEAD_PAIRS + p, b_)),
                  pl.BlockSpec((1, 1, 2, tq), lambda b_, p, i: (b_, p, 0, i))],
        out_specs=pl.BlockSpec((1, tq, LANES), lambda b_, p, i: (b_, i, p)),
        out_shape=jax.ShapeDtypeStruct((b, t, WIDTH), BF16),
        scratch_shapes=[pltpu.VMEM((2, 2 * LANES, tq), BF16), pltpu.VMEM((2, LANES, tq), F32),
                        pltpu.VMEM((2, 1, tq), F32), pltpu.VMEM((2, 1, tq), F32)],
        compiler_params=_params(("arbitrary", "arbitrary", "arbitrary")),
        name="fox_attention_prompt",
    )(k_max, f_block_end, qv_t, qkv, key_cols, qv_t, f_rows)


def _sample_attn_kernel(qsb_ref, ksb_ref, vsb_ref, qfx_ref, kfx_ref, vfx_ref,
                        csk_ref, csv_ref, cfk_ref, cfv_ref, fq_ref, fk_ref,
                        osb_ref, ofx_ref, acc_ref, st_ref, *, tk):
    t_new = qsb_ref.shape[1]
    n_past = csk_ref.shape[1]
    n_blocks = n_past // tk
    first_head = _head_masks()
    row = lax.broadcasted_iota(jnp.int32, (t_new, t_new), 0)
    col = lax.broadcasted_iota(jnp.int32, (t_new, t_new), 1)

    qs = _split_heads(qsb_ref[0], first_head)
    acc_ref[...] = jnp.zeros_like(acc_ref)
    st_ref[...] = jnp.zeros_like(st_ref)
    tri_new = _strict_lower_neg(t_new)
    tri_past = _strict_lower_neg(tk)
    for hd in range(2):
        _sb_block(qs[hd], ksb_ref[0], vsb_ref[0], tri_new, col < row, acc_ref, st_ref, hd)

    def live():
        return jnp.max(jnp.maximum(st_ref[0], st_ref[1]))

    def sb_cond(carry):
        j, r_max = carry
        return jnp.logical_and(j >= 0, r_max > SB_DEAD_LOG)

    def sb_body(carry):
        j, _ = carry
        start = pl.multiple_of(j * tk, tk)
        k2 = csk_ref[0, pl.ds(start, tk), :].astype(BF16)
        v2 = csv_ref[0, pl.ds(start, tk), :].astype(BF16)
        for hd in range(2):
            _sb_block(qs[hd], k2, v2, tri_past, None, acc_ref, st_ref, hd)
        return j - 1, live()

    lax.while_loop(sb_cond, sb_body, (n_blocks - 1, live()))
    osb_ref[0] = jnp.where(first_head, acc_ref[0], acc_ref[1]).astype(osb_ref.dtype)

    qs = _split_heads(qfx_ref[0], first_head)
    acc_ref[...] = jnp.zeros_like(acc_ref)
    st_ref[...] = jnp.full_like(st_ref, -jnp.inf)
    fq = fq_ref[0, 0]
    vs = _with_ones(vfx_ref[0], first_head)
    for hd in range(2):
        fk = fk_ref[0, 0, hd:hd + 1, n_past:n_past + t_new]
        _fox_block(qs[hd], kfx_ref[0], vs[hd], fq[:, hd:hd + 1], fk, col <= row, acc_ref, st_ref, hd)

    k2 = cfk_ref[0].astype(BF16)
    vs_ = _with_ones(cfv_ref[0].astype(BF16), first_head)
    for hd in range(2):
        fk = fk_ref[0, 0, hd:hd + 1, 0:n_past]
        _fox_block(qs[hd], k2, vs_[hd], fq[:, hd:hd + 1], fk, None, acc_ref, st_ref, hd)
    ofx_ref[0] = _fox_finish(acc_ref, first_head).astype(ofx_ref.dtype)


def _sample_attention(qkv, cache_sb_k, cache_sb_v, cache_fox_k, cache_fox_v, f_query, f_keys, tk):
    b, t_new, _ = qkv.shape
    n_past = cache_sb_k.shape[1]
    tp = f_keys.shape[3]
    new = lambda c: pl.BlockSpec((1, t_new, LANES), lambda b_, p: (b_, 0, c * HEAD_PAIRS + p))
    past = pl.BlockSpec((1, n_past, LANES), lambda b_, p: (b_, 0, p))
    out = pl.BlockSpec((1, t_new, LANES), lambda b_, p: (b_, 0, p))
    kernel = functools.partial(_sample_attn_kernel, tk=tk)
    return pl.pallas_call(
        kernel,
        grid=(b, HEAD_PAIRS),
        in_specs=[new(0), new(1), new(2), new(3), new(4), new(5), past, past, past, past,
                  pl.BlockSpec((1, 1, t_new, 2), lambda b_, p: (b_, p, 0, 0)),
                  pl.BlockSpec((1, 1, 2, tp), lambda b_, p: (b_, p, 0, 0))],
        out_specs=[out, out],
        out_shape=[jax.ShapeDtypeStruct((b, t_new, WIDTH), BF16)] * 2,
        scratch_shapes=[pltpu.VMEM((2, t_new, LANES), F32), pltpu.VMEM((2, t_new, 1), F32)],
        compiler_params=_params(("arbitrary", "arbitrary")),
        name="attention_sample",
    )(qkv, qkv, qkv, qkv, qkv, qkv, cache_sb_k, cache_sb_v, cache_fox_k, cache_fox_v, f_query, f_keys)


def _mix_out_kernel(x_ref, osb_ref, ofx_ref, sg_ref, gate_ref, g_ref, wsb_ref, wfx_ref, wo_ref, o_ref):
    d = x_ref.shape[1]
    merged = (sg_ref[:, :d].astype(F32) * _dot(osb_ref[...], wsb_ref[...])
              + sg_ref[:, d:].astype(F32) * _dot(ofx_ref[...], wfx_ref[...]))
    y = _dot(merged.astype(BF16), wo_ref[...])
    o_ref[...] = x_ref[...] + gate_ref[0] * _rms(y, g_ref[...])


def _mix_out(x, o_sb, o_fx, sg, gate, g_post, w_sb_out, w_fox_out, w_out, tm):
    n, d = x.shape
    tiles_per_group = (n // tm) // gate.shape[0]
    const = lambda i: (0, 0)
    row = lambda i: (i, 0)
    return pl.pallas_call(
        _mix_out_kernel,
        grid=(n // tm,),
        in_specs=[pl.BlockSpec((tm, d), row), pl.BlockSpec((tm, WIDTH), row), pl.BlockSpec((tm, WIDTH), row),
                  pl.BlockSpec((tm, 2 * d), row), _mod_spec(gate, tm, tiles_per_group),
                  pl.BlockSpec((1, d), const), pl.BlockSpec(w_sb_out.shape, const),
                  pl.BlockSpec(w_fox_out.shape, const), pl.BlockSpec(w_out.shape, const)],
        out_specs=pl.BlockSpec((tm, d), row),
        out_shape=jax.ShapeDtypeStruct((n, d), F32),
        compiler_params=_params(("arbitrary",)),
        name="mix_out",
    )(x, o_sb, o_fx, sg, gate, g_post, w_sb_out, w_fox_out, w_out)


def _swiglu_chunk(hb, wa, wb, wd):
    a = _dot(hb, wa)
    b = _dot(hb, wb)
    return a * _sigmoid(a) * b, wd


def _ffn_kernel(x_ref, gpre_ref, shift_ref, scale_ref, gate_ref, gpost_ref, wa_ref, wb_ref, wd_ref,
                o_ref, h_ref, acc_ref):
    f = pl.program_id(1)

    @pl.when(f == 0)
    def _():
        h = _rms(x_ref[...], gpre_ref[...]) * (1.0 + scale_ref[0]) + shift_ref[0]
        h_ref[...] = h.astype(BF16)
        acc_ref[...] = jnp.zeros_like(acc_ref)

    hb = h_ref[...]
    a = _dot(hb, wa_ref[...])
    b = _dot(hb, wb_ref[...])
    act = a * _sigmoid(a) * b
    acc_ref[...] += _dot(act.astype(BF16), wd_ref[...])

    @pl.when(f == pl.num_programs(1) - 1)
    def _():
        o_ref[...] = x_ref[...] + gate_ref[0] * _rms(acc_ref[...], gpost_ref[...])


def _ffn(x, g_pre, shift, scale, gate, g_post, w_gate_up, w_down, tm, tf):
    n, d = x.shape
    d_ff = w_down.shape[0]
    nf = d_ff // tf
    tiles_per_group = (n // tm) // shift.shape[0]
    const = lambda i, f: (0, 0)
    row = lambda i, f: (i, 0)
    return pl.pallas_call(
        _ffn_kernel,
        grid=(n // tm, nf),
        in_specs=[pl.BlockSpec((tm, d), row), pl.BlockSpec((1, d), const),
                  _mod_spec(shift, tm, tiles_per_group), _mod_spec(scale, tm, tiles_per_group),
                  _mod_spec(gate, tm, tiles_per_group), pl.BlockSpec((1, d), const),
                  pl.BlockSpec((d, tf), lambda i, f: (0, f)),
                  pl.BlockSpec((d, tf), lambda i, f: (0, nf + f)),
                  pl.BlockSpec((tf, d), lambda i, f: (f, 0))],
        out_specs=pl.BlockSpec((tm, d), row),
        out_shape=jax.ShapeDtypeStruct((n, d), F32),
        scratch_shapes=[pltpu.VMEM((tm, d), BF16), pltpu.VMEM((tm, d), F32)],
        compiler_params=_params(("arbitrary", "arbitrary")),
        name="ffn_dense",
    )(x, g_pre, shift, scale, gate, g_post, w_gate_up, w_gate_up, w_down)


def _route_kernel(x_ref, gpre_ref, shift_ref, scale_ref, wr_ref, h_ref, idx_ref, w_ref, rank_ref, count_ref,
                  *, n_experts):
    tm = x_ref.shape[0]

    @pl.when(pl.program_id(0) == 0)
    def _():
        count_ref[...] = jnp.zeros_like(count_ref)

    h = _rms(x_ref[...], gpre_ref[...]) * (1.0 + scale_ref[0]) + shift_ref[0]
    h_ref[...] = h
    logits = jnp.dot(h, wr_ref[...], precision=lax.Precision.HIGHEST, preferred_element_type=F32)[:, :n_experts]
    idx = lax.broadcasted_iota(jnp.int32, logits.shape, 1)
    m1 = jnp.max(logits, axis=-1, keepdims=True)
    i1 = jnp.min(jnp.where(logits == m1, idx, n_experts), axis=-1, keepdims=True)
    rest = jnp.where(idx == i1, -jnp.inf, logits)
    m2 = jnp.max(rest, axis=-1, keepdims=True)
    i2 = jnp.min(jnp.where(rest == m2, idx, n_experts), axis=-1, keepdims=True)
    e2 = jnp.exp(m2 - m1)
    slot = lax.broadcasted_iota(jnp.int32, idx_ref.shape, 1)
    idx_ref[...] = jnp.where(slot == 0, i1, i2)
    w_ref[...] = jnp.where(slot == 0, 1.0 / (1.0 + e2), e2 / (1.0 + e2))
    lane = lax.broadcasted_iota(jnp.int32, (tm, LANES), 1)
    first, second = lane == i1, lane == i2
    chosen = jnp.where(jnp.logical_or(first, second), 1.0, 0.0)
    t_row = lax.broadcasted_iota(jnp.int32, (tm, tm), 0)
    t_col = lax.broadcasted_iota(jnp.int32, (tm, tm), 1)
    earlier = jnp.where(t_col < t_row, 1.0, 0.0).astype(BF16)
    before = count_ref[0:1, :] + _dot(earlier, chosen.astype(BF16))
    rank1 = jnp.sum(jnp.where(first, before, 0.0), axis=-1, keepdims=True)
    rank2 = jnp.sum(jnp.where(second, before, 0.0), axis=-1, keepdims=True)
    rank_ref[...] = jnp.where(slot == 0, rank1, rank2).astype(jnp.int32)
    count_ref[...] = count_ref[...] + jnp.sum(chosen, axis=0, keepdims=True)


def _route(x, g_pre, shift, scale, w_router, n_experts, tm):
    n, d = x.shape
    tiles_per_group = (n // tm) // shift.shape[0]
    const = lambda i: (0, 0)
    row = lambda i: (i, 0)
    return pl.pallas_call(
        functools.partial(_route_kernel, n_experts=n_experts),
        grid=(n // tm,),
        in_specs=[pl.BlockSpec((tm, d), row), pl.BlockSpec((1, d), const),
                  _mod_spec(shift, tm, tiles_per_group), _mod_spec(scale, tm, tiles_per_group),
                  pl.BlockSpec(w_router.shape, const)],
        out_specs=[pl.BlockSpec((tm, d), row), pl.BlockSpec((tm, TOP_K), row), pl.BlockSpec((tm, TOP_K), row),
                   pl.BlockSpec((tm, TOP_K), row), pl.BlockSpec((8, LANES), const)],
        out_shape=[jax.ShapeDtypeStruct((n, d), F32), jax.ShapeDtypeStruct((n, TOP_K), jnp.int32),
                   jax.ShapeDtypeStruct((n, TOP_K), F32), jax.ShapeDtypeStruct((n, TOP_K), jnp.int32),
                   jax.ShapeDtypeStruct((8, LANES), F32)],
        compiler_params=_params(("arbitrary",)),
        name="moe_route",
    )(x, g_pre, shift, scale, w_router)


def _dispatch_tables(top_idx, rank, counts, n_experts, tm):
    n = top_idx.shape[0]
    n_assign = n * TOP_K
    expert = top_idx.reshape(n_assign)
    padded = (counts + tm - 1) // tm * tm
    group_end = jnp.cumsum(padded)
    group_start = group_end - padded
    onehot = expert[:, None] == jnp.arange(n_experts, dtype=jnp.int32)[None, :]
    pos = jnp.sum(jnp.where(onehot, group_start[None, :], 0), axis=1) + rank.reshape(n_assign)
    p_rows = (-(-n_assign // tm) + n_experts) * tm
    src_token = jnp.zeros((p_rows,), jnp.int32).at[pos].set(jnp.arange(n_assign, dtype=jnp.int32) // TOP_K)
    tile_start = jnp.arange(p_rows // tm, dtype=jnp.int32) * tm
    tile_expert = jnp.minimum(jnp.sum(tile_start[:, None] >= group_end[None, :], axis=1), n_experts - 1)
    n_used = (group_end[-1] // tm).reshape(1)
    return src_token, tile_expert.astype(jnp.int32), n_used.astype(jnp.int32), pos.astype(jnp.int32)


def _gather_rows(src_hbm, dst, sem, index_of_row, n_rows):
    def issue(r, carry):
        pltpu.make_async_copy(src_hbm.at[pl.ds(index_of_row(r), 1), :], dst.at[pl.ds(r, 1), :], sem).start()
        return carry

    lax.fori_loop(0, n_rows, issue, 0, unroll=8)


def _wait_rows(src_hbm, dst, sem):
    pltpu.make_async_copy(src_hbm.at[pl.ds(0, dst.shape[0]), :], dst, sem).wait()


def _experts_kernel(src_ref, expert_ref, used_ref, h_hbm, wa_ref, wb_ref, wd_ref, o_ref,
                    rows_ref, xb_ref, acc_ref, sem, *, tm):
    i = pl.program_id(0)
    f = pl.program_id(1)
    nf = pl.num_programs(1)
    n_used = used_ref[0]
    slot = lax.rem(i, 2)
    active = i < n_used
    rows_per_step = tm // nf

    @pl.when(jnp.logical_and(f == 0, i == 0))
    def _():
        _gather_rows(h_hbm, rows_ref.at[0], sem.at[0], lambda r: src_ref[r], tm)

    @pl.when(jnp.logical_and(f == 0, i <= n_used))
    def _():
        _wait_rows(h_hbm, rows_ref.at[slot], sem.at[slot])

    @pl.when(jnp.logical_and(f == 0, active))
    def _():
        xb_ref[...] = rows_ref[slot].astype(BF16)
        acc_ref[...] = jnp.zeros_like(acc_ref)

    @pl.when(active)
    def _():
        base = (i + 1) * tm + f * rows_per_step
        nxt = rows_ref.at[1 - slot]
        for r in range(rows_per_step):
            row = f * rows_per_step + r
            pltpu.make_async_copy(h_hbm.at[pl.ds(src_ref[base + r], 1), :], nxt.at[pl.ds(row, 1), :],
                                  sem.at[1 - slot]).start()
        xb = xb_ref[...]
        a = _dot(xb, wa_ref[0])
        b = _dot(xb, wb_ref[0])
        act = a * _sigmoid(a) * b
        acc_ref[...] += _dot(act.astype(BF16), wd_ref[0])

    last = f == pl.num_programs(1) - 1

    @pl.when(jnp.logical_and(last, active))
    def _():
        o_ref[...] = acc_ref[...]

    @pl.when(jnp.logical_and(last, jnp.logical_not(active)))
    def _():
        o_ref[...] = jnp.zeros_like(o_ref)


def _experts(h, src_token, tile_expert, n_used, w_gate_up, w_down, tm, tf):
    p_rows = src_token.shape[0]
    d = h.shape[1]
    nf = w_down.shape[1] // tf
    ff = lambda i, f, used: jnp.where(i < used[0], f, nf - 1)
    grid_spec = pltpu.PrefetchScalarGridSpec(
        num_scalar_prefetch=3,
        grid=(p_rows // tm, nf),
        in_specs=[pl.BlockSpec(memory_space=pl.ANY),
                  pl.BlockSpec((1, d, tf), lambda i, f, src, ex, used: (ex[i], 0, ff(i, f, used))),
                  pl.BlockSpec((1, d, tf), lambda i, f, src, ex, used: (ex[i], 0, nf + ff(i, f, used))),
                  pl.BlockSpec((1, tf, d), lambda i, f, src, ex, used: (ex[i], ff(i, f, used), 0))],
        out_specs=pl.BlockSpec((tm, d), lambda i, f, src, ex, used: (i, 0)),
        scratch_shapes=[pltpu.VMEM((2, tm, d), F32), pltpu.VMEM((tm, d), BF16), pltpu.VMEM((tm, d), F32),
                        pltpu.SemaphoreType.DMA((2,))])
    return pl.pallas_call(
        functools.partial(_experts_kernel, tm=tm),
        grid_spec=grid_spec,
        out_shape=jax.ShapeDtypeStruct((p_rows, d), F32),
        compiler_params=_params(("arbitrary", "arbitrary")),
        name="moe_experts",
    )(src_token, tile_expert, n_used, h, w_gate_up, w_gate_up, w_down)


def _combine_kernel(pos_ref, y_hbm, x_ref, w_ref, gate_ref, gpost_ref, o_ref, rows_ref, sem, *, tm):
    i = pl.program_id(0)
    slot = lax.rem(i, 2)

    def gather(tile, into):
        for k in range(TOP_K):
            _gather_rows(y_hbm, rows_ref.at[into, k], sem.at[into],
                         lambda r, k=k: pos_ref[(tile * tm + r) * TOP_K + k], tm)

    @pl.when(i == 0)
    def _():
        gather(0, 0)

    for k in range(TOP_K):
        _wait_rows(y_hbm, rows_ref.at[slot, k], sem.at[slot])

    @pl.when(i + 1 < pl.num_programs(0))
    def _():
        gather(i + 1, 1 - slot)

    w = w_ref[...]
    mixed = w[:, 0:1] * rows_ref[slot, 0] + w[:, 1:2] * rows_ref[slot, 1]
    o_ref[...] = x_ref[...] + gate_ref[0] * _rms(mixed, gpost_ref[...])


def _combine(y, pos, top_w, x, gate, g_post, tm):
    n, d = x.shape
    tiles_per_group = (n // tm) // gate.shape[0]
    grid_spec = pltpu.PrefetchScalarGridSpec(
        num_scalar_prefetch=1,
        grid=(n // tm,),
        in_specs=[pl.BlockSpec(memory_space=pl.ANY),
                  pl.BlockSpec((tm, d), lambda i, pos_: (i, 0)),
                  pl.BlockSpec((tm, TOP_K), lambda i, pos_: (i, 0)),
                  _mod_spec(gate, tm, tiles_per_group),
                  pl.BlockSpec((1, d), lambda i, pos_: (0, 0))],
        out_specs=pl.BlockSpec((tm, d), lambda i, pos_: (i, 0)),
        scratch_shapes=[pltpu.VMEM((2, TOP_K, tm, d), F32), pltpu.SemaphoreType.DMA((2,))])
    return pl.pallas_call(
        functools.partial(_combine_kernel, tm=tm),
        grid_spec=grid_spec,
        out_shape=jax.ShapeDtypeStruct((n, d), F32),
        compiler_params=_params(("arbitrary",)),
        name="moe_combine",
    )(pos, y, x, top_w, gate, g_post)


def _moe(x, g_pre, shift, scale, gate, g_post, w_router, w_gate_up, w_down, n_experts, tm_route, tm, tf, tm_out):
    h, top_idx, top_w, rank, counts = _route(x, g_pre, shift, scale, w_router, n_experts, tm_route)
    counts = counts[0, :n_experts].astype(jnp.int32)
    src_token, tile_expert, n_used, pos = _dispatch_tables(top_idx, rank, counts, n_experts, tm)
    y = _experts(h, src_token, tile_expert, n_used, w_gate_up, w_down, tm, tf)
    return _combine(y, pos, top_w, x, gate, g_post, tm_out)


def _row_tile(n, want):
    return want if n % want == 0 else n


def kernel(x_prompt, x_sample, c_prompt, c_sample, cache_sb_k, cache_sb_v, cache_fox_k, cache_fox_v, cache_fox_logf, w_mod, b_mod, g_pre_mix, g_post_mix, g_pre_ffn, g_post_ffn, w_in, b_forget, w_sb_out, w_fox_out, w_out, w_ffn_gate_up, w_ffn_down, w_router, w_moe_gate_up, w_moe_down):
    bsz, seq, d = x_prompt.shape
    dec_b, dec_t, _ = x_sample.shape
    depth = w_mod.shape[0]
    n_past = cache_sb_k.shape[2]
    n_experts = w_router.shape[2]
    n_p, n_s = bsz * seq, dec_b * dec_t
    tq = _row_tile(seq, 256)
    past_chunk = _row_tile(n_past, 256)

    c_all = jnp.concatenate([c_prompt, c_sample], axis=0)
    c_rows = -(-c_all.shape[0] // 8) * 8
    c_all = jnp.pad(c_all, ((0, c_rows - c_all.shape[0]), (0, 0)))
    mod = _modulation(c_all, w_mod, b_mod)

    xp = x_prompt.reshape(n_p, d)
    xs = x_sample.reshape(n_s, d)
    stacked_p = stacked_s = None
    for l in range(depth):
        mod_p = mod[l, :bsz].reshape(bsz, 1, 6, d)
        mod_s = jnp.repeat(mod[l, bsz:bsz + dec_b].reshape(dec_b, 6, d), dec_t, axis=0)[None]
        mp = [mod_p[:, :, i] for i in range(6)]
        ms = [mod_s[:, :, i] for i in range(6)]
        vec = lambda a: a[l].reshape(1, -1)

        wqkv = w_in[l, :, :6 * WIDTH].astype(BF16)
        wf = jnp.pad(w_in[l, :, 6 * WIDTH:6 * WIDTH + N_HEADS], ((0, 0), (0, LANES - N_HEADS))).astype(BF16)
        wg = w_in[l, :, 6 * WIDTH + N_HEADS:].astype(BF16)
        bf = b_forget[l].reshape(1, N_HEADS)
        wsb, wfx, wo = w_sb_out[l].astype(BF16), w_fox_out[l].astype(BF16), w_out[l].astype(BF16)

        qkv_p, qvt_p, *stacked_p, sg_p, ksq_p = _in_projection(
            xp, vec(g_pre_mix), mp[0], mp[1], wqkv, wf, wg, bf, _row_tile(seq, 256), l, depth, stacked_p)
        lf_p = stacked_p[4][l]
        cum_p, key_cols_p = _forget_cumsum(jnp.swapaxes(lf_p.reshape(bsz, seq, N_HEADS), 1, 2), True)
        qkv_p3 = qkv_p.reshape(bsz, seq, 6 * WIDTH)
        osb_p = _sb_prompt(qkv_p3, qvt_p, tq)
        ofx_p = _fox_prompt(qkv_p3, qvt_p, cum_p, key_cols_p, jnp.sqrt(ksq_p.reshape(bsz, N_HEADS)),
                            _row_tile(seq, 512))
        xp = _mix_out(xp, osb_p.reshape(n_p, WIDTH), ofx_p.reshape(n_p, WIDTH), sg_p, mp[2], vec(g_post_mix),
                      wsb, wfx, wo, _row_tile(seq, 512))

        qkv_s, _, *stacked_s, sg_s, _ = _in_projection(
            xs, vec(g_pre_mix), ms[0], ms[1], wqkv, wf, wg, bf, n_s, l, depth, stacked_s)
        lf_s = stacked_s[4][l]
        t_all = n_past + dec_t
        t_pad = -(-t_all // CUMSUM_CHUNK) * CUMSUM_CHUNK
        lf_all = jnp.concatenate([cache_fox_logf[l].astype(F32), lf_s.reshape(dec_b, dec_t, N_HEADS),
                                  jnp.zeros((dec_b, t_pad - t_all, N_HEADS), F32)], axis=1)
        (cum_s,) = _forget_cumsum(jnp.swapaxes(lf_all, 1, 2), False)
        fk_s, fq_s = _pair_layouts(cum_s, n_past, dec_t)
        cache = lambda a: a[l].reshape(dec_b, n_past, WIDTH)
        osb_s, ofx_s = _sample_attention(qkv_s.reshape(dec_b, dec_t, 6 * WIDTH), cache(cache_sb_k), cache(cache_sb_v),
                                         cache(cache_fox_k), cache(cache_fox_v), fq_s, fk_s, past_chunk)
        xs = _mix_out(xs, osb_s.reshape(n_s, WIDTH), ofx_s.reshape(n_s, WIDTH), sg_s, ms[2], vec(g_post_mix),
                      wsb, wfx, wo, n_s)

        if l % 2 == 0:
            wgu, wd = w_ffn_gate_up[l // 2].astype(BF16), w_ffn_down[l // 2].astype(BF16)
            tf = _row_tile(wd.shape[0], 1408)
            xp = _ffn(xp, vec(g_pre_ffn), mp[3], mp[4], mp[5], vec(g_post_ffn), wgu, wd, _row_tile(seq, 512), tf)
            xs = _ffn(xs, vec(g_pre_ffn), ms[3], ms[4], ms[5], vec(g_post_ffn), wgu, wd, n_s, tf)
        else:
            wr = jnp.pad(w_router[l // 2], ((0, 0), (0, LANES - n_experts)))
            wgu, wd = w_moe_gate_up[l // 2].astype(BF16), w_moe_down[l // 2].astype(BF16)
            tf = _row_tile(wd.shape[1], 896)
            xp = _moe(xp, vec(g_pre_ffn), mp[3], mp[4], mp[5], vec(g_post_ffn), wr, wgu, wd, n_experts,
                      _row_tile(seq, 512), 1024, tf, _row_tile(seq, 256))
            xs = _moe(xs, vec(g_pre_ffn), ms[3], ms[4], ms[5], vec(g_post_ffn), wr, wgu, wd, n_experts,
                      n_s, 128, tf, n_s)

    split_p = [a.reshape(depth, bsz, seq, *a.shape[2:]) for a in stacked_p]
    split_s = [a.reshape(depth, dec_b, dec_t, *a.shape[2:]) for a in stacked_s]
    return (xp.reshape(bsz, seq, d), xs.reshape(dec_b, dec_t, d), *split_p, *split_s)
```

```python
import functools

import jax
import jax.numpy as jnp
from jax import lax
from jax.experimental import pallas as pl
from jax.experimental.pallas import tpu as pltpu

F32 = jnp.float32
BF16 = jnp.bfloat16

HEAD_DIM = 64
N_HEADS = 8
LANES = 128
HEAD_PAIRS = N_HEADS * HEAD_DIM // LANES
WIDTH = N_HEADS * HEAD_DIM
TOP_K = 2
RMS_EPS = 1e-6
LOG2_E = 1.4426950408889634
SB_DEAD_LOG = -104.0
VMEM_LIMIT = 56 * 1024 * 1024


def _params(sem, vmem=VMEM_LIMIT):
    return pltpu.CompilerParams(dimension_semantics=sem, vmem_limit_bytes=vmem)


def _dot(a, b):
    return jnp.dot(a, b, preferred_element_type=F32)


def _dot_nt(a, b):
    return lax.dot_general(a, b, (((1,), (1,)), ((), ())), preferred_element_type=F32)


def _sigmoid(x):
    return 1.0 / (1.0 + jnp.exp(-x))


def _softplus(x):
    return jnp.maximum(x, 0.0) + jnp.log(1.0 + jnp.exp(-jnp.abs(x)))


def _rms(x, g):
    return x * lax.rsqrt(jnp.mean(x * x, axis=-1, keepdims=True) + RMS_EPS) * g


def _mod_kernel(c_ref, w_ref, b_ref, o_ref):
    c = c_ref[...]
    s = c * _sigmoid(c)
    o_ref[0] = jnp.dot(s, w_ref[0], precision=lax.Precision.HIGHEST, preferred_element_type=F32) + b_ref[0]


def _modulation(c_all, w_mod, b_mod):
    depth, d, d6 = w_mod.shape
    rows = c_all.shape[0]
    tn = 1024
    return pl.pallas_call(
        _mod_kernel,
        grid=(depth, d6 // tn),
        in_specs=[pl.BlockSpec((rows, d), lambda l, j: (0, 0)),
                  pl.BlockSpec((1, d, tn), lambda l, j: (l, 0, j)),
                  pl.BlockSpec((1, 1, tn), lambda l, j: (l, 0, j))],
        out_specs=pl.BlockSpec((1, rows, tn), lambda l, j: (l, 0, j)),
        out_shape=jax.ShapeDtypeStruct((depth, rows, d6), F32),
        compiler_params=_params(("arbitrary", "arbitrary")),
        name="modulation",
    )(c_all, w_mod, b_mod.reshape(depth, 1, d6))


def _mod_spec(mod, tm, tiles_per_group):
    _, r, d = mod.shape
    return pl.BlockSpec((1, r, d), lambda i, *_: (i // tiles_per_group, 0, 0))


TRANSPOSED_SECTIONS = (0, 2, 3, 5)


def _inproj_kernel(x_ref, g_ref, shift_ref, scale_ref, wqkv_ref, wf_ref, wg_ref, bf_ref, *refs, tiles_per_group):
    qkv_ref, qvt_ref, ksb_ref, vsb_ref, kfx_ref, vfx_ref, lf_ref, sg_ref, ksq_ref = refs[-9:]
    tm = x_ref.shape[0]
    h = _rms(x_ref[...], g_ref[...]) * (1.0 + scale_ref[0]) + shift_ref[0]
    hb = h.astype(BF16)
    f32_outs = {1: ksb_ref, 2: vsb_ref, 4: kfx_ref, 5: vfx_ref}
    for c in range(6):
        cols = slice(c * WIDTH, (c + 1) * WIDTH)
        acc = _dot(hb, wqkv_ref[:, cols])
        if c in f32_outs:
            by_head = acc.reshape(tm, N_HEADS, HEAD_DIM)
            f32_outs[c][0] = by_head
            if c == 4:
                col = lax.broadcasted_iota(jnp.int32, (WIDTH, LANES), 0)
                out = lax.broadcasted_iota(jnp.int32, (WIDTH, LANES), 1)
                head_of = jnp.where(col // HEAD_DIM == out, 1.0, 0.0).astype(BF16)
                per_head = _dot((acc * acc).astype(BF16), head_of)
                ksq = jnp.max(per_head, axis=0, keepdims=True)[:, :N_HEADS]
                first_tile = pl.program_id(0) % tiles_per_group == 0

                @pl.when(first_tile)
                def _():
                    ksq_ref[0] = ksq

                @pl.when(jnp.logical_not(first_tile))
                def _():
                    ksq_ref[0] = jnp.maximum(ksq_ref[0], ksq)
        if c in (0, 3):
            acc = acc * (HEAD_DIM ** -0.5 * (LOG2_E if c == 3 else 1.0))
        qkv_ref[:, cols] = acc.astype(BF16)
        if c in TRANSPOSED_SECTIONS:
            t = TRANSPOSED_SECTIONS.index(c)
            qvt_ref[t * WIDTH:(t + 1) * WIDTH, :] = acc.T.astype(BF16)
    f = _dot(hb, wf_ref[...])[:, :N_HEADS] + bf_ref[...]
    lf_ref[0] = -_softplus(-f)
    for c in range(wg_ref.shape[1] // WIDTH):
        cols = slice(c * WIDTH, (c + 1) * WIDTH)
        sg_ref[:, cols] = _sigmoid(_dot(hb, wg_ref[:, cols])).astype(BF16)


def _in_projection(x, g_pre, shift, scale, wqkv, wf, wg, b_forget, tm, layer, depth, stacked):
    n, d = x.shape
    tiles_per_group = (n // tm) // shift.shape[0]
    const = lambda i: (0, 0)
    row = lambda i: (i, 0)
    heads_spec = pl.BlockSpec((1, tm, N_HEADS, HEAD_DIM), lambda i: (layer, i, 0, 0))
    heads_shape = jax.ShapeDtypeStruct((depth, n, N_HEADS, HEAD_DIM), F32)
    n_in = 8
    stacked = () if stacked is None else tuple(stacked)
    n_groups = shift.shape[0]
    return pl.pallas_call(
        functools.partial(_inproj_kernel, tiles_per_group=tiles_per_group),
        grid=(n // tm,),
        in_specs=[pl.BlockSpec((tm, d), row),
                  pl.BlockSpec((1, d), const),
                  _mod_spec(shift, tm, tiles_per_group),
                  _mod_spec(scale, tm, tiles_per_group),
                  pl.BlockSpec(wqkv.shape, const),
                  pl.BlockSpec(wf.shape, const),
                  pl.BlockSpec(wg.shape, const),
                  pl.BlockSpec((1, N_HEADS), const)] + [pl.BlockSpec(memory_space=pl.ANY)] * len(stacked),
        out_specs=[pl.BlockSpec((tm, 6 * WIDTH), row), pl.BlockSpec((len(TRANSPOSED_SECTIONS) * WIDTH, tm), lambda i: (0, i))]
                  + [heads_spec] * 4
                  + [pl.BlockSpec((1, tm, N_HEADS), lambda i: (layer, i, 0)), pl.BlockSpec((tm, wg.shape[1]), row),
                     pl.BlockSpec((1, 1, N_HEADS), lambda i: (i // tiles_per_group, 0, 0))],
        out_shape=[jax.ShapeDtypeStruct((n, 6 * WIDTH), BF16),
                   jax.ShapeDtypeStruct((len(TRANSPOSED_SECTIONS) * WIDTH, n), BF16)] + [heads_shape] * 4
                  + [jax.ShapeDtypeStruct((depth, n, N_HEADS), F32), jax.ShapeDtypeStruct((n, wg.shape[1]), BF16),
                     jax.ShapeDtypeStruct((n_groups, 1, N_HEADS), F32)],
        input_output_aliases={n_in + k: 2 + k for k in range(len(stacked))},
        compiler_params=_params(("arbitrary",)),
        name="in_projection",
    )(x, g_pre, shift, scale, wqkv, wf, wg, b_forget, *stacked)


CUMSUM_CHUNK = 256


FORGET_SPLIT = 3


def _cumsum_kernel(x_ref, o_ref, kcol_ref=None):
    t = x_ref.shape[2]
    r = lax.broadcasted_iota(jnp.int32, (CUMSUM_CHUNK, CUMSUM_CHUNK), 0)
    c = lax.broadcasted_iota(jnp.int32, (CUMSUM_CHUNK, CUMSUM_CHUNK), 1)
    upper = jnp.where(r <= c, 1.0, 0.0).astype(F32)
    row = lax.broadcasted_iota(jnp.int32, (LANES, LANES), 0)
    lane = lax.broadcasted_iota(jnp.int32, (LANES, LANES), 1)
    head, term = row % N_HEADS, row // N_HEADS
    placed = jnp.logical_and(row < FORGET_SPLIT * N_HEADS, lane == (head % 2) * FORGET_SPLIT + term)
    pad_rows = jnp.zeros((LANES - FORGET_SPLIT * N_HEADS, CUMSUM_CHUNK), F32)

    def step(i, carry):
        start = pl.multiple_of(i * CUMSUM_CHUNK, CUMSUM_CHUNK)
        seg = x_ref[0, :, pl.ds(start, CUMSUM_CHUNK)]
        cs = jnp.dot(seg, upper, precision=lax.Precision.HIGHEST, preferred_element_type=F32) + carry
        f_log2 = cs * LOG2_E
        o_ref[0, :, pl.ds(start, CUMSUM_CHUNK)] = f_log2
        if kcol_ref is None:
            return cs[:, CUMSUM_CHUNK - 1:CUMSUM_CHUNK]
        rest = -f_log2
        terms = []
        for _ in range(FORGET_SPLIT):
            part = pltpu.bitcast(pltpu.bitcast(rest, jnp.uint32) & jnp.uint32(0xFFFF0000), F32)
            terms.append(part)
            rest = rest - part
        terms_t = jnp.concatenate(terms + [pad_rows], axis=0).T.astype(BF16)
        for p in range(HEAD_PAIRS):
            select = jnp.where(jnp.logical_and(placed, head // 2 == p), 1.0, 0.0).astype(BF16)
            kcol_ref[0, p, pl.ds(start, CUMSUM_CHUNK), :] = _dot(terms_t, select).astype(BF16)
        return cs[:, CUMSUM_CHUNK - 1:CUMSUM_CHUNK]

    lax.fori_loop(0, t // CUMSUM_CHUNK, step, jnp.zeros((x_ref.shape[1], 1), F32))


def _forget_cumsum(x, key_columns):
    b, r, t = x.shape
    out_specs = [pl.BlockSpec((1, r, t), lambda i: (i, 0, 0))]
    out_shape = [jax.ShapeDtypeStruct((b, r, t), F32)]
    if key_columns:
        out_specs.append(pl.BlockSpec((1, HEAD_PAIRS, t, LANES), lambda i: (i, 0, 0, 0)))
        out_shape.append(jax.ShapeDtypeStruct((b, HEAD_PAIRS, t, LANES), BF16))
    return pl.pallas_call(
        _cumsum_kernel,
        grid=(b,),
        in_specs=[pl.BlockSpec((1, r, t), lambda i: (i, 0, 0))],
        out_specs=out_specs,
        out_shape=out_shape,
        compiler_params=_params(("arbitrary",)),
        name="forget_cumsum",
    )(x)


def _pair_layouts(cum_t, t_query_start, t_query):
    b, _, tp = cum_t.shape
    f_keys = cum_t.reshape(b, HEAD_PAIRS, 2, tp)
    f_query = jnp.swapaxes(f_keys[:, :, :, t_query_start:t_query_start + t_query], 2, 3)
    return f_keys, f_query


def _head_masks():
    lane = lax.broadcasted_iota(jnp.int32, (1, LANES), 1)
    return lane < HEAD_DIM


def _split_heads(q2, first_head):
    zero = jnp.zeros_like(q2)
    return jnp.where(first_head, q2, zero), jnp.where(first_head, zero, q2)


def _strict_lower_neg(n):
    r = lax.broadcasted_iota(jnp.int32, (n, n), 0)
    c = lax.broadcasted_iota(jnp.int32, (n, n), 1)
    return jnp.where(r > c, -1.0, 0.0).astype(BF16)


def _sb_block(qh, k2, v2, neg_tri, visible, acc_ref, r_ref, hd):
    z = _dot_nt(qh, k2)
    sp = _softplus(z)
    if visible is not None:
        sp = jnp.where(visible, sp, 0.0)
    later = _dot(sp.astype(BF16), neg_tri)
    p = jnp.exp(z - sp + later)
    if visible is not None:
        p = jnp.where(visible, p, 0.0)
    r = r_ref[hd]
    acc_ref[hd] += jnp.exp(r) * _dot(p.astype(BF16), v2)
    r_ref[hd] = r + later[:, 0:1] - sp[:, 0:1]


def _fox_block(qh, k2, v2_ones, fq, fk, allowed, acc_ref, m_ref, hd):
    s = _dot_nt(qh, k2) + fq - fk
    if allowed is not None:
        s = jnp.where(allowed, s, -jnp.inf)
    m_old = m_ref[hd]
    m_new = jnp.maximum(m_old, jnp.max(s, axis=-1, keepdims=True))
    p = jnp.exp2(s - m_new)
    acc_ref[hd] = jnp.exp2(m_old - m_new) * acc_ref[hd] + _dot(p.astype(BF16), v2_ones)
    m_ref[hd] = m_new


def _fox_finish(acc_ref, first_head):
    a, b = acc_ref[0], acc_ref[1]
    num = jnp.where(first_head, a, b)
    den = jnp.where(first_head, pltpu.roll(a, HEAD_DIM, 1), pltpu.roll(b, HEAD_DIM, 1))
    return num / den


def _with_ones(v2, first_head):
    one = jnp.ones_like(v2)
    return jnp.where(first_head, v2, one), jnp.where(first_head, one, v2)


def _sb_prompt_kernel(qt_ref, k_ref, vt_ref, o_ref, w_ref, acc_ref, r_ref, tri_ref, *, tq):
    i = pl.program_id(2)
    n_pairs = qt_ref.shape[0] // LANES
    sub = lax.broadcasted_iota(jnp.int32, (LANES, tq), 0)
    first_rows = sub < HEAD_DIM
    for pp in range(n_pairs):
        qt = qt_ref[pp * LANES:(pp + 1) * LANES, :]
        zero = jnp.zeros_like(qt)
        w_ref[2 * pp] = jnp.where(first_rows, qt, zero)
        w_ref[2 * pp + 1] = jnp.where(first_rows, zero, qt)
    acc_ref[...] = jnp.zeros_like(acc_ref)
    r_ref[...] = jnp.zeros_like(r_ref)

    @pl.when(jnp.logical_and(jnp.logical_and(pl.program_id(0) == 0, pl.program_id(1) == 0), i == 0))
    def _():
        s_idx = lax.broadcasted_iota(jnp.int32, tri_ref.shape, 0)
        j_idx = lax.broadcasted_iota(jnp.int32, tri_ref.shape, 1)
        tri_ref[...] = jnp.where(j_idx > s_idx, -1.0, 0.0).astype(BF16)

    def block(start, size, vis):
        neg_tri = tri_ref[:size, :size]
        for hd in range(2 * n_pairs):
            lanes = slice((hd // 2) * LANES, (hd // 2 + 1) * LANES)
            keys = k_ref[0, pl.ds(start, size), lanes]
            vt = vt_ref[lanes, pl.ds(start, size)]
            z = _dot(keys, w_ref[hd])
            sp = _softplus(z)
            if vis is not None:
                sp = jnp.where(vis, sp, 0.0)
            later = _dot(neg_tri, sp.astype(BF16))
            p = jnp.exp(z - sp + later)
            if vis is not None:
                p = jnp.where(vis, p, 0.0)
            r = r_ref[hd]
            acc_ref[hd] += jnp.exp(r) * _dot(vt, p.astype(BF16))
            r_ref[hd] = r + later[0:1, :] - sp[0:1, :]

    def live():
        worst = r_ref[0]
        for hd in range(1, 2 * n_pairs):
            worst = jnp.maximum(worst, r_ref[hd])
        return jnp.max(worst)

    first = jnp.maximum(i - 1, 0)
    ahead = lax.broadcasted_iota(jnp.int32, (2 * tq, tq), 0) - lax.broadcasted_iota(jnp.int32, (2 * tq, tq), 1)
    block(pl.multiple_of(first * tq, tq), 2 * tq, ahead < (i - first) * tq)

    def cond(carry):
        j, r_max = carry
        return jnp.logical_and(j >= 0, r_max > SB_DEAD_LOG)

    def body(carry):
        j, _ = carry
        block(pl.multiple_of(j * tq, tq), tq, None)
        return j - 1, live()

    lax.while_loop(cond, body, (i - 2, live()))
    for pp in range(n_pairs):
        out_t = jnp.where(first_rows, acc_ref[2 * pp], acc_ref[2 * pp + 1])
        o_ref[0, :, pp * LANES:(pp + 1) * LANES] = out_t.T.astype(o_ref.dtype)


SB_PAIRS_PER_STEP = 2


def _sb_prompt(qkv, qv_t, tq):
    b, t, _ = qkv.shape
    nq = t // tq
    group = SB_PAIRS_PER_STEP * LANES
    n_groups = WIDTH // group
    heads = 2 * SB_PAIRS_PER_STEP
    kernel = functools.partial(_sb_prompt_kernel, tq=tq)
    return pl.pallas_call(
        kernel,
        grid=(b, n_groups, nq),
        in_specs=[pl.BlockSpec((group, tq), lambda b_, p, i: (p, b_ * nq + i)),
                  pl.BlockSpec((1, t, group), lambda b_, p, i: (b_, 0, n_groups + p)),
                  pl.BlockSpec((group, t), lambda b_, p, i: (n_groups + p, b_))],
        out_specs=pl.BlockSpec((1, tq, group), lambda b_, p, i: (b_, i, p)),
        out_shape=jax.ShapeDtypeStruct((b, t, WIDTH), BF16),
        scratch_shapes=[pltpu.VMEM((heads, LANES, tq), BF16), pltpu.VMEM((heads, LANES, tq), F32),
                        pltpu.VMEM((heads, 1, tq), F32), pltpu.VMEM((2 * tq, 2 * tq), BF16)],
        compiler_params=_params(("arbitrary", "arbitrary", "arbitrary")),
        name="sb_attention_prompt",
    )(qv_t, qkv, qv_t)


FOX_DEAD_LOG2 = -150.0
FOX_BOUNDED_MAX = 40.0


def _fox_prompt_kernel(kmax_ref, fend_ref, qt_ref, k_ref, kf_ref, vt_ref, fq_ref, o_ref,
                       w_ref, acc_ref, m_ref, zb_ref, *, tq):
    bi = pl.program_id(0)
    pi = pl.program_id(1)
    i = pl.program_id(2)
    sub = lax.broadcasted_iota(jnp.int32, (LANES, tq), 0)
    first_rows = sub < HEAD_DIM
    qt = qt_ref[...]
    zero = jnp.zeros_like(qt)
    for hd in range(2):
        lo = hd * FORGET_SPLIT
        select = jnp.where(jnp.logical_and(sub >= lo, sub < lo + FORGET_SPLIT), 1.0, 0.0).astype(BF16)
        q_head = jnp.where(first_rows, qt, zero) if hd == 0 else jnp.where(first_rows, zero, qt)
        w_ref[hd, :LANES, :] = q_head
        w_ref[hd, LANES:, :] = select
        q_f32 = q_head.astype(F32)
        q_norm = jnp.sqrt(jnp.sum(q_f32 * q_f32, axis=0, keepdims=True))
        zb_ref[hd] = q_norm * (kmax_ref[bi, 2 * pi + hd] * 1.01)
    acc_ref[...] = jnp.zeros_like(acc_ref)
    m_ref[...] = jnp.full_like(m_ref, -jnp.inf)
    key_idx = lax.broadcasted_iota(jnp.int32, (tq, tq), 0)
    query_idx = lax.broadcasted_iota(jnp.int32, (tq, tq), 1)
    allowed = key_idx <= query_idx
    den_row = (HEAD_DIM, 0)

    def operands(j):
        start = pl.multiple_of(j * tq, tq)
        keys = jnp.concatenate([k_ref[0, pl.ds(start, tq), :], kf_ref[0, 0, pl.ds(start, tq), :]], axis=1)
        vt = vt_ref[:, pl.ds(start, tq)]
        one = jnp.ones_like(vt)
        first_v = lax.broadcasted_iota(jnp.int32, vt.shape, 0) < HEAD_DIM
        return keys, (jnp.where(first_v, vt, one), jnp.where(first_v, one, vt))

    def block_bounded(j, mask):
        keys, vts = operands(j)
        for hd in range(2):
            u = _dot(keys, w_ref[hd])
            if mask is not None:
                u = jnp.where(mask, u, -jnp.inf)
            p = jnp.exp2(u + (fq_ref[0, 0, hd:hd + 1, :] - zb_ref[hd]))
            acc_ref[hd] += _dot(vts[hd], p.astype(BF16))

    def alive_bounded(j):
        jc = jnp.maximum(j, 0)
        reach = []
        for hd in range(2):
            den = acc_ref[hd, den_row[hd]:den_row[hd] + 1, :]
            gap = jnp.max(fq_ref[0, 0, hd:hd + 1, :] - jnp.log2(den))
            reach.append(gap - fend_ref[bi, pi, hd, jc])
        return jnp.maximum(reach[0], reach[1]) + tq.bit_length() > FOX_DEAD_LOG2

    def block_exact(j, mask):
        keys, vts = operands(j)
        for hd in range(2):
            u = _dot(keys, w_ref[hd])
            if mask is not None:
                u = jnp.where(mask, u, -jnp.inf)
            fq = fq_ref[0, 0, hd:hd + 1, :]
            m_old = m_ref[hd]
            m_new = jnp.maximum(m_old, jnp.max(u, axis=0, keepdims=True) + fq)
            p = jnp.exp2(u + (fq - m_new))
            acc_ref[hd] = jnp.exp2(m_old - m_new) * acc_ref[hd] + _dot(vts[hd], p.astype(BF16))
            m_ref[hd] = m_new

    def alive_exact(j):
        jc = jnp.maximum(j, 0)
        reach = [jnp.max(zb_ref[hd] + fq_ref[0, 0, hd:hd + 1, :] - m_ref[hd]) - fend_ref[bi, pi, hd, jc]
                 for hd in range(2)]
        return jnp.maximum(reach[0], reach[1]) > FOX_DEAD_LOG2

    def sweep(block, alive):
        block(i, allowed)

        def cond(carry):
            j, go = carry
            return jnp.logical_and(j >= 0, go)

        def body(carry):
            j, _ = carry
            block(j, None)
            return j - 1, alive(j - 1)

        lax.while_loop(cond, body, (i - 1, alive(i - 1)))

    bounded = jnp.max(jnp.maximum(zb_ref[0], zb_ref[1])) <= FOX_BOUNDED_MAX

    @pl.when(bounded)
    def _():
        sweep(block_bounded, alive_bounded)

    @pl.when(jnp.logical_not(bounded))
    def _():
        sweep(block_exact, alive_exact)

    a, b = acc_ref[0], acc_ref[1]
    out_t = jnp.where(first_rows, a / a[HEAD_DIM:HEAD_DIM + 1, :], b / b[0:1, :])
    o_ref[0] = out_t.T.astype(o_ref.dtype)


def _fox_prompt(qkv, qv_t, cum_t, key_cols, k_max, tq):
    b, t, _ = qkv.shape
    nq = t // tq
    f_rows = cum_t.reshape(b, HEAD_PAIRS, 2, t)
    f_block_end = f_rows[:, :, :, tq - 1::tq]
    smem = pl.BlockSpec(memory_space=pltpu.SMEM)
    kernel = functools.partial(_fox_prompt_kernel, tq=tq)
    return pl.pallas_call(
        kernel,
        grid=(b, HEAD_PAIRS, nq),
        in_specs=[smem, smem,
                  pl.BlockSpec((LANES, tq), lambda b_, p, i: (2 * HEAD_PAIRS + p, b_ * nq + i)),
                  pl.BlockSpec((1, t, LANES), lambda b_, p, i: (b_, 0, 4 * HEAD_PAIRS + p)),
                  pl.BlockSpec((1, 1, t, LANES), lambda b_, p, i: (b_, p, 0, 0)),
                  pl.BlockSpec((LANES, t), lambda b_, p, i: (3 * HEAD_PAIRS + p, b_)),
                  pl.BlockSpec((1, 1, 2, tq), lambda b_, p, i: (b_, p, 0, i))],
        out_specs=pl.BlockSpec((1, tq, LANES), lambda b_, p, i: (b_, i, p)),
        out_shape=jax.ShapeDtypeStruct((b, t, WIDTH), BF16),
        scratch_shapes=[pltpu.VMEM((2, 2 * LANES, tq), BF16), pltpu.VMEM((2, LANES, tq), F32),
                        pltpu.VMEM((2, 1, tq), F32), pltpu.VMEM((2, 1, tq), F32)],
        compiler_params=_params(("arbitrary", "arbitrary", "arbitrary")),
        name="fox_attention_prompt",
    )(k_max, f_block_end, qv_t, qkv, key_cols, qv_t, f_rows)


def _sample_attn_kernel(qsb_ref, ksb_ref, vsb_ref, qfx_ref, kfx_ref, vfx_ref,
                        csk_ref, csv_ref, cfk_ref, cfv_ref, fq_ref, fk_ref,
                        osb_ref, ofx_ref, acc_ref, st_ref, *, tk):
    t_new = qsb_ref.shape[1]
    n_past = csk_ref.shape[1]
    n_blocks = n_past // tk
    first_head = _head_masks()
    row = lax.broadcasted_iota(jnp.int32, (t_new, t_new), 0)
    col = lax.broadcasted_iota(jnp.int32, (t_new, t_new), 1)

    qs = _split_heads(qsb_ref[0], first_head)
    acc_ref[...] = jnp.zeros_like(acc_ref)
    st_ref[...] = jnp.zeros_like(st_ref)
    tri_new = _strict_lower_neg(t_new)
    tri_past = _strict_lower_neg(tk)
    for hd in range(2):
        _sb_block(qs[hd], ksb_ref[0], vsb_ref[0], tri_new, col < row, acc_ref, st_ref, hd)

    def live():
        return jnp.max(jnp.maximum(st_ref[0], st_ref[1]))

    def sb_cond(carry):
        j, r_max = carry
        return jnp.logical_and(j >= 0, r_max > SB_DEAD_LOG)

    def sb_body(carry):
        j, _ = carry
        start = pl.multiple_of(j * tk, tk)
        k2 = csk_ref[0, pl.ds(start, tk), :].astype(BF16)
        v2 = csv_ref[0, pl.ds(start, tk), :].astype(BF16)
        for hd in range(2):
            _sb_block(qs[hd], k2, v2, tri_past, None, acc_ref, st_ref, hd)
        return j - 1, live()

    lax.while_loop(sb_cond, sb_body, (n_blocks - 1, live()))
    osb_ref[0] = jnp.where(first_head, acc_ref[0], acc_ref[1]).astype(osb_ref.dtype)

    qs = _split_heads(qfx_ref[0], first_head)
    acc_ref[...] = jnp.zeros_like(acc_ref)
    st_ref[...] = jnp.full_like(st_ref, -jnp.inf)
    fq = fq_ref[0, 0]
    vs = _with_ones(vfx_ref[0], first_head)
    for hd in range(2):
        fk = fk_ref[0, 0, hd:hd + 1, n_past:n_past + t_new]
        _fox_block(qs[hd], kfx_ref[0], vs[hd], fq[:, hd:hd + 1], fk, col <= row, acc_ref, st_ref, hd)

    k2 = cfk_ref[0].astype(BF16)
    vs_ = _with_ones(cfv_ref[0].astype(BF16), first_head)
    for hd in range(2):
        fk = fk_ref[0, 0, hd:hd + 1, 0:n_past]
        _fox_block(qs[hd], k2, vs_[hd], fq[:, hd:hd + 1], fk, None, acc_ref, st_ref, hd)
    ofx_ref[0] = _fox_finish(acc_ref, first_head).astype(ofx_ref.dtype)


def _sample_attention(qkv, cache_sb_k, cache_sb_v, cache_fox_k, cache_fox_v, f_query, f_keys, tk):
    b, t_new, _ = qkv.shape
    n_past = cache_sb_k.shape[1]
    tp = f_keys.shape[3]
    new = lambda c: pl.BlockSpec((1, t_new, LANES), lambda b_, p: (b_, 0, c * HEAD_PAIRS + p))
    past = pl.BlockSpec((1, n_past, LANES), lambda b_, p: (b_, 0, p))
    out = pl.BlockSpec((1, t_new, LANES), lambda b_, p: (b_, 0, p))
    kernel = functools.partial(_sample_attn_kernel, tk=tk)
    return pl.pallas_call(
        kernel,
        grid=(b, HEAD_PAIRS),
        in_specs=[new(0), new(1), new(2), new(3), new(4), new(5), past, past, past, past,
                  pl.BlockSpec((1, 1, t_new, 2), lambda b_, p: (b_, p, 0, 0)),
                  pl.BlockSpec((1, 1, 2, tp), lambda b_, p: (b_, p, 0, 0))],
        out_specs=[out, out],
        out_shape=[jax.ShapeDtypeStruct((b, t_new, WIDTH), BF16)] * 2,
        scratch_shapes=[pltpu.VMEM((2, t_new, LANES), F32), pltpu.VMEM((2, t_new, 1), F32)],
        compiler_params=_params(("arbitrary", "arbitrary")),
        name="attention_sample",
    )(qkv, qkv, qkv, qkv, qkv, qkv, cache_sb_k, cache_sb_v, cache_fox_k, cache_fox_v, f_query, f_keys)


def _mix_out_kernel(x_ref, osb_ref, ofx_ref, sg_ref, gate_ref, g_ref, wsb_ref, wfx_ref, wo_ref, o_ref):
    d = x_ref.shape[1]
    merged = (sg_ref[:, :d].astype(F32) * _dot(osb_ref[...], wsb_ref[...])
              + sg_ref[:, d:].astype(F32) * _dot(ofx_ref[...], wfx_ref[...]))
    y = _dot(merged.astype(BF16), wo_ref[...])
    o_ref[...] = x_ref[...] + gate_ref[0] * _rms(y, g_ref[...])


def _mix_out(x, o_sb, o_fx, sg, gate, g_post, w_sb_out, w_fox_out, w_out, tm):
    n, d = x.shape
    tiles_per_group = (n // tm) // gate.shape[0]
    const = lambda i: (0, 0)
    row = lambda i: (i, 0)
    return pl.pallas_call(
        _mix_out_kernel,
        grid=(n // tm,),
        in_specs=[pl.BlockSpec((tm, d), row), pl.BlockSpec((tm, WIDTH), row), pl.BlockSpec((tm, WIDTH), row),
                  pl.BlockSpec((tm, 2 * d), row), _mod_spec(gate, tm, tiles_per_group),
                  pl.BlockSpec((1, d), const), pl.BlockSpec(w_sb_out.shape, const),
                  pl.BlockSpec(w_fox_out.shape, const), pl.BlockSpec(w_out.shape, const)],
        out_specs=pl.BlockSpec((tm, d), row),
        out_shape=jax.ShapeDtypeStruct((n, d), F32),
        compiler_params=_params(("arbitrary",)),
        name="mix_out",
    )(x, o_sb, o_fx, sg, gate, g_post, w_sb_out, w_fox_out, w_out)


def _swiglu_chunk(hb, wa, wb, wd):
    a = _dot(hb, wa)
    b = _dot(hb, wb)
    return a * _sigmoid(a) * b, wd


def _ffn_kernel(x_ref, gpre_ref, shift_ref, scale_ref, gate_ref, gpost_ref, wa_ref, wb_ref, wd_ref,
                o_ref, h_ref, acc_ref):
    f = pl.program_id(1)

    @pl.when(f == 0)
    def _():
        h = _rms(x_ref[...], gpre_ref[...]) * (1.0 + scale_ref[0]) + shift_ref[0]
        h_ref[...] = h.astype(BF16)
        acc_ref[...] = jnp.zeros_like(acc_ref)

    hb = h_ref[...]
    a = _dot(hb, wa_ref[...])
    b = _dot(hb, wb_ref[...])
    act = a * _sigmoid(a) * b
    acc_ref[...] += _dot(act.astype(BF16), wd_ref[...])

    @pl.when(f == pl.num_programs(1) - 1)
    def _():
        o_ref[...] = x_ref[...] + gate_ref[0] * _rms(acc_ref[...], gpost_ref[...])


def _ffn(x, g_pre, shift, scale, gate, g_post, w_gate_up, w_down, tm, tf):
    n, d = x.shape
    d_ff = w_down.shape[0]
    nf = d_ff // tf
    tiles_per_group = (n // tm) // shift.shape[0]
    const = lambda i, f: (0, 0)
    row = lambda i, f: (i, 0)
    return pl.pallas_call(
        _ffn_kernel,
        grid=(n // tm, nf),
        in_specs=[pl.BlockSpec((tm, d), row), pl.BlockSpec((1, d), const),
                  _mod_spec(shift, tm, tiles_per_group), _mod_spec(scale, tm, tiles_per_group),
                  _mod_spec(gate, tm, tiles_per_group), pl.BlockSpec((1, d), const),
                  pl.BlockSpec((d, tf), lambda i, f: (0, f)),
                  pl.BlockSpec((d, tf), lambda i, f: (0, nf + f)),
                  pl.BlockSpec((tf, d), lambda i, f: (f, 0))],
        out_specs=pl.BlockSpec((tm, d), row),
        out_shape=jax.ShapeDtypeStruct((n, d), F32),
        scratch_shapes=[pltpu.VMEM((tm, d), BF16), pltpu.VMEM((tm, d), F32)],
        compiler_params=_params(("arbitrary", "arbitrary")),
        name="ffn_dense",
    )(x, g_pre, shift, scale, gate, g_post, w_gate_up, w_gate_up, w_down)


def _route_kernel(x_ref, gpre_ref, shift_ref, scale_ref, wr_ref, h_ref, idx_ref, w_ref, rank_ref, count_ref,
                  *, n_experts):
    tm = x_ref.shape[0]

    @pl.when(pl.program_id(0) == 0)
    def _():
        count_ref[...] = jnp.zeros_like(count_ref)

    h = _rms(x_ref[...], gpre_ref[...]) * (1.0 + scale_ref[0]) + shift_ref[0]
    h_ref[...] = h
    logits = jnp.dot(h, wr_ref[...], precision=lax.Precision.HIGHEST, preferred_element_type=F32)[:, :n_experts]
    idx = lax.broadcasted_iota(jnp.int32, logits.shape, 1)
    m1 = jnp.max(logits, axis=-1, keepdims=True)
    i1 = jnp.min(jnp.where(logits == m1, idx, n_experts), axis=-1, keepdims=True)
    rest = jnp.where(idx == i1, -jnp.inf, logits)
    m2 = jnp.max(rest, axis=-1, keepdims=True)
    i2 = jnp.min(jnp.where(rest == m2, idx, n_experts), axis=-1, keepdims=True)
    e2 = jnp.exp(m2 - m1)
    slot = lax.broadcasted_iota(jnp.int32, idx_ref.shape, 1)
    idx_ref[...] = jnp.where(slot == 0, i1, i2)
    w_ref[...] = jnp.where(slot == 0, 1.0 / (1.0 + e2), e2 / (1.0 + e2))
    lane = lax.broadcasted_iota(jnp.int32, (tm, LANES), 1)
    first, second = lane == i1, lane == i2
    chosen = jnp.where(jnp.logical_or(first, second), 1.0, 0.0)
    t_row = lax.broadcasted_iota(jnp.int32, (tm, tm), 0)
    t_col = lax.broadcasted_iota(jnp.int32, (tm, tm), 1)
    earlier = jnp.where(t_col < t_row, 1.0, 0.0).astype(BF16)
    before = count_ref[0:1, :] + _dot(earlier, chosen.astype(BF16))
    rank1 = jnp.sum(jnp.where(first, before, 0.0), axis=-1, keepdims=True)
    rank2 = jnp.sum(jnp.where(second, before, 0.0), axis=-1, keepdims=True)
    rank_ref[...] = jnp.where(slot == 0, rank1, rank2).astype(jnp.int32)
    count_ref[...] = count_ref[...] + jnp.sum(chosen, axis=0, keepdims=True)


def _route(x, g_pre, shift, scale, w_router, n_experts, tm):
    n, d = x.shape
    tiles_per_group = (n // tm) // shift.shape[0]
    const = lambda i: (0, 0)
    row = lambda i: (i, 0)
    return pl.pallas_call(
        functools.partial(_route_kernel, n_experts=n_experts),
        grid=(n // tm,),
        in_specs=[pl.BlockSpec((tm, d), row), pl.BlockSpec((1, d), const),
                  _mod_spec(shift, tm, tiles_per_group), _mod_spec(scale, tm, tiles_per_group),
                  pl.BlockSpec(w_router.shape, const)],
        out_specs=[pl.BlockSpec((tm, d), row), pl.BlockSpec((tm, TOP_K), row), pl.BlockSpec((tm, TOP_K), row),
                   pl.BlockSpec((tm, TOP_K), row), pl.BlockSpec((8, LANES), const)],
        out_shape=[jax.ShapeDtypeStruct((n, d), F32), jax.ShapeDtypeStruct((n, TOP_K), jnp.int32),
                   jax.ShapeDtypeStruct((n, TOP_K), F32), jax.ShapeDtypeStruct((n, TOP_K), jnp.int32),
                   jax.ShapeDtypeStruct((8, LANES), F32)],
        compiler_params=_params(("arbitrary",)),
        name="moe_route",
    )(x, g_pre, shift, scale, w_router)


def _dispatch_tables(top_idx, rank, counts, n_experts, tm):
    n = top_idx.shape[0]
    n_assign = n * TOP_K
    expert = top_idx.reshape(n_assign)
    padded = (counts + tm - 1) // tm * tm
    group_end = jnp.cumsum(padded)
    group_start = group_end - padded
    onehot = expert[:, None] == jnp.arange(n_experts, dtype=jnp.int32)[None, :]
    pos = jnp.sum(jnp.where(onehot, group_start[None, :], 0), axis=1) + rank.reshape(n_assign)
    p_rows = (-(-n_assign // tm) + n_experts) * tm
    src_token = jnp.zeros((p_rows,), jnp.int32).at[pos].set(jnp.arange(n_assign, dtype=jnp.int32) // TOP_K)
    tile_start = jnp.arange(p_rows // tm, dtype=jnp.int32) * tm
    tile_expert = jnp.minimum(jnp.sum(tile_start[:, None] >= group_end[None, :], axis=1), n_experts - 1)
    n_used = (group_end[-1] // tm).reshape(1)
    return src_token, tile_expert.astype(jnp.int32), n_used.astype(jnp.int32), pos.astype(jnp.int32)


def _gather_rows(src_hbm, dst, sem, index_of_row, n_rows, unroll=8):
    def issue(r, carry):
        pltpu.make_async_copy(src_hbm.at[pl.ds(index_of_row(r), 1), :], dst.at[pl.ds(r, 1), :], sem).start()
        return carry

    lax.fori_loop(0, n_rows, issue, 0, unroll=unroll)


def _wait_rows(src_hbm, dst, sem):
    pltpu.make_async_copy(src_hbm.at[pl.ds(0, dst.shape[0]), :], dst, sem).wait()


def _experts_kernel(src_ref, expert_ref, used_ref, h_hbm, wa_ref, wb_ref, wd_ref, o_ref,
                    rows_ref, xb_ref, acc_ref, sem, *, tm):
    i = pl.program_id(0)
    f = pl.program_id(1)
    nf = pl.num_programs(1)
    n_used = used_ref[0]
    slot = lax.rem(i, 2)
    active = i < n_used
    rows_per_step = tm // nf

    @pl.when(jnp.logical_and(f == 0, i == 0))
    def _():
        _gather_rows(h_hbm, rows_ref.at[0], sem.at[0], lambda r: src_ref[r], tm)

    @pl.when(jnp.logical_and(f == 0, i <= n_used))
    def _():
        _wait_rows(h_hbm, rows_ref.at[slot], sem.at[slot])

    @pl.when(jnp.logical_and(f == 0, active))
    def _():
        xb_ref[...] = rows_ref[slot].astype(BF16)
        acc_ref[...] = jnp.zeros_like(acc_ref)

    @pl.when(active)
    def _():
        base = (i + 1) * tm + f * rows_per_step
        nxt = rows_ref.at[1 - slot]
        for r in range(rows_per_step):
            row = f * rows_per_step + r
            pltpu.make_async_copy(h_hbm.at[pl.ds(src_ref[base + r], 1), :], nxt.at[pl.ds(row, 1), :],
                                  sem.at[1 - slot]).start()
        xb = xb_ref[...]
        a = _dot(xb, wa_ref[0])
        b = _dot(xb, wb_ref[0])
        act = a * _sigmoid(a) * b
        acc_ref[...] += _dot(act.astype(BF16), wd_ref[0])

    last = f == pl.num_programs(1) - 1

    @pl.when(jnp.logical_and(last, active))
    def _():
        o_ref[...] = acc_ref[...]

    @pl.when(jnp.logical_and(last, jnp.logical_not(active)))
    def _():
        o_ref[...] = jnp.zeros_like(o_ref)


def _experts(h, src_token, tile_expert, n_used, w_gate_up, w_down, tm, tf):
    p_rows = src_token.shape[0]
    d = h.shape[1]
    nf = w_down.shape[1] // tf
    ff = lambda i, f, used: jnp.where(i < used[0], f, nf - 1)
    grid_spec = pltpu.PrefetchScalarGridSpec(
        num_scalar_prefetch=3,
        grid=(p_rows // tm, nf),
        in_specs=[pl.BlockSpec(memory_space=pl.ANY),
                  pl.BlockSpec((1, d, tf), lambda i, f, src, ex, used: (ex[i], 0, ff(i, f, used))),
                  pl.BlockSpec((1, d, tf), lambda i, f, src, ex, used: (ex[i], 0, nf + ff(i, f, used))),
                  pl.BlockSpec((1, tf, d), lambda i, f, src, ex, used: (ex[i], ff(i, f, used), 0))],
        out_specs=pl.BlockSpec((tm, d), lambda i, f, src, ex, used: (i, 0)),
        scratch_shapes=[pltpu.VMEM((2, tm, d), F32), pltpu.VMEM((tm, d), BF16), pltpu.VMEM((tm, d), F32),
                        pltpu.SemaphoreType.DMA((2,))])
    return pl.pallas_call(
        functools.partial(_experts_kernel, tm=tm),
        grid_spec=grid_spec,
        out_shape=jax.ShapeDtypeStruct((p_rows, d), F32),
        compiler_params=_params(("arbitrary", "arbitrary")),
        name="moe_experts",
    )(src_token, tile_expert, n_used, h, w_gate_up, w_gate_up, w_down)


def _combine_kernel(pos_ref, y_hbm, x_ref, w_ref, gate_ref, gpost_ref, o_ref, rows_ref, sem, *, tm):
    i = pl.program_id(0)
    slot = lax.rem(i, 2)

    def gather(tile, into):
        for k in range(TOP_K):
            _gather_rows(y_hbm, rows_ref.at[into, k], sem.at[into],
                         lambda r, k=k: pos_ref[(tile * tm + r) * TOP_K + k], tm, unroll=True)

    @pl.when(i == 0)
    def _():
        gather(0, 0)

    for k in range(TOP_K):
        _wait_rows(y_hbm, rows_ref.at[slot, k], sem.at[slot])

    @pl.when(i + 1 < pl.num_programs(0))
    def _():
        gather(i + 1, 1 - slot)

    w = w_ref[...]
    mixed = w[:, 0:1] * rows_ref[slot, 0] + w[:, 1:2] * rows_ref[slot, 1]
    o_ref[...] = x_ref[...] + gate_ref[0] * _rms(mixed, gpost_ref[...])


def _combine(y, pos, top_w, x, gate, g_post, tm):
    n, d = x.shape
    tiles_per_group = (n // tm) // gate.shape[0]
    grid_spec = pltpu.PrefetchScalarGridSpec(
        num_scalar_prefetch=1,
        grid=(n // tm,),
        in_specs=[pl.BlockSpec(memory_space=pl.ANY),
                  pl.BlockSpec((tm, d), lambda i, pos_: (i, 0)),
                  pl.BlockSpec((tm, TOP_K), lambda i, pos_: (i, 0)),
                  _mod_spec(gate, tm, tiles_per_group),
                  pl.BlockSpec((1, d), lambda i, pos_: (0, 0))],
        out_specs=pl.BlockSpec((tm, d), lambda i, pos_: (i, 0)),
        scratch_shapes=[pltpu.VMEM((2, TOP_K, tm, d), F32), pltpu.SemaphoreType.DMA((2,))])
    return pl.pallas_call(
        functools.partial(_combine_kernel, tm=tm),
        grid_spec=grid_spec,
        out_shape=jax.ShapeDtypeStruct((n, d), F32),
        compiler_params=_params(("arbitrary",)),
        name="moe_combine",
    )(pos, y, x, top_w, gate, g_post)


def _moe(x, g_pre, shift, scale, gate, g_post, w_router, w_gate_up, w_down, n_experts, tm_route, tm, tf, tm_out):
    h, top_idx, top_w, rank, counts = _route(x, g_pre, shift, scale, w_router, n_experts, tm_route)
    counts = counts[0, :n_experts].astype(jnp.int32)
    src_token, tile_expert, n_used, pos = _dispatch_tables(top_idx, rank, counts, n_experts, tm)
    y = _experts(h, src_token, tile_expert, n_used, w_gate_up, w_down, tm, tf)
    return _combine(y, pos, top_w, x, gate, g_post, tm_out)


def _row_tile(n, want):
    return want if n % want == 0 else n


def kernel(x_prompt, x_sample, c_prompt, c_sample, cache_sb_k, cache_sb_v, cache_fox_k, cache_fox_v, cache_fox_logf, w_mod, b_mod, g_pre_mix, g_post_mix, g_pre_ffn, g_post_ffn, w_in, b_forget, w_sb_out, w_fox_out, w_out, w_ffn_gate_up, w_ffn_down, w_router, w_moe_gate_up, w_moe_down):
    bsz, seq, d = x_prompt.shape
    dec_b, dec_t, _ = x_sample.shape
    depth = w_mod.shape[0]
    n_past = cache_sb_k.shape[2]
    n_experts = w_router.shape[2]
    n_p, n_s = bsz * seq, dec_b * dec_t
    tq = _row_tile(seq, 256)
    past_chunk = _row_tile(n_past, 256)

    c_all = jnp.concatenate([c_prompt, c_sample], axis=0)
    c_rows = -(-c_all.shape[0] // 8) * 8
    c_all = jnp.pad(c_all, ((0, c_rows - c_all.shape[0]), (0, 0)))
    mod = _modulation(c_all, w_mod, b_mod)

    xp = x_prompt.reshape(n_p, d)
    xs = x_sample.reshape(n_s, d)
    stacked_p = stacked_s = None
    for l in range(depth):
        mod_p = mod[l, :bsz].reshape(bsz, 1, 6, d)
        mod_s = jnp.repeat(mod[l, bsz:bsz + dec_b].reshape(dec_b, 6, d), dec_t, axis=0)[None]
        mp = [mod_p[:, :, i] for i in range(6)]
        ms = [mod_s[:, :, i] for i in range(6)]
        vec = lambda a: a[l].reshape(1, -1)

        wqkv = w_in[l, :, :6 * WIDTH].astype(BF16)
        wf = jnp.pad(w_in[l, :, 6 * WIDTH:6 * WIDTH + N_HEADS], ((0, 0), (0, LANES - N_HEADS))).astype(BF16)
        wg = w_in[l, :, 6 * WIDTH + N_HEADS:].astype(BF16)
        bf = b_forget[l].reshape(1, N_HEADS)
        wsb, wfx, wo = w_sb_out[l].astype(BF16), w_fox_out[l].astype(BF16), w_out[l].astype(BF16)

        qkv_p, qvt_p, *stacked_p, sg_p, ksq_p = _in_projection(
            xp, vec(g_pre_mix), mp[0], mp[1], wqkv, wf, wg, bf, _row_tile(seq, 256), l, depth, stacked_p)
        lf_p = stacked_p[4][l]
        cum_p, key_cols_p = _forget_cumsum(jnp.swapaxes(lf_p.reshape(bsz, seq, N_HEADS), 1, 2), True)
        qkv_p3 = qkv_p.reshape(bsz, seq, 6 * WIDTH)
        osb_p = _sb_prompt(qkv_p3, qvt_p, tq)
        ofx_p = _fox_prompt(qkv_p3, qvt_p, cum_p, key_cols_p, jnp.sqrt(ksq_p.reshape(bsz, N_HEADS)),
                            _row_tile(seq, 512))
        xp = _mix_out(xp, osb_p.reshape(n_p, WIDTH), ofx_p.reshape(n_p, WIDTH), sg_p, mp[2], vec(g_post_mix),
                      wsb, wfx, wo, _row_tile(seq, 512))

        qkv_s, _, *stacked_s, sg_s, _ = _in_projection(
            xs, vec(g_pre_mix), ms[0], ms[1], wqkv, wf, wg, bf, n_s, l, depth, stacked_s)
        lf_s = stacked_s[4][l]
        t_all = n_past + dec_t
        t_pad = -(-t_all // CUMSUM_CHUNK) * CUMSUM_CHUNK
        lf_all = jnp.concatenate([cache_fox_logf[l].astype(F32), lf_s.reshape(dec_b, dec_t, N_HEADS),
                                  jnp.zeros((dec_b, t_pad - t_all, N_HEADS), F32)], axis=1)
        (cum_s,) = _forget_cumsum(jnp.swapaxes(lf_all, 1, 2), False)
        fk_s, fq_s = _pair_layouts(cum_s, n_past, dec_t)
        cache = lambda a: a[l].reshape(dec_b, n_past, WIDTH)
        osb_s, ofx_s = _sample_attention(qkv_s.reshape(dec_b, dec_t, 6 * WIDTH), cache(cache_sb_k), cache(cache_sb_v),
                                         cache(cache_fox_k), cache(cache_fox_v), fq_s, fk_s, past_chunk)
        xs = _mix_out(xs, osb_s.reshape(n_s, WIDTH), ofx_s.reshape(n_s, WIDTH), sg_s, ms[2], vec(g_post_mix),
                      wsb, wfx, wo, n_s)

        if l % 2 == 0:
            wgu, wd = w_ffn_gate_up[l // 2].astype(BF16), w_ffn_down[l // 2].astype(BF16)
            tf = _row_tile(wd.shape[0], 1408)
            xp = _ffn(xp, vec(g_pre_ffn), mp[3], mp[4], mp[5], vec(g_post_ffn), wgu, wd, _row_tile(seq, 512), tf)
            xs = _ffn(xs, vec(g_pre_ffn), ms[3], ms[4], ms[5], vec(g_post_ffn), wgu, wd, n_s, tf)
        else:
            wr = jnp.pad(w_router[l // 2], ((0, 0), (0, LANES - n_experts)))
            wgu, wd = w_moe_gate_up[l // 2].astype(BF16), w_moe_down[l // 2].astype(BF16)
            tf = _row_tile(wd.shape[1], 896)
            xp = _moe(xp, vec(g_pre_ffn), mp[3], mp[4], mp[5], vec(g_post_ffn), wr, wgu, wd, n_experts,
                      _row_tile(seq, 512), 1024, tf, _row_tile(seq, 256))
            xs = _moe(xs, vec(g_pre_ffn), ms[3], ms[4], ms[5], vec(g_post_ffn), wr, wgu, wd, n_experts,
                      n_s, 128, tf, n_s)

    split_p = [a.reshape(depth, bsz, seq, *a.shape[2:]) for a in stacked_p]
    split_s = [a.reshape(depth, dec_b, dec_t, *a.shape[2:]) for a in stacked_s]
    return (xp.reshape(bsz, seq, d), xs.reshape(dec_b, dec_t, d), *split_p, *split_s)
```

```python
import functools

import jax
import jax.numpy as jnp
from jax import lax
from jax.experimental import pallas as pl
from jax.experimental.pallas import tpu as pltpu

F32 = jnp.float32
BF16 = jnp.bfloat16

HEAD_DIM = 64
N_HEADS = 8
LANES = 128
HEAD_PAIRS = N_HEADS * HEAD_DIM // LANES
WIDTH = N_HEADS * HEAD_DIM
TOP_K = 2
RMS_EPS = 1e-6
LOG2_E = 1.4426950408889634
SB_DEAD_LOG = -104.0
VMEM_LIMIT = 56 * 1024 * 1024


def _params(sem, vmem=VMEM_LIMIT):
    return pltpu.CompilerParams(dimension_semantics=sem, vmem_limit_bytes=vmem)


def _dot(a, b):
    return jnp.dot(a, b, preferred_element_type=F32)


def _dot_nt(a, b):
    return lax.dot_general(a, b, (((1,), (1,)), ((), ())), preferred_element_type=F32)


def _sigmoid(x):
    return 1.0 / (1.0 + jnp.exp(-x))


def _softplus(x):
    return jnp.maximum(x, 0.0) + jnp.log(1.0 + jnp.exp(-jnp.abs(x)))


def _rms(x, g):
    return x * lax.rsqrt(jnp.mean(x * x, axis=-1, keepdims=True) + RMS_EPS) * g


def _mod_kernel(c_ref, w_ref, b_ref, o_ref):
    c = c_ref[...]
    s = c * _sigmoid(c)
    o_ref[0] = jnp.dot(s, w_ref[0], precision=lax.Precision.HIGHEST, preferred_element_type=F32) + b_ref[0]


def _modulation(c_all, w_mod, b_mod):
    depth, d, d6 = w_mod.shape
    rows = c_all.shape[0]
    tn = 1024
    return pl.pallas_call(
        _mod_kernel,
        grid=(depth, d6 // tn),
        in_specs=[pl.BlockSpec((rows, d), lambda l, j: (0, 0)),
                  pl.BlockSpec((1, d, tn), lambda l, j: (l, 0, j)),
                  pl.BlockSpec((1, 1, tn), lambda l, j: (l, 0, j))],
        out_specs=pl.BlockSpec((1, rows, tn), lambda l, j: (l, 0, j)),
        out_shape=jax.ShapeDtypeStruct((depth, rows, d6), F32),
        compiler_params=_params(("arbitrary", "arbitrary")),
        name="modulation",
    )(c_all, w_mod, b_mod.reshape(depth, 1, d6))


def _mod_spec(mod, tm, tiles_per_group):
    _, r, d = mod.shape
    return pl.BlockSpec((1, r, d), lambda i, *_: (i // tiles_per_group, 0, 0))


TRANSPOSED_SECTIONS = (0, 2, 3, 5)


def _inproj_kernel(x_ref, g_ref, shift_ref, scale_ref, wqkv_ref, wf_ref, wg_ref, bf_ref, *refs, tiles_per_group):
    qkv_ref, qvt_ref, ksb_ref, vsb_ref, kfx_ref, vfx_ref, lf_ref, sg_ref, ksq_ref = refs[-9:]
    tm = x_ref.shape[0]
    h = _rms(x_ref[...], g_ref[...]) * (1.0 + scale_ref[0]) + shift_ref[0]
    hb = h.astype(BF16)
    f32_outs = {1: ksb_ref, 2: vsb_ref, 4: kfx_ref, 5: vfx_ref}
    for c in range(6):
        cols = slice(c * WIDTH, (c + 1) * WIDTH)
        acc = _dot(hb, wqkv_ref[:, cols])
        if c in f32_outs:
            by_head = acc.reshape(tm, N_HEADS, HEAD_DIM)
            f32_outs[c][0] = by_head
            if c == 4:
                col = lax.broadcasted_iota(jnp.int32, (WIDTH, LANES), 0)
                out = lax.broadcasted_iota(jnp.int32, (WIDTH, LANES), 1)
                head_of = jnp.where(col // HEAD_DIM == out, 1.0, 0.0).astype(BF16)
                per_head = _dot((acc * acc).astype(BF16), head_of)
                ksq = jnp.max(per_head, axis=0, keepdims=True)[:, :N_HEADS]
                first_tile = pl.program_id(0) % tiles_per_group == 0

                @pl.when(first_tile)
                def _():
                    ksq_ref[0] = ksq

                @pl.when(jnp.logical_not(first_tile))
                def _():
                    ksq_ref[0] = jnp.maximum(ksq_ref[0], ksq)
        if c in (0, 3):
            acc = acc * (HEAD_DIM ** -0.5 * (LOG2_E if c == 3 else 1.0))
        qkv_ref[:, cols] = acc.astype(BF16)
        if c in TRANSPOSED_SECTIONS:
            t = TRANSPOSED_SECTIONS.index(c)
            qvt_ref[t * WIDTH:(t + 1) * WIDTH, :] = acc.T.astype(BF16)
    f = _dot(hb, wf_ref[...])[:, :N_HEADS] + bf_ref[...]
    lf_ref[0] = -_softplus(-f)
    for c in range(wg_ref.shape[1] // WIDTH):
        cols = slice(c * WIDTH, (c + 1) * WIDTH)
        sg_ref[:, cols] = _sigmoid(_dot(hb, wg_ref[:, cols])).astype(BF16)


def _in_projection(x, g_pre, shift, scale, wqkv, wf, wg, b_forget, tm, layer, depth, stacked):
    n, d = x.shape
    tiles_per_group = (n // tm) // shift.shape[0]
    const = lambda i: (0, 0)
    row = lambda i: (i, 0)
    heads_spec = pl.BlockSpec((1, tm, N_HEADS, HEAD_DIM), lambda i: (layer, i, 0, 0))
    heads_shape = jax.ShapeDtypeStruct((depth, n, N_HEADS, HEAD_DIM), F32)
    n_in = 8
    stacked = () if stacked is None else tuple(stacked)
    n_groups = shift.shape[0]
    return pl.pallas_call(
        functools.partial(_inproj_kernel, tiles_per_group=tiles_per_group),
        grid=(n // tm,),
        in_specs=[pl.BlockSpec((tm, d), row),
                  pl.BlockSpec((1, d), const),
                  _mod_spec(shift, tm, tiles_per_group),
                  _mod_spec(scale, tm, tiles_per_group),
                  pl.BlockSpec(wqkv.shape, const),
                  pl.BlockSpec(wf.shape, const),
                  pl.BlockSpec(wg.shape, const),
                  pl.BlockSpec((1, N_HEADS), const)] + [pl.BlockSpec(memory_space=pl.ANY)] * len(stacked),
        out_specs=[pl.BlockSpec((tm, 6 * WIDTH), row), pl.BlockSpec((len(TRANSPOSED_SECTIONS) * WIDTH, tm), lambda i: (0, i))]
                  + [heads_spec] * 4
                  + [pl.BlockSpec((1, tm, N_HEADS), lambda i: (layer, i, 0)), pl.BlockSpec((tm, wg.shape[1]), row),
                     pl.BlockSpec((1, 1, N_HEADS), lambda i: (i // tiles_per_group, 0, 0))],
        out_shape=[jax.ShapeDtypeStruct((n, 6 * WIDTH), BF16),
                   jax.ShapeDtypeStruct((len(TRANSPOSED_SECTIONS) * WIDTH, n), BF16)] + [heads_shape] * 4
                  + [jax.ShapeDtypeStruct((depth, n, N_HEADS), F32), jax.ShapeDtypeStruct((n, wg.shape[1]), BF16),
                     jax.ShapeDtypeStruct((n_groups, 1, N_HEADS), F32)],
        input_output_aliases={n_in + k: 2 + k for k in range(len(stacked))},
        compiler_params=_params(("arbitrary",)),
        name="in_projection",
    )(x, g_pre, shift, scale, wqkv, wf, wg, b_forget, *stacked)


CUMSUM_CHUNK = 256


FORGET_SPLIT = 3


def _cumsum_kernel(x_ref, o_ref, kcol_ref=None):
    t = x_ref.shape[2]
    r = lax.broadcasted_iota(jnp.int32, (CUMSUM_CHUNK, CUMSUM_CHUNK), 0)
    c = lax.broadcasted_iota(jnp.int32, (CUMSUM_CHUNK, CUMSUM_CHUNK), 1)
    upper = jnp.where(r <= c, 1.0, 0.0).astype(F32)
    row = lax.broadcasted_iota(jnp.int32, (LANES, LANES), 0)
    lane = lax.broadcasted_iota(jnp.int32, (LANES, LANES), 1)
    head, term = row % N_HEADS, row // N_HEADS
    placed = jnp.logical_and(row < FORGET_SPLIT * N_HEADS, lane == (head % 2) * FORGET_SPLIT + term)
    pad_rows = jnp.zeros((LANES - FORGET_SPLIT * N_HEADS, CUMSUM_CHUNK), F32)

    def step(i, carry):
        start = pl.multiple_of(i * CUMSUM_CHUNK, CUMSUM_CHUNK)
        seg = x_ref[0, :, pl.ds(start, CUMSUM_CHUNK)]
        cs = jnp.dot(seg, upper, precision=lax.Precision.HIGHEST, preferred_element_type=F32) + carry
        f_log2 = cs * LOG2_E
        o_ref[0, :, pl.ds(start, CUMSUM_CHUNK)] = f_log2
        if kcol_ref is None:
            return cs[:, CUMSUM_CHUNK - 1:CUMSUM_CHUNK]
        rest = -f_log2
        terms = []
        for _ in range(FORGET_SPLIT):
            part = pltpu.bitcast(pltpu.bitcast(rest, jnp.uint32) & jnp.uint32(0xFFFF0000), F32)
            terms.append(part)
            rest = rest - part
        terms_t = jnp.concatenate(terms + [pad_rows], axis=0).T.astype(BF16)
        for p in range(HEAD_PAIRS):
            select = jnp.where(jnp.logical_and(placed, head // 2 == p), 1.0, 0.0).astype(BF16)
            kcol_ref[0, p, pl.ds(start, CUMSUM_CHUNK), :] = _dot(terms_t, select).astype(BF16)
        return cs[:, CUMSUM_CHUNK - 1:CUMSUM_CHUNK]

    lax.fori_loop(0, t // CUMSUM_CHUNK, step, jnp.zeros((x_ref.shape[1], 1), F32))


def _forget_cumsum(x, key_columns):
    b, r, t = x.shape
    out_specs = [pl.BlockSpec((1, r, t), lambda i: (i, 0, 0))]
    out_shape = [jax.ShapeDtypeStruct((b, r, t), F32)]
    if key_columns:
        out_specs.append(pl.BlockSpec((1, HEAD_PAIRS, t, LANES), lambda i: (i, 0, 0, 0)))
        out_shape.append(jax.ShapeDtypeStruct((b, HEAD_PAIRS, t, LANES), BF16))
    return pl.pallas_call(
        _cumsum_kernel,
        grid=(b,),
        in_specs=[pl.BlockSpec((1, r, t), lambda i: (i, 0, 0))],
        out_specs=out_specs,
        out_shape=out_shape,
        compiler_params=_params(("arbitrary",)),
        name="forget_cumsum",
    )(x)


def _pair_layouts(cum_t, t_query_start, t_query):
    b, _, tp = cum_t.shape
    f_keys = cum_t.reshape(b, HEAD_PAIRS, 2, tp)
    f_query = jnp.swapaxes(f_keys[:, :, :, t_query_start:t_query_start + t_query], 2, 3)
    return f_keys, f_query


def _head_masks():
    lane = lax.broadcasted_iota(jnp.int32, (1, LANES), 1)
    return lane < HEAD_DIM


def _split_heads(q2, first_head):
    zero = jnp.zeros_like(q2)
    return jnp.where(first_head, q2, zero), jnp.where(first_head, zero, q2)


def _strict_lower_neg(n):
    r = lax.broadcasted_iota(jnp.int32, (n, n), 0)
    c = lax.broadcasted_iota(jnp.int32, (n, n), 1)
    return jnp.where(r > c, -1.0, 0.0).astype(BF16)


def _sb_block(qh, k2, v2, neg_tri, visible, acc_ref, r_ref, hd):
    z = _dot_nt(qh, k2)
    sp = _softplus(z)
    if visible is not None:
        sp = jnp.where(visible, sp, 0.0)
    later = _dot(sp.astype(BF16), neg_tri)
    p = jnp.exp(z - sp + later)
    if visible is not None:
        p = jnp.where(visible, p, 0.0)
    r = r_ref[hd]
    acc_ref[hd] += jnp.exp(r) * _dot(p.astype(BF16), v2)
    r_ref[hd] = r + later[:, 0:1] - sp[:, 0:1]


def _fox_block(qh, k2, v2_ones, fq, fk, allowed, acc_ref, m_ref, hd):
    s = _dot_nt(qh, k2) + fq - fk
    if allowed is not None:
        s = jnp.where(allowed, s, -jnp.inf)
    m_old = m_ref[hd]
    m_new = jnp.maximum(m_old, jnp.max(s, axis=-1, keepdims=True))
    p = jnp.exp2(s - m_new)
    acc_ref[hd] = jnp.exp2(m_old - m_new) * acc_ref[hd] + _dot(p.astype(BF16), v2_ones)
    m_ref[hd] = m_new


def _fox_finish(acc_ref, first_head):
    a, b = acc_ref[0], acc_ref[1]
    num = jnp.where(first_head, a, b)
    den = jnp.where(first_head, pltpu.roll(a, HEAD_DIM, 1), pltpu.roll(b, HEAD_DIM, 1))
    return num / den


def _with_ones(v2, first_head):
    one = jnp.ones_like(v2)
    return jnp.where(first_head, v2, one), jnp.where(first_head, one, v2)


def _sb_prompt_kernel(qt_ref, k_ref, vt_ref, o_ref, w_ref, acc_ref, r_ref, tri_ref, *, tq):
    i = pl.program_id(2)
    n_pairs = qt_ref.shape[0] // LANES
    sub = lax.broadcasted_iota(jnp.int32, (LANES, tq), 0)
    first_rows = sub < HEAD_DIM
    for pp in range(n_pairs):
        qt = qt_ref[pp * LANES:(pp + 1) * LANES, :]
        zero = jnp.zeros_like(qt)
        w_ref[2 * pp] = jnp.where(first_rows, qt, zero)
        w_ref[2 * pp + 1] = jnp.where(first_rows, zero, qt)
    acc_ref[...] = jnp.zeros_like(acc_ref)
    r_ref[...] = jnp.zeros_like(r_ref)

    @pl.when(jnp.logical_and(jnp.logical_and(pl.program_id(0) == 0, pl.program_id(1) == 0), i == 0))
    def _():
        s_idx = lax.broadcasted_iota(jnp.int32, tri_ref.shape, 0)
        j_idx = lax.broadcasted_iota(jnp.int32, tri_ref.shape, 1)
        tri_ref[...] = jnp.where(j_idx > s_idx, -1.0, 0.0).astype(BF16)

    def block(start, size, vis):
        neg_tri = tri_ref[:size, :size]
        for hd in range(2 * n_pairs):
            lanes = slice((hd // 2) * LANES, (hd // 2 + 1) * LANES)
            keys = k_ref[0, pl.ds(start, size), lanes]
            vt = vt_ref[lanes, pl.ds(start, size)]
            z = _dot(keys, w_ref[hd])
            sp = _softplus(z)
            if vis is not None:
                sp = jnp.where(vis, sp, 0.0)
            later = _dot(neg_tri, sp.astype(BF16))
            p = jnp.exp(z - sp + later)
            if vis is not None:
                p = jnp.where(vis, p, 0.0)
            r = r_ref[hd]
            acc_ref[hd] += jnp.exp(r) * _dot(vt, p.astype(BF16))
            r_ref[hd] = r + later[0:1, :] - sp[0:1, :]

    def live():
        worst = r_ref[0]
        for hd in range(1, 2 * n_pairs):
            worst = jnp.maximum(worst, r_ref[hd])
        return jnp.max(worst)

    first = jnp.maximum(i - 1, 0)
    ahead = lax.broadcasted_iota(jnp.int32, (2 * tq, tq), 0) - lax.broadcasted_iota(jnp.int32, (2 * tq, tq), 1)
    block(pl.multiple_of(first * tq, tq), 2 * tq, ahead < (i - first) * tq)

    def cond(carry):
        j, r_max = carry
        return jnp.logical_and(j >= 0, r_max > SB_DEAD_LOG)

    def body(carry):
        j, _ = carry
        block(pl.multiple_of(j * tq, tq), tq, None)
        return j - 1, live()

    lax.while_loop(cond, body, (i - 2, live()))
    for pp in range(n_pairs):
        out_t = jnp.where(first_rows, acc_ref[2 * pp], acc_ref[2 * pp + 1])
        o_ref[0, :, pp * LANES:(pp + 1) * LANES] = out_t.T.astype(o_ref.dtype)


SB_PAIRS_PER_STEP = 2


def _sb_prompt(qkv, qv_t, tq):
    b, t, _ = qkv.shape
    nq = t // tq
    group = SB_PAIRS_PER_STEP * LANES
    n_groups = WIDTH // group
    heads = 2 * SB_PAIRS_PER_STEP
    kernel = functools.partial(_sb_prompt_kernel, tq=tq)
    return pl.pallas_call(
        kernel,
        grid=(b, n_groups, nq),
        in_specs=[pl.BlockSpec((group, tq), lambda b_, p, i: (p, b_ * nq + i)),
                  pl.BlockSpec((1, t, group), lambda b_, p, i: (b_, 0, n_groups + p)),
                  pl.BlockSpec((group, t), lambda b_, p, i: (n_groups + p, b_))],
        out_specs=pl.BlockSpec((1, tq, group), lambda b_, p, i: (b_, i, p)),
        out_shape=jax.ShapeDtypeStruct((b, t, WIDTH), BF16),
        scratch_shapes=[pltpu.VMEM((heads, LANES, tq), BF16), pltpu.VMEM((heads, LANES, tq), F32),
                        pltpu.VMEM((heads, 1, tq), F32), pltpu.VMEM((2 * tq, 2 * tq), BF16)],
        compiler_params=_params(("arbitrary", "arbitrary", "arbitrary")),
        name="sb_attention_prompt",
    )(qv_t, qkv, qv_t)


FOX_DEAD_LOG2 = -150.0
FOX_BOUNDED_MAX = 40.0


def _fox_prompt_kernel(kmax_ref, fend_ref, qt_ref, k_ref, kf_ref, vt_ref, fq_ref, o_ref,
                       w_ref, acc_ref, m_ref, zb_ref, *, tq):
    bi = pl.program_id(0)
    pi = pl.program_id(1)
    i = pl.program_id(2)
    sub = lax.broadcasted_iota(jnp.int32, (LANES, tq), 0)
    first_rows = sub < HEAD_DIM
    qt = qt_ref[...]
    zero = jnp.zeros_like(qt)
    for hd in range(2):
        lo = hd * FORGET_SPLIT
        select = jnp.where(jnp.logical_and(sub >= lo, sub < lo + FORGET_SPLIT), 1.0, 0.0).astype(BF16)
        q_head = jnp.where(first_rows, qt, zero) if hd == 0 else jnp.where(first_rows, zero, qt)
        w_ref[hd, :LANES, :] = q_head
        w_ref[hd, LANES:, :] = select
        q_f32 = q_head.astype(F32)
        q_norm = jnp.sqrt(jnp.sum(q_f32 * q_f32, axis=0, keepdims=True))
        zb_ref[hd] = q_norm * (kmax_ref[bi, 2 * pi + hd] * 1.01)
    acc_ref[...] = jnp.zeros_like(acc_ref)
    m_ref[...] = jnp.full_like(m_ref, -jnp.inf)
    key_idx = lax.broadcasted_iota(jnp.int32, (tq, tq), 0)
    query_idx = lax.broadcasted_iota(jnp.int32, (tq, tq), 1)
    allowed = key_idx <= query_idx
    den_row = (HEAD_DIM, 0)

    def operands(j):
        start = pl.multiple_of(j * tq, tq)
        keys = jnp.concatenate([k_ref[0, pl.ds(start, tq), :], kf_ref[0, 0, pl.ds(start, tq), :]], axis=1)
        vt = vt_ref[:, pl.ds(start, tq)]
        one = jnp.ones_like(vt)
        first_v = lax.broadcasted_iota(jnp.int32, vt.shape, 0) < HEAD_DIM
        return keys, (jnp.where(first_v, vt, one), jnp.where(first_v, one, vt))

    def block_bounded(j, mask):
        keys, vts = operands(j)
        for hd in range(2):
            u = _dot(keys, w_ref[hd])
            if mask is not None:
                u = jnp.where(mask, u, -jnp.inf)
            p = jnp.exp2(u + (fq_ref[0, 0, hd:hd + 1, :] - zb_ref[hd]))
            acc_ref[hd] += _dot(vts[hd], p.astype(BF16))

    def alive_bounded(j):
        jc = jnp.maximum(j, 0)
        reach = []
        for hd in range(2):
            den = acc_ref[hd, den_row[hd]:den_row[hd] + 1, :]
            gap = jnp.max(fq_ref[0, 0, hd:hd + 1, :] - jnp.log2(den))
            reach.append(gap - fend_ref[bi, pi, hd, jc])
        return jnp.maximum(reach[0], reach[1]) + tq.bit_length() > FOX_DEAD_LOG2

    def block_exact(j, mask):
        keys, vts = operands(j)
        for hd in range(2):
            u = _dot(keys, w_ref[hd])
            if mask is not None:
                u = jnp.where(mask, u, -jnp.inf)
            fq = fq_ref[0, 0, hd:hd + 1, :]
            m_old = m_ref[hd]
            m_new = jnp.maximum(m_old, jnp.max(u, axis=0, keepdims=True) + fq)
            p = jnp.exp2(u + (fq - m_new))
            acc_ref[hd] = jnp.exp2(m_old - m_new) * acc_ref[hd] + _dot(vts[hd], p.astype(BF16))
            m_ref[hd] = m_new

    def alive_exact(j):
        jc = jnp.maximum(j, 0)
        reach = [jnp.max(zb_ref[hd] + fq_ref[0, 0, hd:hd + 1, :] - m_ref[hd]) - fend_ref[bi, pi, hd, jc]
                 for hd in range(2)]
        return jnp.maximum(reach[0], reach[1]) > FOX_DEAD_LOG2

    def sweep(block, alive):
        block(i, allowed)

        def cond(carry):
            j, go = carry
            return jnp.logical_and(j >= 0, go)

        def body(carry):
            j, _ = carry
            block(j, None)
            return j - 1, alive(j - 1)

        lax.while_loop(cond, body, (i - 1, alive(i - 1)))

    bounded = jnp.max(jnp.maximum(zb_ref[0], zb_ref[1])) <= FOX_BOUNDED_MAX

    @pl.when(bounded)
    def _():
        sweep(block_bounded, alive_bounded)

    @pl.when(jnp.logical_not(bounded))
    def _():
        sweep(block_exact, alive_exact)

    a, b = acc_ref[0], acc_ref[1]
    out_t = jnp.where(first_rows, a / a[HEAD_DIM:HEAD_DIM + 1, :], b / b[0:1, :])
    o_ref[0] = out_t.T.astype(o_ref.dtype)


def _fox_prompt(qkv, qv_t, cum_t, key_cols, k_max, tq):
    b, t, _ = qkv.shape
    nq = t // tq
    f_rows = cum_t.reshape(b, HEAD_PAIRS, 2, t)
    f_block_end = f_rows[:, :, :, tq - 1::tq]
    smem = pl.BlockSpec(memory_space=pltpu.SMEM)
    kernel = functools.partial(_fox_prompt_kernel, tq=tq)
    return pl.pallas_call(
        kernel,
        grid=(b, HEAD_PAIRS, nq),
        in_specs=[smem, smem,
                  pl.BlockSpec((LANES, tq), lambda b_, p, i: (2 * HEAD_PAIRS + p, b_ * nq + i)),
                  pl.BlockSpec((1, t, LANES), lambda b_, p, i: (b_, 0, 4 * HEAD_PAIRS + p)),
                  pl.BlockSpec((1, 1, t, LANES), lambda b_, p, i: (b_, p, 0, 0)),
                  pl.BlockSpec((LANES, t), lambda b_, p, i: (3 * HEAD_PAIRS + p, b_)),
                  pl.BlockSpec((1, 1, 2, tq), lambda b_, p, i: (b_, p, 0, i))],
        out_specs=pl.BlockSpec((1, tq, LANES), lambda b_, p, i: (b_, i, p)),
        out_shape=jax.ShapeDtypeStruct((b, t, WIDTH), BF16),
        scratch_shapes=[pltpu.VMEM((2, 2 * LANES, tq), BF16), pltpu.VMEM((2, LANES, tq), F32),
                        pltpu.VMEM((2, 1, tq), F32), pltpu.VMEM((2, 1, tq), F32)],
        compiler_params=_params(("arbitrary", "arbitrary", "arbitrary")),
        name="fox_attention_prompt",
    )(k_max, f_block_end, qv_t, qkv, key_cols, qv_t, f_rows)


def _sample_attn_kernel(qsb_ref, ksb_ref, vsb_ref, qfx_ref, kfx_ref, vfx_ref,
                        csk_ref, csv_ref, cfk_ref, cfv_ref, fq_ref, fk_ref,
                        osb_ref, ofx_ref, acc_ref, st_ref, *, tk):
    t_new = qsb_ref.shape[1]
    n_past = csk_ref.shape[1]
    n_blocks = n_past // tk
    first_head = _head_masks()
    row = lax.broadcasted_iota(jnp.int32, (t_new, t_new), 0)
    col = lax.broadcasted_iota(jnp.int32, (t_new, t_new), 1)

    qs = _split_heads(qsb_ref[0], first_head)
    acc_ref[...] = jnp.zeros_like(acc_ref)
    st_ref[...] = jnp.zeros_like(st_ref)
    tri_new = _strict_lower_neg(t_new)
    tri_past = _strict_lower_neg(tk)
    for hd in range(2):
        _sb_block(qs[hd], ksb_ref[0], vsb_ref[0], tri_new, col < row, acc_ref, st_ref, hd)

    def live():
        return jnp.max(jnp.maximum(st_ref[0], st_ref[1]))

    def sb_cond(carry):
        j, r_max = carry
        return jnp.logical_and(j >= 0, r_max > SB_DEAD_LOG)

    def sb_body(carry):
        j, _ = carry
        start = pl.multiple_of(j * tk, tk)
        k2 = csk_ref[0, pl.ds(start, tk), :].astype(BF16)
        v2 = csv_ref[0, pl.ds(start, tk), :].astype(BF16)
        for hd in range(2):
            _sb_block(qs[hd], k2, v2, tri_past, None, acc_ref, st_ref, hd)
        return j - 1, live()

    lax.while_loop(sb_cond, sb_body, (n_blocks - 1, live()))
    osb_ref[0] = jnp.where(first_head, acc_ref[0], acc_ref[1]).astype(osb_ref.dtype)

    qs = _split_heads(qfx_ref[0], first_head)
    acc_ref[...] = jnp.zeros_like(acc_ref)
    st_ref[...] = jnp.full_like(st_ref, -jnp.inf)
    fq = fq_ref[0, 0]
    vs = _with_ones(vfx_ref[0], first_head)
    for hd in range(2):
        fk = fk_ref[0, 0, hd:hd + 1, n_past:n_past + t_new]
        _fox_block(qs[hd], kfx_ref[0], vs[hd], fq[:, hd:hd + 1], fk, col <= row, acc_ref, st_ref, hd)

    k2 = cfk_ref[0].astype(BF16)
    vs_ = _with_ones(cfv_ref[0].astype(BF16), first_head)
    for hd in range(2):
        fk = fk_ref[0, 0, hd:hd + 1, 0:n_past]
        _fox_block(qs[hd], k2, vs_[hd], fq[:, hd:hd + 1], fk, None, acc_ref, st_ref, hd)
    ofx_ref[0] = _fox_finish(acc_ref, first_head).astype(ofx_ref.dtype)


def _sample_attention(qkv, cache_sb_k, cache_sb_v, cache_fox_k, cache_fox_v, f_query, f_keys, tk):
    b, t_new, _ = qkv.shape
    n_past = cache_sb_k.shape[1]
    tp = f_keys.shape[3]
    new = lambda c: pl.BlockSpec((1, t_new, LANES), lambda b_, p: (b_, 0, c * HEAD_PAIRS + p))
    past = pl.BlockSpec((1, n_past, LANES), lambda b_, p: (b_, 0, p))
    out = pl.BlockSpec((1, t_new, LANES), lambda b_, p: (b_, 0, p))
    kernel = functools.partial(_sample_attn_kernel, tk=tk)
    return pl.pallas_call(
        kernel,
        grid=(b, HEAD_PAIRS),
        in_specs=[new(0), new(1), new(2), new(3), new(4), new(5), past, past, past, past,
                  pl.BlockSpec((1, 1, t_new, 2), lambda b_, p: (b_, p, 0, 0)),
                  pl.BlockSpec((1, 1, 2, tp), lambda b_, p: (b_, p, 0, 0))],
        out_specs=[out, out],
        out_shape=[jax.ShapeDtypeStruct((b, t_new, WIDTH), BF16)] * 2,
        scratch_shapes=[pltpu.VMEM((2, t_new, LANES), F32), pltpu.VMEM((2, t_new, 1), F32)],
        compiler_params=_params(("arbitrary", "arbitrary")),
        name="attention_sample",
    )(qkv, qkv, qkv, qkv, qkv, qkv, cache_sb_k, cache_sb_v, cache_fox_k, cache_fox_v, f_query, f_keys)


def _mix_out_kernel(x_ref, osb_ref, ofx_ref, sg_ref, gate_ref, g_ref, wsb_ref, wfx_ref, wo_ref, o_ref):
    d = x_ref.shape[1]
    merged = (sg_ref[:, :d].astype(F32) * _dot(osb_ref[...], wsb_ref[...])
              + sg_ref[:, d:].astype(F32) * _dot(ofx_ref[...], wfx_ref[...]))
    y = _dot(merged.astype(BF16), wo_ref[...])
    o_ref[...] = x_ref[...] + gate_ref[0] * _rms(y, g_ref[...])


def _mix_out(x, o_sb, o_fx, sg, gate, g_post, w_sb_out, w_fox_out, w_out, tm):
    n, d = x.shape
    tiles_per_group = (n // tm) // gate.shape[0]
    const = lambda i: (0, 0)
    row = lambda i: (i, 0)
    return pl.pallas_call(
        _mix_out_kernel,
        grid=(n // tm,),
        in_specs=[pl.BlockSpec((tm, d), row), pl.BlockSpec((tm, WIDTH), row), pl.BlockSpec((tm, WIDTH), row),
                  pl.BlockSpec((tm, 2 * d), row), _mod_spec(gate, tm, tiles_per_group),
                  pl.BlockSpec((1, d), const), pl.BlockSpec(w_sb_out.shape, const),
                  pl.BlockSpec(w_fox_out.shape, const), pl.BlockSpec(w_out.shape, const)],
        out_specs=pl.BlockSpec((tm, d), row),
        out_shape=jax.ShapeDtypeStruct((n, d), F32),
        compiler_params=_params(("arbitrary",)),
        name="mix_out",
    )(x, o_sb, o_fx, sg, gate, g_post, w_sb_out, w_fox_out, w_out)


def _swiglu_chunk(hb, wa, wb, wd):
    a = _dot(hb, wa)
    b = _dot(hb, wb)
    return a * _sigmoid(a) * b, wd


def _ffn_kernel(x_ref, gpre_ref, shift_ref, scale_ref, gate_ref, gpost_ref, wa_ref, wb_ref, wd_ref,
                o_ref, h_ref, acc_ref):
    f = pl.program_id(1)

    @pl.when(f == 0)
    def _():
        h = _rms(x_ref[...], gpre_ref[...]) * (1.0 + scale_ref[0]) + shift_ref[0]
        h_ref[...] = h.astype(BF16)
        acc_ref[...] = jnp.zeros_like(acc_ref)

    hb = h_ref[...]
    a = _dot(hb, wa_ref[...])
    b = _dot(hb, wb_ref[...])
    act = a * _sigmoid(a) * b
    acc_ref[...] += _dot(act.astype(BF16), wd_ref[...])

    @pl.when(f == pl.num_programs(1) - 1)
    def _():
        o_ref[...] = x_ref[...] + gate_ref[0] * _rms(acc_ref[...], gpost_ref[...])


def _ffn(x, g_pre, shift, scale, gate, g_post, w_gate_up, w_down, tm, tf):
    n, d = x.shape
    d_ff = w_down.shape[0]
    nf = d_ff // tf
    tiles_per_group = (n // tm) // shift.shape[0]
    const = lambda i, f: (0, 0)
    row = lambda i, f: (i, 0)
    return pl.pallas_call(
        _ffn_kernel,
        grid=(n // tm, nf),
        in_specs=[pl.BlockSpec((tm, d), row), pl.BlockSpec((1, d), const),
                  _mod_spec(shift, tm, tiles_per_group), _mod_spec(scale, tm, tiles_per_group),
                  _mod_spec(gate, tm, tiles_per_group), pl.BlockSpec((1, d), const),
                  pl.BlockSpec((d, tf), lambda i, f: (0, f)),
                  pl.BlockSpec((d, tf), lambda i, f: (0, nf + f)),
                  pl.BlockSpec((tf, d), lambda i, f: (f, 0))],
        out_specs=pl.BlockSpec((tm, d), row),
        out_shape=jax.ShapeDtypeStruct((n, d), F32),
        scratch_shapes=[pltpu.VMEM((tm, d), BF16), pltpu.VMEM((tm, d), F32)],
        compiler_params=_params(("arbitrary", "arbitrary")),
        name="ffn_dense",
    )(x, g_pre, shift, scale, gate, g_post, w_gate_up, w_gate_up, w_down)


def _route_kernel(x_ref, gpre_ref, shift_ref, scale_ref, wr_ref, h_ref, idx_ref, w_ref, rank_ref, count_ref,
                  *, n_experts):
    tm = x_ref.shape[0]

    @pl.when(pl.program_id(0) == 0)
    def _():
        count_ref[...] = jnp.zeros_like(count_ref)

    h = _rms(x_ref[...], gpre_ref[...]) * (1.0 + scale_ref[0]) + shift_ref[0]
    h_ref[...] = h
    logits = jnp.dot(h, wr_ref[...], precision=lax.Precision.HIGHEST, preferred_element_type=F32)[:, :n_experts]
    idx = lax.broadcasted_iota(jnp.int32, logits.shape, 1)
    m1 = jnp.max(logits, axis=-1, keepdims=True)
    i1 = jnp.min(jnp.where(logits == m1, idx, n_experts), axis=-1, keepdims=True)
    rest = jnp.where(idx == i1, -jnp.inf, logits)
    m2 = jnp.max(rest, axis=-1, keepdims=True)
    i2 = jnp.min(jnp.where(rest == m2, idx, n_experts), axis=-1, keepdims=True)
    e2 = jnp.exp(m2 - m1)
    slot = lax.broadcasted_iota(jnp.int32, idx_ref.shape, 1)
    idx_ref[...] = jnp.where(slot == 0, i1, i2)
    w_ref[...] = jnp.where(slot == 0, 1.0 / (1.0 + e2), e2 / (1.0 + e2))
    lane = lax.broadcasted_iota(jnp.int32, (tm, LANES), 1)
    first, second = lane == i1, lane == i2
    chosen = jnp.where(jnp.logical_or(first, second), 1.0, 0.0)
    t_row = lax.broadcasted_iota(jnp.int32, (tm, tm), 0)
    t_col = lax.broadcasted_iota(jnp.int32, (tm, tm), 1)
    earlier = jnp.where(t_col < t_row, 1.0, 0.0).astype(BF16)
    before = count_ref[0:1, :] + _dot(earlier, chosen.astype(BF16))
    rank1 = jnp.sum(jnp.where(first, before, 0.0), axis=-1, keepdims=True)
    rank2 = jnp.sum(jnp.where(second, before, 0.0), axis=-1, keepdims=True)
    rank_ref[...] = jnp.where(slot == 0, rank1, rank2).astype(jnp.int32)
    count_ref[...] = count_ref[...] + jnp.sum(chosen, axis=0, keepdims=True)


def _route(x, g_pre, shift, scale, w_router, n_experts, tm):
    n, d = x.shape
    tiles_per_group = (n // tm) // shift.shape[0]
    const = lambda i: (0, 0)
    row = lambda i: (i, 0)
    return pl.pallas_call(
        functools.partial(_route_kernel, n_experts=n_experts),
        grid=(n // tm,),
        in_specs=[pl.BlockSpec((tm, d), row), pl.BlockSpec((1, d), const),
                  _mod_spec(shift, tm, tiles_per_group), _mod_spec(scale, tm, tiles_per_group),
                  pl.BlockSpec(w_router.shape, const)],
        out_specs=[pl.BlockSpec((tm, d), row), pl.BlockSpec((tm, TOP_K), row), pl.BlockSpec((tm, TOP_K), row),
                   pl.BlockSpec((tm, TOP_K), row), pl.BlockSpec((8, LANES), const)],
        out_shape=[jax.ShapeDtypeStruct((n, d), F32), jax.ShapeDtypeStruct((n, TOP_K), jnp.int32),
                   jax.ShapeDtypeStruct((n, TOP_K), F32), jax.ShapeDtypeStruct((n, TOP_K), jnp.int32),
                   jax.ShapeDtypeStruct((8, LANES), F32)],
        compiler_params=_params(("arbitrary",)),
        name="moe_route",
    )(x, g_pre, shift, scale, w_router)


def _dispatch_tables(top_idx, rank, counts, n_experts, tm):
    n = top_idx.shape[0]
    n_assign = n * TOP_K
    expert = top_idx.reshape(n_assign)
    padded = (counts + tm - 1) // tm * tm
    group_end = jnp.cumsum(padded)
    group_start = group_end - padded
    onehot = expert[:, None] == jnp.arange(n_experts, dtype=jnp.int32)[None, :]
    pos = jnp.sum(jnp.where(onehot, group_start[None, :], 0), axis=1) + rank.reshape(n_assign)
    p_rows = (-(-n_assign // tm) + n_experts) * tm
    src_token = jnp.zeros((p_rows,), jnp.int32).at[pos].set(jnp.arange(n_assign, dtype=jnp.int32) // TOP_K)
    tile_start = jnp.arange(p_rows // tm, dtype=jnp.int32) * tm
    tile_expert = jnp.minimum(jnp.sum(tile_start[:, None] >= group_end[None, :], axis=1), n_experts - 1)
    n_used = (group_end[-1] // tm).reshape(1)
    return src_token, tile_expert.astype(jnp.int32), n_used.astype(jnp.int32), pos.astype(jnp.int32)


def _gather_rows(src_hbm, dst, sem, index_of_row, n_rows, unroll=8):
    def issue(r, carry):
        pltpu.make_async_copy(src_hbm.at[pl.ds(index_of_row(r), 1), :], dst.at[pl.ds(r, 1), :], sem).start()
        return carry

    lax.fori_loop(0, n_rows, issue, 0, unroll=unroll)


def _wait_rows(src_hbm, dst, sem):
    pltpu.make_async_copy(src_hbm.at[pl.ds(0, dst.shape[0]), :], dst, sem).wait()


def _experts_kernel(src_ref, expert_ref, used_ref, h_hbm, wa_ref, wb_ref, wd_ref, o_ref,
                    rows_ref, xb_ref, acc_ref, sem, *, tm):
    i = pl.program_id(0)
    f = pl.program_id(1)
    nf = pl.num_programs(1)
    n_used = used_ref[0]
    slot = lax.rem(i, 2)
    active = i < n_used
    rows_per_step = tm // nf

    @pl.when(jnp.logical_and(f == 0, i == 0))
    def _():
        _gather_rows(h_hbm, rows_ref.at[0], sem.at[0], lambda r: src_ref[r], tm)

    @pl.when(jnp.logical_and(f == 0, i <= n_used))
    def _():
        _wait_rows(h_hbm, rows_ref.at[slot], sem.at[slot])

    @pl.when(jnp.logical_and(f == 0, active))
    def _():
        xb_ref[...] = rows_ref[slot].astype(BF16)
        acc_ref[...] = jnp.zeros_like(acc_ref)

    @pl.when(active)
    def _():
        base = (i + 1) * tm + f * rows_per_step
        nxt = rows_ref.at[1 - slot]
        for r in range(rows_per_step):
            row = f * rows_per_step + r
            pltpu.make_async_copy(h_hbm.at[pl.ds(src_ref[base + r], 1), :], nxt.at[pl.ds(row, 1), :],
                                  sem.at[1 - slot]).start()
        xb = xb_ref[...]
        a = _dot(xb, wa_ref[0])
        b = _dot(xb, wb_ref[0])
        act = a * _sigmoid(a) * b
        acc_ref[...] += _dot(act.astype(BF16), wd_ref[0])

    last = f == pl.num_programs(1) - 1

    @pl.when(jnp.logical_and(last, active))
    def _():
        o_ref[...] = acc_ref[...]

    @pl.when(jnp.logical_and(last, jnp.logical_not(active)))
    def _():
        o_ref[...] = jnp.zeros_like(o_ref)


def _experts(h, src_token, tile_expert, n_used, w_gate_up, w_down, tm, tf):
    p_rows = src_token.shape[0]
    d = h.shape[1]
    nf = w_down.shape[1] // tf
    ff = lambda i, f, used: jnp.where(i < used[0], f, nf - 1)
    grid_spec = pltpu.PrefetchScalarGridSpec(
        num_scalar_prefetch=3,
        grid=(p_rows // tm, nf),
        in_specs=[pl.BlockSpec(memory_space=pl.ANY),
                  pl.BlockSpec((1, d, tf), lambda i, f, src, ex, used: (ex[i], 0, ff(i, f, used))),
                  pl.BlockSpec((1, d, tf), lambda i, f, src, ex, used: (ex[i], 0, nf + ff(i, f, used))),
                  pl.BlockSpec((1, tf, d), lambda i, f, src, ex, used: (ex[i], ff(i, f, used), 0))],
        out_specs=pl.BlockSpec((tm, d), lambda i, f, src, ex, used: (i, 0)),
        scratch_shapes=[pltpu.VMEM((2, tm, d), F32), pltpu.VMEM((tm, d), BF16), pltpu.VMEM((tm, d), F32),
                        pltpu.SemaphoreType.DMA((2,))])
    return pl.pallas_call(
        functools.partial(_experts_kernel, tm=tm),
        grid_spec=grid_spec,
        out_shape=jax.ShapeDtypeStruct((p_rows, d), F32),
        compiler_params=_params(("arbitrary", "arbitrary")),
        name="moe_experts",
    )(src_token, tile_expert, n_used, h, w_gate_up, w_gate_up, w_down)


def _combine_kernel(pos_ref, y_hbm, x_ref, w_ref, gate_ref, gpost_ref, o_ref, rows_ref, sem, *, tm):
    i = pl.program_id(0)
    slot = lax.rem(i, 2)

    def gather(tile, into):
        for k in range(TOP_K):
            dst = rows_ref.at[into, k]
            for r in range(tm):
                src_row = pos_ref[(tile * tm + r) * TOP_K + k]
                pltpu.make_async_copy(y_hbm.at[pl.ds(src_row, 1), :], dst.at[pl.ds(r, 1), :],
                                      sem.at[into]).start(priority=r % 2)

    @pl.when(i == 0)
    def _():
        gather(0, 0)

    for k in range(TOP_K):
        _wait_rows(y_hbm, rows_ref.at[slot, k], sem.at[slot])

    @pl.when(i + 1 < pl.num_programs(0))
    def _():
        gather(i + 1, 1 - slot)

    w = w_ref[...]
    mixed = w[:, 0:1] * rows_ref[slot, 0] + w[:, 1:2] * rows_ref[slot, 1]
    o_ref[...] = x_ref[...] + gate_ref[0] * _rms(mixed, gpost_ref[...])


def _combine(y, pos, top_w, x, gate, g_post, tm):
    n, d = x.shape
    tiles_per_group = (n // tm) // gate.shape[0]
    grid_spec = pltpu.PrefetchScalarGridSpec(
        num_scalar_prefetch=1,
        grid=(n // tm,),
        in_specs=[pl.BlockSpec(memory_space=pl.ANY),
                  pl.BlockSpec((tm, d), lambda i, pos_: (i, 0)),
                  pl.BlockSpec((tm, TOP_K), lambda i, pos_: (i, 0)),
                  _mod_spec(gate, tm, tiles_per_group),
                  pl.BlockSpec((1, d), lambda i, pos_: (0, 0))],
        out_specs=pl.BlockSpec((tm, d), lambda i, pos_: (i, 0)),
        scratch_shapes=[pltpu.VMEM((2, TOP_K, tm, d), F32), pltpu.SemaphoreType.DMA((2,))])
    return pl.pallas_call(
        functools.partial(_combine_kernel, tm=tm),
        grid_spec=grid_spec,
        out_shape=jax.ShapeDtypeStruct((n, d), F32),
        compiler_params=_params(("arbitrary",)),
        name="moe_combine",
    )(pos, y, x, top_w, gate, g_post)


def _moe(x, g_pre, shift, scale, gate, g_post, w_router, w_gate_up, w_down, n_experts, tm_route, tm, tf, tm_out):
    h, top_idx, top_w, rank, counts = _route(x, g_pre, shift, scale, w_router, n_experts, tm_route)
    counts = counts[0, :n_experts].astype(jnp.int32)
    src_token, tile_expert, n_used, pos = _dispatch_tables(top_idx, rank, counts, n_experts, tm)
    y = _experts(h, src_token, tile_expert, n_used, w_gate_up, w_down, tm, tf)
    return _combine(y, pos, top_w, x, gate, g_post, tm_out)


def _row_tile(n, want):
    return want if n % want == 0 else n


def kernel(x_prompt, x_sample, c_prompt, c_sample, cache_sb_k, cache_sb_v, cache_fox_k, cache_fox_v, cache_fox_logf, w_mod, b_mod, g_pre_mix, g_post_mix, g_pre_ffn, g_post_ffn, w_in, b_forget, w_sb_out, w_fox_out, w_out, w_ffn_gate_up, w_ffn_down, w_router, w_moe_gate_up, w_moe_down):
    bsz, seq, d = x_prompt.shape
    dec_b, dec_t, _ = x_sample.shape
    depth = w_mod.shape[0]
    n_past = cache_sb_k.shape[2]
    n_experts = w_router.shape[2]
    n_p, n_s = bsz * seq, dec_b * dec_t
    tq = _row_tile(seq, 256)
    past_chunk = _row_tile(n_past, 256)

    c_all = jnp.concatenate([c_prompt, c_sample], axis=0)
    c_rows = -(-c_all.shape[0] // 8) * 8
    c_all = jnp.pad(c_all, ((0, c_rows - c_all.shape[0]), (0, 0)))
    mod = _modulation(c_all, w_mod, b_mod)

    xp = x_prompt.reshape(n_p, d)
    xs = x_sample.reshape(n_s, d)
    stacked_p = stacked_s = None
    for l in range(depth):
        mod_p = mod[l, :bsz].reshape(bsz, 1, 6, d)
        mod_s = jnp.repeat(mod[l, bsz:bsz + dec_b].reshape(dec_b, 6, d), dec_t, axis=0)[None]
        mp = [mod_p[:, :, i] for i in range(6)]
        ms = [mod_s[:, :, i] for i in range(6)]
        vec = lambda a: a[l].reshape(1, -1)

        wqkv = w_in[l, :, :6 * WIDTH].astype(BF16)
        wf = jnp.pad(w_in[l, :, 6 * WIDTH:6 * WIDTH + N_HEADS], ((0, 0), (0, LANES - N_HEADS))).astype(BF16)
        wg = w_in[l, :, 6 * WIDTH + N_HEADS:].astype(BF16)
        bf = b_forget[l].reshape(1, N_HEADS)
        wsb, wfx, wo = w_sb_out[l].astype(BF16), w_fox_out[l].astype(BF16), w_out[l].astype(BF16)

        qkv_p, qvt_p, *stacked_p, sg_p, ksq_p = _in_projection(
            xp, vec(g_pre_mix), mp[0], mp[1], wqkv, wf, wg, bf, _row_tile(seq, 256), l, depth, stacked_p)
        lf_p = stacked_p[4][l]
        cum_p, key_cols_p = _forget_cumsum(jnp.swapaxes(lf_p.reshape(bsz, seq, N_HEADS), 1, 2), True)
        qkv_p3 = qkv_p.reshape(bsz, seq, 6 * WIDTH)
        osb_p = _sb_prompt(qkv_p3, qvt_p, tq)
        ofx_p = _fox_prompt(qkv_p3, qvt_p, cum_p, key_cols_p, jnp.sqrt(ksq_p.reshape(bsz, N_HEADS)),
                            _row_tile(seq, 512))
        xp = _mix_out(xp, osb_p.reshape(n_p, WIDTH), ofx_p.reshape(n_p, WIDTH), sg_p, mp[2], vec(g_post_mix),
                      wsb, wfx, wo, _row_tile(seq, 512))

        qkv_s, _, *stacked_s, sg_s, _ = _in_projection(
            xs, vec(g_pre_mix), ms[0], ms[1], wqkv, wf, wg, bf, n_s, l, depth, stacked_s)
        lf_s = stacked_s[4][l]
        t_all = n_past + dec_t
        t_pad = -(-t_all // CUMSUM_CHUNK) * CUMSUM_CHUNK
        lf_all = jnp.concatenate([cache_fox_logf[l].astype(F32), lf_s.reshape(dec_b, dec_t, N_HEADS),
                                  jnp.zeros((dec_b, t_pad - t_all, N_HEADS), F32)], axis=1)
        (cum_s,) = _forget_cumsum(jnp.swapaxes(lf_all, 1, 2), False)
        fk_s, fq_s = _pair_layouts(cum_s, n_past, dec_t)
        cache = lambda a: a[l].reshape(dec_b, n_past, WIDTH)
        osb_s, ofx_s = _sample_attention(qkv_s.reshape(dec_b, dec_t, 6 * WIDTH), cache(cache_sb_k), cache(cache_sb_v),
                                         cache(cache_fox_k), cache(cache_fox_v), fq_s, fk_s, past_chunk)
        xs = _mix_out(xs, osb_s.reshape(n_s, WIDTH), ofx_s.reshape(n_s, WIDTH), sg_s, ms[2], vec(g_post_mix),
                      wsb, wfx, wo, n_s)

        if l % 2 == 0:
            wgu, wd = w_ffn_gate_up[l // 2].astype(BF16), w_ffn_down[l // 2].astype(BF16)
            tf = _row_tile(wd.shape[0], 1408)
            xp = _ffn(xp, vec(g_pre_ffn), mp[3], mp[4], mp[5], vec(g_post_ffn), wgu, wd, _row_tile(seq, 512), tf)
            xs = _ffn(xs, vec(g_pre_ffn), ms[3], ms[4], ms[5], vec(g_post_ffn), wgu, wd, n_s, tf)
        else:
            wr = jnp.pad(w_router[l // 2], ((0, 0), (0, LANES - n_experts)))
            wgu, wd = w_moe_gate_up[l // 2].astype(BF16), w_moe_down[l // 2].astype(BF16)
            tf = _row_tile(wd.shape[1], 896)
            xp = _moe(xp, vec(g_pre_ffn), mp[3], mp[4], mp[5], vec(g_post_ffn), wr, wgu, wd, n_experts,
                      _row_tile(seq, 512), 1024, tf, _row_tile(seq, 256))
            xs = _moe(xs, vec(g_pre_ffn), ms[3], ms[4], ms[5], vec(g_post_ffn), wr, wgu, wd, n_experts,
                      n_s, 128, tf, n_s)

    split_p = [a.reshape(depth, bsz, seq, *a.shape[2:]) for a in stacked_p]
    split_s = [a.reshape(depth, dec_b, dec_t, *a.shape[2:]) for a in stacked_s]
    return (xp.reshape(bsz, seq, d), xs.reshape(dec_b, dec_t, d), *split_p, *split_s)
```
